```python
import functools
import jax, jax.numpy as jnp
from jax import lax
import numpy as np

D_MODEL = 1024
BATCH = 8
SEQ = 8192
DEPTH = 1
DEC_BATCH = 128
DEC_SEQ = 4
PAST_LEN = 8192
PAGE_SIZE = 128

D_ATTN = D_MODEL // 2
D_RNN = D_MODEL // 2
D_MIX = D_ATTN + D_RNN
HEAD_DIM = 64
N_HEADS = D_ATTN // HEAD_DIM
N_KV_HEADS = 2
N_IDX_HEADS = 8
IDX_HEAD_DIM = 32
TOPK_MAX = 256
Q_BLOCK = 128
ROPE_THETA = 10000.0
CONV_WIDTH = 4
RG_BLOCKS = 8
RG_BLOCK_W = D_RNN // RG_BLOCKS
RG_C = 8.0
N_GROUPS = 4
EXPERTS_PER_GROUP = 8
N_EXPERTS = N_GROUPS * EXPERTS_PER_GROUP
TOP_K_IN_GROUP = 2
D_EXPERT = 256
MOE_BLOCK = 128
NORM_EPS = 1e-6

D_Q = N_HEADS * HEAD_DIM
D_KV = N_KV_HEADS * HEAD_DIM
D_IQ = N_IDX_HEADS * IDX_HEAD_DIM
SPLITS = (D_Q, D_KV, D_KV, D_IQ, IDX_HEAD_DIM, N_IDX_HEADS, D_RNN, D_RNN)
D_IN = D_Q + 2 * D_KV + D_IQ + IDX_HEAD_DIM + N_IDX_HEADS + 2 * D_RNN

kernel_name = 'hymba_dsa_rglru_hmoe_step'


def rms_norm(x, g):
    xf = x.astype(jnp.float32)
    y = xf * lax.rsqrt(jnp.mean(xf * xf, axis=-1, keepdims=True) + NORM_EPS)
    return (y * g.astype(jnp.float32)).astype(x.dtype)


def rope(x, pos):
    half = x.shape[-1] // 2
    inv_freq = ROPE_THETA ** (-jnp.arange(half, dtype=jnp.float32) / half)
    ang = pos.astype(jnp.float32)[:, None] * inv_freq[None, :]
    cos = jnp.cos(ang)[None, :, None, :]
    sin = jnp.sin(ang)[None, :, None, :]
    xf = x.astype(jnp.float32)
    x1, x2 = xf[..., :half], xf[..., half:]
    return jnp.concatenate([x1 * cos - x2 * sin, x2 * cos + x1 * sin], axis=-1).astype(x.dtype)


def split_cols(z):
    outs = []
    off = 0
    for w in SPLITS:
        outs.append(z[..., off:off + w])
        off += w
    return outs


def gather_rows(rows, idx):
    return jax.vmap(lambda r, i: r[i])(rows, idx)


def mixer_inputs(h, w_in, pos):
    n, t, _ = h.shape
    q, k, v, iq, ik, iw, xr, yg = split_cols(h @ w_in)
    q = rope(q.reshape(n, t, N_HEADS, HEAD_DIM), pos)
    k = rope(k.reshape(n, t, N_KV_HEADS, HEAD_DIM), pos)
    v = v.reshape(n, t, N_KV_HEADS, HEAD_DIM)
    iq = rope(iq.reshape(n, t, N_IDX_HEADS, IDX_HEAD_DIM), pos)
    ik = rope(ik[:, :, None, :], pos)[:, :, 0, :]
    iw = iw * N_IDX_HEADS ** -0.5
    return q, k, v, iq, ik, iw, xr, yg


def indexer_scores(iq, iw, ik):
    dots = jnp.einsum('nqhd,nsd->nqhs', iq, ik).astype(jnp.float32) * IDX_HEAD_DIM ** -0.5
    return jnp.einsum('nqh,nqhs->nqs', iw.astype(jnp.float32), jax.nn.relu(dots))


def selected_attention(q, k_sel, v_sel, valid):
    n, nq = q.shape[0], q.shape[1]
    qg = q.reshape(n, nq, N_KV_HEADS, N_HEADS // N_KV_HEADS, HEAD_DIM)
    s = jnp.einsum('nqhgd,nqkhd->nqhgk', qg, k_sel).astype(jnp.float32) * HEAD_DIM ** -0.5
    s = jnp.where(valid[:, :, None, None, :], s, -jnp.inf)
    p = jax.nn.softmax(s, axis=-1).astype(v_sel.dtype)
    o = jnp.einsum('nqhgk,nqkhd->nqhgd', p, v_sel)
    return o.reshape(n, nq, D_ATTN)


def prompt_attention(q, k, v, iq, iw, ik):
    n, t = q.shape[0], q.shape[1]
    n_sel = min(TOPK_MAX, t // 4)
    n_blk = t // Q_BLOCK
    key_pos = jnp.arange(t)

    def to_blocks(a):
        return jnp.moveaxis(a.reshape((n, n_blk, Q_BLOCK) + a.shape[2:]), 1, 0)

    def block(args):
        qb, iqb, iwb, qpos = args
        scores = indexer_scores(iqb, iwb, ik)
        scores = jnp.where((key_pos[None, :] <= qpos[:, None])[None], scores, -jnp.inf)
        _, idx = lax.top_k(scores, n_sel)
        valid = idx <= qpos[None, :, None]
        return selected_attention(qb, gather_rows(k, idx), gather_rows(v, idx), valid)

    out = lax.map(block, (to_blocks(q), to_blocks(iq), to_blocks(iw),
                          key_pos.reshape(n_blk, Q_BLOCK)))
    return jnp.moveaxis(out, 0, 1).reshape(n, t, D_ATTN)


def sample_attention(cache_k_l, cache_v_l, cache_ik_l, page_table, q, k, v, iq, iw, ik):
    n, nq = q.shape[0], q.shape[1]
    past = page_table.shape[1] * PAGE_SIZE
    n_keys = past + nq
    n_sel = min(TOPK_MAX, n_keys // 4)
    ik_past = cache_ik_l[page_table].reshape(n, past, IDX_HEAD_DIM).astype(ik.dtype)
    scores = indexer_scores(iq, iw, jnp.concatenate([ik_past, ik], axis=1))
    qpos = past + jnp.arange(nq)
    key_pos = jnp.arange(n_keys)
    scores = jnp.where((key_pos[None, :] <= qpos[:, None])[None], scores, -jnp.inf)
    _, idx = lax.top_k(scores, n_sel)
    valid = idx <= qpos[None, :, None]
    in_past = (idx < past)[..., None, None]
    pidx = jnp.minimum(idx, past - 1)
    phys = jax.vmap(lambda pt, pg: pt[pg])(page_table, pidx // PAGE_SIZE)
    off = pidx % PAGE_SIZE
    nidx = jnp.clip(idx - past, 0, nq - 1)
    k_sel = jnp.where(in_past, cache_k_l[phys, off].astype(k.dtype), gather_rows(k, nidx))
    v_sel = jnp.where(in_past, cache_v_l[phys, off].astype(v.dtype), gather_rows(v, nidx))
    return selected_attention(q, k_sel, v_sel, valid)


def rglru_branch(xr, yg, conv_prev, h_prev, conv_w, conv_b, rg_wa, rg_ba, rg_wx, rg_bx, rg_lambda):
    n, t, _ = xr.shape
    xp = jnp.concatenate([conv_prev.astype(xr.dtype), xr], axis=1)
    xc = sum(conv_w[j] * xp[:, j:j + t] for j in range(CONV_WIDTH)) + conv_b
    conv_new = xp[:, t:]
    xb = xc.reshape(n, t, RG_BLOCKS, RG_BLOCK_W)
    r = jax.nn.sigmoid(jnp.einsum('ntkc,kcd->ntkd', xb, rg_wa).reshape(n, t, D_RNN) + rg_ba)
    i = jax.nn.sigmoid(jnp.einsum('ntkc,kcd->ntkd', xb, rg_wx).reshape(n, t, D_RNN) + rg_bx)
    log_a = -RG_C * r.astype(jnp.float32) * jax.nn.softplus(-rg_lambda.astype(jnp.float32))
    a = jnp.exp(log_a)
    b = jnp.sqrt(-jnp.expm1(2.0 * log_a)) * (i * xc).astype(jnp.float32)

    def step(hc, ab):
        a_t, b_t = ab
        hc = a_t * hc + b_t
        return hc, hc

    h_last, hs = lax.scan(step, h_prev.astype(jnp.float32),
                          (jnp.swapaxes(a, 0, 1), jnp.swapaxes(b, 0, 1)))
    y = jnp.swapaxes(hs, 0, 1).astype(xr.dtype) * jax.nn.gelu(yg)
    return y, conv_new, h_last.astype(h_prev.dtype)


def moe_ffn(h, router_grp_w, router_grp_b, router_exp_w, router_exp_b, exp_w_gate, exp_w_up, exp_w_down):
    n, t, d = h.shape
    n_tok = n * t
    hf = h.reshape(n_tok, d)
    gl = (hf @ router_grp_w).astype(jnp.float32) + router_grp_b.astype(jnp.float32)
    gp = jax.nn.softmax(gl, axis=-1)
    g = jnp.argmax(gl, axis=-1)
    el = ((hf @ router_exp_w).astype(jnp.float32) + router_exp_b.astype(jnp.float32)).reshape(
        n_tok, N_GROUPS, EXPERTS_PER_GROUP)
    el_g = jnp.take_along_axis(el, g[:, None, None], axis=1)[:, 0]
    top_l, top_j = lax.top_k(el_g, TOP_K_IN_GROUP)
    gate = jax.nn.softmax(top_l, axis=-1) * jnp.take_along_axis(gp, g[:, None], axis=1)
    expert = g[:, None] * EXPERTS_PER_GROUP + top_j
    m = n_tok * TOP_K_IN_GROUP
    flat_e = expert.reshape(m)
    flat_t = jnp.repeat(jnp.arange(n_tok, dtype=jnp.int32), TOP_K_IN_GROUP)
    flat_w = gate.reshape(m)
    order = jnp.argsort(flat_e)
    e_sorted = flat_e[order]
    counts = jnp.bincount(flat_e, length=N_EXPERTS)
    padded = (counts + MOE_BLOCK - 1) // MOE_BLOCK * MOE_BLOCK
    pad_end = jnp.cumsum(padded)
    pad_start = pad_end - padded
    start = jnp.cumsum(counts) - counts
    dest = pad_start[e_sorted] + jnp.arange(m) - start[e_sorted]
    n_blocks = -(-(m + N_EXPERTS * (MOE_BLOCK - 1)) // MOE_BLOCK)
    n_slots = n_blocks * MOE_BLOCK
    slot_tok = jnp.full((n_slots,), n_tok, dtype=jnp.int32).at[dest].set(flat_t[order])
    slot_w = jnp.zeros((n_slots,), jnp.float32).at[dest].set(flat_w[order])
    blk_e = jnp.minimum(jnp.searchsorted(pad_end, jnp.arange(n_blocks) * MOE_BLOCK, side='right'),
                        N_EXPERTS - 1)
    h_pad = jnp.concatenate([hf, jnp.zeros((1, d), hf.dtype)], axis=0)
    xs = h_pad[slot_tok].reshape(n_blocks, MOE_BLOCK, d)

    def expert_block(args):
        xb, e = args
        return (jax.nn.silu(xb @ exp_w_gate[e]) * (xb @ exp_w_up[e])) @ exp_w_down[e]

    yb = lax.map(expert_block, (xs, blk_e)).reshape(n_slots, d)
    yb = yb * slot_w[:, None].astype(yb.dtype)
    out = jax.ops.segment_sum(yb, slot_tok, num_segments=n_tok + 1)[:n_tok]
    return out.reshape(n, t, d)


def decoder_layer(x, c, pos, attend, conv_prev, h_prev,
                  ada_w, ada_b, norm1_g, w_in, conv_w, conv_b, rg_wa, rg_ba, rg_wx, rg_bx, rg_lambda,
                  attn_out_g, rnn_out_g, w_out, norm2_g, router_grp_w, router_grp_b,
                  router_exp_w, router_exp_b, exp_w_gate, exp_w_up, exp_w_down):
    mod = jax.nn.silu(c) @ ada_w + ada_b
    sh1, sc1, g1, sh2, sc2, g2 = [m[:, None, :] for m in jnp.split(mod, 6, axis=-1)]
    h = rms_norm(x, norm1_g) * (1.0 + sc1) + sh1
    q, k, v, iq, ik, iw, xr, yg = mixer_inputs(h, w_in, pos)
    attn = attend(q, k, v, iq, iw, ik)
    rnn, conv_new, h_new = rglru_branch(xr, yg, conv_prev, h_prev, conv_w, conv_b,
                                        rg_wa, rg_ba, rg_wx, rg_bx, rg_lambda)
    mixed = jnp.concatenate([rms_norm(attn, attn_out_g), rms_norm(rnn, rnn_out_g)], axis=-1) @ w_out
    x = x + g1 * mixed
    h2 = rms_norm(x, norm2_g) * (1.0 + sc2) + sh2
    x = x + g2 * moe_ffn(h2, router_grp_w, router_grp_b, router_exp_w, router_exp_b,
                         exp_w_gate, exp_w_up, exp_w_down)
    return x, k, v, ik, conv_new, h_new


def setup_inputs(seed: int = 0) -> dict:
    key = jax.random.key(seed)
    ks = iter(jax.random.split(key, 40))
    f32 = jnp.float32

    def nrm(shape, scale):
        return jax.random.normal(next(ks), shape, f32) * scale

    n_pages = PAST_LEN // PAGE_SIZE
    n_used = DEC_BATCH * n_pages
    n_pool = n_used + max(1, n_used // 4)
    page_table = jax.random.permutation(next(ks), n_pool)[:n_used].reshape(
        DEC_BATCH, n_pages).astype(jnp.int32)
    a0 = jax.random.uniform(next(ks), (DEPTH, D_RNN), f32, minval=0.9, maxval=0.999)
    return {
        'x_prompt': nrm((BATCH, SEQ, D_MODEL), 1.0),
        'x_sample': nrm((DEC_BATCH, DEC_SEQ, D_MODEL), 1.0),
        'cache_k': nrm((DEPTH, n_pool, PAGE_SIZE, N_KV_HEADS, HEAD_DIM), 1.0),
        'cache_v': nrm((DEPTH, n_pool, PAGE_SIZE, N_KV_HEADS, HEAD_DIM), 1.0),
        'cache_idx_k': nrm((DEPTH, n_pool, PAGE_SIZE, IDX_HEAD_DIM), 1.0),
        'state_conv': nrm((DEPTH, DEC_BATCH, CONV_WIDTH - 1, D_RNN), 1.0),
        'state_rglru': nrm((DEPTH, DEC_BATCH, D_RNN), 0.5),
        'page_table': page_table,
        'c_prompt': nrm((BATCH, D_MODEL), 1.0),
        'c_sample': nrm((DEC_BATCH, D_MODEL), 1.0),
        'ada_w': nrm((DEPTH, D_MODEL, 6 * D_MODEL), 0.5 * D_MODEL ** -0.5),
        'ada_b': nrm((DEPTH, 6 * D_MODEL), 0.02),
        'norm1_g': 1.0 + nrm((DEPTH, D_MODEL), 0.02),
        'w_in': nrm((DEPTH, D_MODEL, D_IN), D_MODEL ** -0.5),
        'conv_w': nrm((DEPTH, CONV_WIDTH, D_RNN), CONV_WIDTH ** -0.5),
        'conv_b': nrm((DEPTH, D_RNN), 0.02),
        'rg_wa': nrm((DEPTH, RG_BLOCKS, RG_BLOCK_W, RG_BLOCK_W), RG_BLOCK_W ** -0.5),
        'rg_ba': nrm((DEPTH, D_RNN), 0.02),
        'rg_wx': nrm((DEPTH, RG_BLOCKS, RG_BLOCK_W, RG_BLOCK_W), RG_BLOCK_W ** -0.5),
        'rg_bx': nrm((DEPTH, D_RNN), 0.02),
        'rg_lambda': jnp.log(a0) - jnp.log1p(-a0),
        'attn_out_g': 1.0 + nrm((DEPTH, D_ATTN), 0.02),
        'rnn_out_g': 1.0 + nrm((DEPTH, D_RNN), 0.02),
        'w_out': nrm((DEPTH, D_MIX, D_MODEL), D_MIX ** -0.5),
        'norm2_g': 1.0 + nrm((DEPTH, D_MODEL), 0.02),
        'router_grp_w': nrm((DEPTH, D_MODEL, N_GROUPS), D_MODEL ** -0.5),
        'router_grp_b': nrm((DEPTH, N_GROUPS), 0.01),
        'router_exp_w': nrm((DEPTH, D_MODEL, N_EXPERTS), D_MODEL ** -0.5),
        'router_exp_b': nrm((DEPTH, N_EXPERTS), 0.01),
        'exp_w_gate': nrm((DEPTH, N_EXPERTS, D_MODEL, D_EXPERT), D_MODEL ** -0.5),
        'exp_w_up': nrm((DEPTH, N_EXPERTS, D_MODEL, D_EXPERT), D_MODEL ** -0.5),
        'exp_w_down': nrm((DEPTH, N_EXPERTS, D_EXPERT, D_MODEL), D_EXPERT ** -0.5),
        'final_g': 1.0 + nrm((D_MODEL,), 0.02),
    }


def reference(x_prompt, x_sample, cache_k, cache_v, cache_idx_k, state_conv, state_rglru, page_table,
              c_prompt, c_sample, ada_w, ada_b, norm1_g, w_in, conv_w, conv_b, rg_wa, rg_ba, rg_wx,
              rg_bx, rg_lambda, attn_out_g, rnn_out_g, w_out, norm2_g, router_grp_w, router_grp_b,
              router_exp_w, router_exp_b, exp_w_gate, exp_w_up, exp_w_down, final_g):
    n_p, t_p = x_prompt.shape[0], x_prompt.shape[1]
    n_s, t_s = x_sample.shape[0], x_sample.shape[1]
    past = page_table.shape[1] * PAGE_SIZE
    pos_p = jnp.arange(t_p)
    pos_s = past + jnp.arange(t_s)
    xp, xs = x_prompt, x_sample
    kp, vp, ikp, cvp, hp = [], [], [], [], []
    ksl, vsl, iks, cvs, hs = [], [], [], [], []
    for l in range(DEPTH):
        wl = (ada_w[l], ada_b[l], norm1_g[l], w_in[l], conv_w[l], conv_b[l], rg_wa[l], rg_ba[l],
              rg_wx[l], rg_bx[l], rg_lambda[l], attn_out_g[l], rnn_out_g[l], w_out[l], norm2_g[l],
              router_grp_w[l], router_grp_b[l], router_exp_w[l], router_exp_b[l],
              exp_w_gate[l], exp_w_up[l], exp_w_down[l])
        conv0 = jnp.zeros((n_p, CONV_WIDTH - 1, D_RNN), x_prompt.dtype)
        h0 = jnp.zeros((n_p, D_RNN), x_prompt.dtype)
        xp, k1, v1, ik1, cv1, h1 = decoder_layer(xp, c_prompt, pos_p, prompt_attention, conv0, h0, *wl)
        attend_s = functools.partial(sample_attention, cache_k[l], cache_v[l], cache_idx_k[l], page_table)
        xs, k2, v2, ik2, cv2, h2 = decoder_layer(xs, c_sample, pos_s, attend_s,
                                                 state_conv[l], state_rglru[l], *wl)
        kp.append(k1); vp.append(v1); ikp.append(ik1); cvp.append(cv1); hp.append(h1)
        ksl.append(k2); vsl.append(v2); iks.append(ik2); cvs.append(cv2); hs.append(h2)
    y_prompt = rms_norm(xp, final_g)
    y_sample = rms_norm(xs, final_g)
    return (y_prompt, y_sample,
            jnp.stack(kp), jnp.stack(vp), jnp.stack(ikp), jnp.stack(cvp), jnp.stack(hp),
            jnp.stack(ksl), jnp.stack(vsl), jnp.stack(iks), jnp.stack(cvs), jnp.stack(hs))
```

```python
import functools

import jax
import jax.numpy as jnp
from jax import lax
from jax.experimental import pallas as pl
from jax.experimental.pallas import tpu as pltpu

F32, BF16, I32 = jnp.float32, jnp.bfloat16, jnp.int32
HIGHEST = lax.Precision.HIGHEST

HEAD_DIM = 64
N_HEADS = 8
N_KV_HEADS = 2
HEADS_PER_KV = N_HEADS // N_KV_HEADS
N_IDX_HEADS = 8
IDX_HEAD_DIM = 32
TOPK_MAX = 256
ROPE_THETA = 10000.0
CONV_WIDTH = 4
RG_BLOCKS = 8
RG_C = 8.0
N_GROUPS = 4
EXPERTS_PER_GROUP = 8
N_EXPERTS = N_GROUPS * EXPERTS_PER_GROUP
PAGE_SIZE = 128
NORM_EPS = 1e-6

LANES = 128
SUBLANES = 8
VMEM_LIMIT = 48 * 1024 * 1024
INT_MIN = -(2 ** 31)
NEG_BIG = -1e30
EXPERT_LANE0 = N_GROUPS


def _cparams(*sem):
    return pltpu.CompilerParams(dimension_semantics=sem, vmem_limit_bytes=VMEM_LIMIT)


def _rms(x, g):
    return x * lax.rsqrt(jnp.mean(x * x, axis=-1, keepdims=True) + NORM_EPS) * g


def _sigmoid(x):
    return 1.0 / (1.0 + jnp.exp(-x))


def _float_key(s):
    b = pltpu.bitcast(s + 0.0, I32)
    return b ^ ((b >> 31) & 0x7FFFFFFF)


def _ada_body(c_ref, w_ref, b_ref, o_ref):
    c = c_ref[...]
    s = c * _sigmoid(c)
    o_ref[...] = jnp.dot(s, w_ref[...], preferred_element_type=F32, precision=HIGHEST) + b_ref[...]


def ada_mod(c, w, b):
    m, d = c.shape
    n_chunks = w.shape[1] // d
    return pl.pallas_call(
        _ada_body,
        grid=(n_chunks,),
        in_specs=[pl.BlockSpec((m, d), lambda j: (0, 0)),
                  pl.BlockSpec((d, d), lambda j: (0, j)),
                  pl.BlockSpec((1, d), lambda j: (0, j))],
        out_specs=pl.BlockSpec((m, d), lambda j: (0, j)),
        out_shape=jax.ShapeDtypeStruct((m, w.shape[1]), F32),
        compiler_params=_cparams("arbitrary"),
        name="ada_mod",
    )(c, w, b.reshape(1, -1))


_C_Q, _C_QR, _C_K, _C_KR, _C_V, _C_IQ, _C_IQR, _C_IKW, _C_IKR, _C_XR, _C_YG, _C_END = (
    0, 512, 1024, 1152, 1280, 1408, 1664, 1920, 2048, 2176, 2688, 3200)


def _rot_half_cols(w, head_dim):
    d_in, n = w.shape
    wh = w.reshape(d_in, n // head_dim, 2, head_dim // 2)
    return jnp.stack([-wh[:, :, 1], wh[:, :, 0]], axis=2).reshape(d_in, n)


def prep_w_in(w_in):
    d_in = w_in.shape[0]
    q, k, v, iq, ik, iw, xr, yg = jnp.split(w_in, [512, 640, 768, 1024, 1056, 1064, 1576], axis=1)
    z = lambda n: jnp.zeros((d_in, n), w_in.dtype)
    ikw = jnp.concatenate([ik, iw, z(LANES - IDX_HEAD_DIM - N_IDX_HEADS)], axis=1)
    ikr = jnp.concatenate([_rot_half_cols(ik, IDX_HEAD_DIM), z(LANES - IDX_HEAD_DIM)], axis=1)
    w = jnp.concatenate([q, _rot_half_cols(q, HEAD_DIM), k, _rot_half_cols(k, HEAD_DIM), v,
                         iq, _rot_half_cols(iq, IDX_HEAD_DIM), ikw, ikr, xr, yg], axis=1)
    return w.astype(BF16)


def rope_tables(pos):
    pos = pos.astype(F32)[:, None]

    def cs(half, reps):
        inv = ROPE_THETA ** (-jnp.arange(half, dtype=F32) / half)
        ang = pos * inv[None, :]
        return jnp.tile(jnp.cos(ang), (1, 2 * reps)), jnp.tile(jnp.sin(ang), (1, 2 * reps))

    cq, sq = cs(HEAD_DIM // 2, LANES // HEAD_DIM)
    ci, si = cs(IDX_HEAD_DIM // 2, LANES // IDX_HEAD_DIM)
    lane = jnp.arange(LANES)[None, :]
    w_scale = (IDX_HEAD_DIM ** -0.5) * (N_IDX_HEADS ** -0.5)
    cx = jnp.where(lane < IDX_HEAD_DIM, ci, jnp.where(lane < IDX_HEAD_DIM + N_IDX_HEADS, w_scale, 0.0))
    sx = jnp.where(lane < IDX_HEAD_DIM, si, 0.0)
    return cq, sq, ci, si, cx.astype(F32), sx.astype(F32)


def _in_proj_body(x_ref, sc_ref, sh_ref, g_ref, w_ref, cq_ref, sq_ref, ci_ref, si_ref, cx_ref, sx_ref,
                  q_ref, k_ref, v_ref, kb_ref, vt_ref, iq_ref, ikw_ref, ikb_ref, xr_ref, yg_ref):
    x = x_ref[...]
    h = (_rms(x, g_ref[...]) * (1.0 + sc_ref[...]) + sh_ref[...]).astype(BF16)

    def seg(a, b):
        return jnp.dot(h, w_ref[:, a:b], preferred_element_type=F32)

    cq, sq = cq_ref[...], sq_ref[...]
    cq4 = jnp.concatenate([cq] * 4, axis=1)
    sq4 = jnp.concatenate([sq] * 4, axis=1)
    q = (seg(_C_Q, _C_QR) * cq4 + seg(_C_QR, _C_K) * sq4) * (HEAD_DIM ** -0.5)
    q_ref[...] = q.astype(BF16)
    k = seg(_C_K, _C_KR) * cq + seg(_C_KR, _C_V) * sq
    k_ref[...] = k
    kb_ref[...] = k.astype(BF16)
    v = seg(_C_V, _C_IQ)
    v_ref[...] = v
    vt_ref[...] = v.T.astype(BF16)
    ci2 = jnp.concatenate([ci_ref[...]] * 2, axis=1)
    si2 = jnp.concatenate([si_ref[...]] * 2, axis=1)
    iq_ref[...] = (seg(_C_IQ, _C_IQR) * ci2 + seg(_C_IQR, _C_IKW) * si2).astype(BF16)
    ikw = seg(_C_IKW, _C_IKR) * cx_ref[...] + seg(_C_IKR, _C_XR) * sx_ref[...]
    ikw_ref[...] = ikw
    ikb_ref[...] = ikw.astype(BF16)
    xr_ref[...] = seg(_C_XR, _C_YG)
    yg_ref[...] = seg(_C_YG, _C_END)


def in_proj(x, sc, sh, g, w, tables, tt):
    n, t, d = x.shape
    per_row = sc.shape[1] != 1
    mod_spec = (pl.BlockSpec((None, tt, d), lambda b, i: (b, i, 0)) if per_row
                else pl.BlockSpec((None, 1, d), lambda b, i: (b, 0, 0)))
    row = lambda w_: pl.BlockSpec((None, tt, w_), lambda b, i: (b, i, 0))
    tab = pl.BlockSpec((tt, LANES), lambda b, i: (i, 0))
    outs = [(512, BF16), (128, F32), (128, F32), (128, BF16), None, (256, BF16), (128, F32), (128, BF16),
            (512, F32), (512, F32)]
    out_shape, out_specs = [], []
    for o in outs:
        if o is None:
            out_shape.append(jax.ShapeDtypeStruct((n, LANES, t), BF16))
            out_specs.append(pl.BlockSpec((None, LANES, tt), lambda b, i: (b, 0, i)))
        else:
            out_shape.append(jax.ShapeDtypeStruct((n, t, o[0]), o[1]))
            out_specs.append(row(o[0]))
    return pl.pallas_call(
        _in_proj_body,
        grid=(n, t // tt),
        in_specs=[row(d), mod_spec, mod_spec,
                  pl.BlockSpec((1, d), lambda b, i: (0, 0)),
                  pl.BlockSpec(w.shape, lambda b, i: (0, 0))] + [tab] * 6,
        out_specs=out_specs,
        out_shape=out_shape,
        compiler_params=_cparams("parallel", "arbitrary"),
        name="in_proj",
    )(x, sc, sh, g.reshape(1, d), w, *tables)


def _pattn_body(q_ref, iq_ref, iwt_ref, k_ref, vt_ref, ik_ref, o_ref,
                s_ref, p_ref, m_ref, l_ref, acc_ref, *, n_sel, idx_bits):
    qi = pl.program_id(1)
    tq = q_ref.shape[0]
    tk = tq
    n_tiles = qi + 1

    iq_t = iq_ref[...].astype(F32).T
    iq_st = jnp.concatenate([iq_t[IDX_HEAD_DIM * h:IDX_HEAD_DIM * (h + 1)] for h in range(N_IDX_HEADS)],
                            axis=1).astype(BF16)
    w_t = iwt_ref[...]
    q_t = q_ref[...].astype(F32).T
    zero_half = jnp.zeros((HEAD_DIM, tq), F32)
    q_pad = []
    for g in range(N_KV_HEADS):
        cols = []
        for hh in range(HEADS_PER_KV):
            h = g * HEADS_PER_KV + hh
            qh = q_t[HEAD_DIM * h:HEAD_DIM * (h + 1)]
            cols.append(jnp.concatenate([qh, zero_half] if g == 0 else [zero_half, qh], axis=0))
        q_pad.append(jnp.concatenate(cols, axis=1).astype(BF16))

    row_s = lax.broadcasted_iota(I32, (tk, tq), 0)
    col_t = lax.broadcasted_iota(I32, (tk, tq), 1)
    causal = row_s <= col_t

    def tile_off(j):
        return pl.multiple_of(j * tk, tk)

    def score_keys(j):
        ik = ik_ref[pl.ds(tile_off(j), tk), :][:, :IDX_HEAD_DIM]
        d = jnp.dot(ik, iq_st, preferred_element_type=F32)
        sc = w_t[0:1, :] * jnp.maximum(d[:, 0:tq], 0.0)
        for h in range(1, N_IDX_HEADS):
            sc = sc + w_t[h:h + 1, :] * jnp.maximum(d[:, h * tq:(h + 1) * tq], 0.0)
        return _float_key(sc)

    def p1(j, c):
        s_ref[pl.ds(tile_off(j), tk), :] = score_keys(j)
        return c

    lax.fori_loop(0, qi, p1, 0)
    s_ref[pl.ds(tile_off(qi), tk), :] = jnp.where(causal, score_keys(qi), INT_MIN)

    def count(pred):
        def body(j, acc):
            t = s_ref[pl.ds(tile_off(j), tk), :].reshape(tk // SUBLANES, SUBLANES, tq)
            return acc + jnp.sum(pred(t, j), axis=0)
        acc = lax.fori_loop(0, n_tiles, body, jnp.zeros((SUBLANES, tq), F32))
        return jnp.sum(acc, axis=0, keepdims=True)

    def bisect(b, thr):
        cand = thr + jnp.left_shift(jnp.int32(1), 31 - b)
        cand_b = jnp.broadcast_to(cand, (SUBLANES, tq))[None]
        c = count(lambda t, j: jnp.where(t >= cand_b, 1.0, 0.0))
        return jnp.where(c >= n_sel, cand, thr)

    thr = lax.fori_loop(0, 32, bisect, jnp.full((1, tq), INT_MIN, I32))
    thr_b = jnp.broadcast_to(thr, (SUBLANES, tq))[None]
    c_gt = count(lambda t, j: jnp.where(t > thr_b, 1.0, 0.0))
    c_ge = count(lambda t, j: jnp.where(t >= thr_b, 1.0, 0.0))
    ties_left = n_sel - c_gt
    p_ref[...] = jnp.full((1, tq), 2 ** idx_bits - 1, I32)
    excess = jnp.max(jnp.where(thr > INT_MIN, c_ge, 0.0)) > n_sel

    @pl.when(excess)
    def _():
        sub = lax.broadcasted_iota(I32, (tk // SUBLANES, SUBLANES, tq), 0) * SUBLANES + \
            lax.broadcasted_iota(I32, (tk // SUBLANES, SUBLANES, tq), 1)

        def tb(b, p):
            cand = p | jnp.left_shift(jnp.int32(1), idx_bits - 1 - b)
            cand_b = jnp.broadcast_to(cand, (SUBLANES, tq))[None]
            c = count(lambda t, j: jnp.where(t == thr_b, jnp.where(sub + j * tk < cand_b, 1.0, 0.0), 0.0))
            return jnp.where(c < ties_left, cand, p)

        p_ref[...] = lax.fori_loop(0, idx_bits, tb, jnp.zeros((1, tq), I32))

    p_last = p_ref[...]

    m_ref[...] = jnp.full(m_ref.shape, NEG_BIG, F32)
    l_ref[...] = jnp.zeros(l_ref.shape, F32)
    acc_ref[...] = jnp.zeros(acc_ref.shape, F32)

    def attend(j, diag):
        off = tile_off(j)
        t = s_ref[pl.ds(off, tk), :]
        keep_tie = jnp.where(row_s + j * tk <= p_last, 0.0, NEG_BIG)
        bias = jnp.where(t > thr, 0.0, jnp.where(t == thr, keep_tie, NEG_BIG))
        if diag:
            bias = jnp.where(causal, bias, NEG_BIG)
        bias4 = jnp.concatenate([bias] * HEADS_PER_KV, axis=1)
        kt = k_ref[pl.ds(off, tk), :]
        vt = vt_ref[:, pl.ds(off, tk)]
        for g in range(N_KV_HEADS):
            s = jnp.dot(kt, q_pad[g], preferred_element_type=F32) + bias4
            m_old = m_ref[g:g + 1, :]
            m_new = jnp.maximum(m_old, jnp.max(s, axis=0, keepdims=True))
            alpha = jnp.exp(m_old - m_new)
            p = jnp.exp(s - m_new)
            l_ref[g:g + 1, :] = alpha * l_ref[g:g + 1, :] + jnp.sum(p, axis=0, keepdims=True)
            acc_ref[g] = alpha * acc_ref[g] + jnp.dot(vt, p.astype(BF16), preferred_element_type=F32)
            m_ref[g:g + 1, :] = m_new

    def p3(j, c):
        attend(j, False)
        return c

    lax.fori_loop(0, qi, p3, 0)
    attend(qi, True)

    rows = []
    for g in range(N_KV_HEADS):
        o_g = acc_ref[g][HEAD_DIM * g:HEAD_DIM * (g + 1), :] / l_ref[g:g + 1, :]
        rows += [o_g[:, hh * tq:(hh + 1) * tq] for hh in range(HEADS_PER_KV)]
    o_ref[...] = jnp.concatenate(rows, axis=0).T


def prompt_attention(q, iq, iw_t, kb, vt, ikb, tq):
    n, t, _ = q.shape
    n_sel = min(TOPK_MAX, t // 4)
    idx_bits = max(1, (t - 1).bit_length())
    body = functools.partial(_pattn_body, n_sel=n_sel, idx_bits=idx_bits)
    return pl.pallas_call(
        body,
        grid=(n, t // tq),
        in_specs=[pl.BlockSpec((None, tq, q.shape[2]), lambda b, i: (b, i, 0)),
                  pl.BlockSpec((None, tq, iq.shape[2]), lambda b, i: (b, i, 0)),
                  pl.BlockSpec((None, N_IDX_HEADS, tq), lambda b, i: (b, 0, i)),
                  pl.BlockSpec((None, t, LANES), lambda b, i: (b, 0, 0)),
                  pl.BlockSpec((None, LANES, t), lambda b, i: (b, 0, 0)),
                  pl.BlockSpec((None, t, LANES), lambda b, i: (b, 0, 0))],
        out_specs=pl.BlockSpec((None, tq, N_HEADS * HEAD_DIM), lambda b, i: (b, i, 0)),
        out_shape=jax.ShapeDtypeStruct((n, t, N_HEADS * HEAD_DIM), F32),
        scratch_shapes=[pltpu.VMEM((t, tq), I32),
                        pltpu.VMEM((1, tq), I32),
                        pltpu.VMEM((N_KV_HEADS, HEADS_PER_KV * tq), F32),
                        pltpu.VMEM((N_KV_HEADS, HEADS_PER_KV * tq), F32),
                        pltpu.VMEM((N_KV_HEADS, LANES, HEADS_PER_KV * tq), F32)],
        compiler_params=_cparams("parallel", "arbitrary"),
        name="prompt_attention",
    )(q, iq, iw_t, kb, vt, ikb)


Q_PAD = SUBLANES


def _sattn_body(pt_ref, q_ref, iq_ref, iw_ref, kn_ref, vn_ref, ikn_ref, ck_hbm, cv_hbm, cik_hbm, o_ref,
                kbuf, vbuf, ikbuf, s_ref, sem, *, n_pages, n_sel, idx_bits):
    b = pl.program_id(0)
    past = n_pages * PAGE_SIZE
    lk = past + LANES

    streams = ((cik_hbm, ikbuf), (ck_hbm, kbuf), (cv_hbm, vbuf))

    def page_copy(kind, p):
        src, dst = streams[kind]
        return pltpu.make_async_copy(src.at[pt_ref[b, p]], dst.at[pl.ds(p * PAGE_SIZE, PAGE_SIZE)], sem.at[kind])

    for kind in range(len(streams)):
        for p in range(n_pages):
            page_copy(kind, p).start()

    kbuf[pl.ds(past, LANES), :] = jnp.zeros((LANES, LANES), F32)
    vbuf[pl.ds(past, LANES), :] = jnp.zeros((LANES, LANES), F32)
    ikbuf[pl.ds(past, LANES), :] = jnp.zeros((LANES, IDX_HEAD_DIM), F32)
    kbuf[pl.ds(past, Q_PAD), :] = kn_ref[...]
    vbuf[pl.ds(past, Q_PAD), :] = vn_ref[...]
    ikbuf[pl.ds(past, Q_PAD), :] = ikn_ref[...][:, :IDX_HEAD_DIM]

    iq = iq_ref[...].astype(F32)
    iq_st = jnp.concatenate([iq[:, IDX_HEAD_DIM * h:IDX_HEAD_DIM * (h + 1)] for h in range(N_IDX_HEADS)],
                            axis=0).astype(BF16)
    qf = q_ref[...].astype(F32)
    lane128 = lax.broadcasted_iota(I32, (Q_PAD, LANES), 1)
    q_rows = []
    for h in range(N_HEADS):
        g = h // HEADS_PER_KV
        slab = qf[:, LANES * (h // 2):LANES * (h // 2 + 1)]
        if (h % 2) != g:
            slab = pltpu.roll(slab, HEAD_DIM, axis=1)
        keep = (lane128 < HEAD_DIM) if g == 0 else (lane128 >= HEAD_DIM)
        q_rows.append(jnp.where(keep, slab, 0.0))
    q_pad = jnp.concatenate(q_rows, axis=0).astype(BF16)
    iw = iw_ref[...]

    key_pos = lax.broadcasted_iota(I32, (Q_PAD, lk), 1)
    q_row = lax.broadcasted_iota(I32, (Q_PAD, lk), 0)
    admissible = key_pos <= past + q_row

    for p in range(n_pages):
        page_copy(0, p).wait()
    ikb = ikbuf[...].astype(BF16)
    d = lax.dot_general(iq_st, ikb, (((1,), (1,)), ((), ())), preferred_element_type=F32)
    sc = None
    for h in range(N_IDX_HEADS):
        w_h = iw[:, IDX_HEAD_DIM + h:IDX_HEAD_DIM + h + 1]
        term = w_h * jnp.maximum(d[Q_PAD * h:Q_PAD * (h + 1)], 0.0)
        sc = term if sc is None else sc + term
    s_ref[...] = jnp.where(admissible, _float_key(sc), INT_MIN)

    def count(pred):
        return jnp.sum(pred(s_ref[...]), axis=1, keepdims=True)

    def bisect(i, thr):
        cand = thr + jnp.left_shift(jnp.int32(1), 31 - i)
        c = count(lambda t: jnp.where(t >= cand, 1.0, 0.0))
        return jnp.where(c >= n_sel, cand, thr)

    thr = lax.fori_loop(0, 32, bisect, jnp.full((Q_PAD, 1), INT_MIN, I32))
    ties_left = n_sel - count(lambda t: jnp.where(t > thr, 1.0, 0.0))

    def tb(i, pidx):
        cand = pidx | jnp.left_shift(jnp.int32(1), idx_bits - 1 - i)
        c = count(lambda t: jnp.where(t == thr, jnp.where(key_pos < cand, 1.0, 0.0), 0.0))
        return jnp.where(c < ties_left, cand, pidx)

    p_last = lax.fori_loop(0, idx_bits, tb, jnp.zeros((Q_PAD, 1), I32))
    t = s_ref[...]
    keep_tie = jnp.where(key_pos <= p_last, 0.0, NEG_BIG)
    bias = jnp.where(t > thr, 0.0, jnp.where(t == thr, keep_tie, NEG_BIG))
    bias = jnp.where(admissible, bias, NEG_BIG)

    for p in range(n_pages):
        page_copy(1, p).wait()
    kb = kbuf[...].astype(BF16)
    s = lax.dot_general(q_pad, kb, (((1,), (1,)), ((), ())), preferred_element_type=F32)
    s = s + jnp.concatenate([bias] * N_HEADS, axis=0)
    m = jnp.max(s, axis=1, keepdims=True)
    pexp = jnp.exp(s - m)
    l = jnp.sum(pexp, axis=1, keepdims=True)
    for p in range(n_pages):
        page_copy(2, p).wait()
    o = jnp.dot(pexp.astype(BF16), vbuf[...].astype(BF16), preferred_element_type=F32)
    o_ref[...] = o / l


def sample_attention(page_table, q, iq, ikw, k_new, v_new, cache_k, cache_v, cache_ik):
    n, n_new, _ = q.shape
    assert n_new <= Q_PAD
    n_pages = page_table.shape[1]
    past = n_pages * PAGE_SIZE
    lk = past + LANES
    n_sel = min(TOPK_MAX, (past + n_new) // 4)
    idx_bits = max(1, (lk - 1).bit_length())
    body = functools.partial(_sattn_body, n_pages=n_pages, n_sel=n_sel, idx_bits=idx_bits)
    pad = lambda a: jnp.pad(a, ((0, 0), (0, Q_PAD - n_new), (0, 0)))
    q, iq, ikw, k_new, v_new = pad(q), pad(iq), pad(ikw), pad(k_new), pad(v_new)
    seq = lambda w_: pl.BlockSpec((None, Q_PAD, w_), lambda b, pt: (b, 0, 0))
    hbm = pl.BlockSpec(memory_space=pl.ANY)
    grid_spec = pltpu.PrefetchScalarGridSpec(
        num_scalar_prefetch=1,
        grid=(n,),
        in_specs=[seq(q.shape[2]), seq(iq.shape[2]), seq(LANES), seq(LANES), seq(LANES), seq(LANES), hbm, hbm, hbm],
        out_specs=pl.BlockSpec((None, N_HEADS * Q_PAD, LANES), lambda b, pt: (b, 0, 0)),
        scratch_shapes=[pltpu.VMEM((lk, LANES), F32),
                        pltpu.VMEM((lk, LANES), F32),
                        pltpu.VMEM((lk, IDX_HEAD_DIM), F32),
                        pltpu.VMEM((Q_PAD, lk), I32),
                        pltpu.SemaphoreType.DMA((3,))],
    )
    raw = pl.pallas_call(
        body,
        grid_spec=grid_spec,
        out_shape=jax.ShapeDtypeStruct((n, N_HEADS * Q_PAD, LANES), F32),
        compiler_params=_cparams("arbitrary"),
        name="sample_attention",
    )(page_table, q, iq, ikw, k_new, v_new, ikw, cache_k, cache_v, cache_ik)
    raw = raw.reshape(n, N_HEADS, Q_PAD, N_KV_HEADS, HEAD_DIM)[:, :, :n_new]
    per_head = jnp.stack([raw[:, h, :, h // HEADS_PER_KV] for h in range(N_HEADS)], axis=2)
    return per_head.reshape(n, n_new, N_HEADS * HEAD_DIM)


def _rglru_body(xr_ref, yg_ref, cprev_ref, hprev_ref, cw_ref, cb_ref, wa_ref, ba_ref, wx_ref, bx_ref, lam_ref,
                y_ref, hlast_ref, xbuf, a_buf, b_buf, h_buf, h_carry):
    i = pl.program_id(1)
    tt = xr_ref.shape[0]
    halo = SUBLANES

    @pl.when(i == 0)
    def _():
        xbuf[pl.ds(0, halo), :] = jnp.zeros((halo, xbuf.shape[1]), F32)
        xbuf[pl.ds(halo - (CONV_WIDTH - 1), CONV_WIDTH - 1), :] = cprev_ref[...]
        h_carry[...] = hprev_ref[...]

    x = xr_ref[...]
    xbuf[pl.ds(halo, tt), :] = x
    xc = cb_ref[...] + cw_ref[CONV_WIDTH - 1:CONV_WIDTH, :] * x
    for j in range(CONV_WIDTH - 1):
        xc = xc + cw_ref[j:j + 1, :] * xbuf[pl.ds(halo - (CONV_WIDTH - 1) + j, tt), :]
    xcb = xc.astype(BF16)
    r = _sigmoid(jnp.dot(xcb, wa_ref[...], preferred_element_type=F32) + ba_ref[...])
    ig = _sigmoid(jnp.dot(xcb, wx_ref[...], preferred_element_type=F32) + bx_ref[...])
    z = -lam_ref[...]
    softplus = jnp.maximum(z, 0.0) + jnp.log1p(jnp.exp(-jnp.abs(z)))
    log_a = -RG_C * r * softplus
    a = jnp.exp(log_a)
    a_buf[...] = a
    b_buf[...] = jnp.sqrt(jnp.tanh(-log_a) * (1.0 + a * a)) * (ig * xc)

    def step(t, h):
        h = a_buf[pl.ds(t, 1), :] * h + b_buf[pl.ds(t, 1), :]
        h_buf[pl.ds(t, 1), :] = h
        return h

    h_fin = lax.fori_loop(0, tt, step, h_carry[...], unroll=min(8, tt))
    h_carry[...] = h_fin
    hlast_ref[...] = h_fin
    y_ref[...] = h_buf[...] * jax.nn.gelu(yg_ref[...])
    xbuf[pl.ds(halo - (CONV_WIDTH - 1), CONV_WIDTH - 1), :] = xbuf[pl.ds(halo + tt - (CONV_WIDTH - 1), CONV_WIDTH - 1), :]


def _block_diag(w):
    k, c, d = w.shape
    eye = jnp.eye(k, dtype=w.dtype)
    return (eye[:, None, :, None] * w[:, :, None, :]).reshape(k * c, k * d)


def rglru(xr, yg, conv_prev, h_prev, conv_w, conv_b, rg_wa, rg_ba, rg_wx, rg_bx, rg_lambda, tt):
    n, t, c = xr.shape
    row = pl.BlockSpec((None, tt, c), lambda b, i: (b, i, 0))
    const = lambda shape: pl.BlockSpec(shape, lambda b, i: (0,) * len(shape))
    vec = lambda a: a.reshape(1, c)
    return pl.pallas_call(
        _rglru_body,
        grid=(n, t // tt),
        in_specs=[row, row,
                  pl.BlockSpec((None, CONV_WIDTH - 1, c), lambda b, i: (b, 0, 0)),
                  pl.BlockSpec((None, 1, c), lambda b, i: (b, 0, 0)),
                  const((CONV_WIDTH, c)), const((1, c)), const((c, c)), const((1, c)), const((c, c)),
                  const((1, c)), const((1, c))],
        out_specs=[row, pl.BlockSpec((None, 1, c), lambda b, i: (b, 0, 0))],
        out_shape=[jax.ShapeDtypeStruct((n, t, c), F32), jax.ShapeDtypeStruct((n, 1, c), F32)],
        scratch_shapes=[pltpu.VMEM((tt + SUBLANES, c), F32), pltpu.VMEM((tt, c), F32), pltpu.VMEM((tt, c), F32),
                        pltpu.VMEM((tt, c), F32), pltpu.VMEM((1, c), F32)],
        compiler_params=_cparams("parallel", "arbitrary"),
        name="rglru",
    )(xr, yg, conv_prev, h_prev.reshape(n, 1, c), conv_w, vec(conv_b), _block_diag(rg_wa).astype(BF16), vec(rg_ba),
      _block_diag(rg_wx).astype(BF16), vec(rg_bx), vec(rg_lambda))


def _out_router_body(attn_ref, rnn_ref, x_ref, g1_ref, sc2_ref, sh2_ref, ag_ref, rg_ref, wo_ref, n2_ref,
                     rw_ref, rb_ref, x1_ref, h2_ref, gate_ref):
    mixed = jnp.concatenate([_rms(attn_ref[...], ag_ref[...]), _rms(rnn_ref[...], rg_ref[...])], axis=1)
    x1 = x_ref[...] + g1_ref[...] * jnp.dot(mixed.astype(BF16), wo_ref[...], preferred_element_type=F32)
    x1_ref[...] = x1
    h2 = _rms(x1, n2_ref[...]) * (1.0 + sc2_ref[...]) + sh2_ref[...]
    h2_ref[...] = h2.astype(BF16)
    z = jnp.dot(h2, rw_ref[...], preferred_element_type=F32, precision=HIGHEST) + rb_ref[...]
    lane = lax.broadcasted_iota(I32, z.shape, 1)
    neg_inf = -jnp.inf
    gl = jnp.where(lane < N_GROUPS, z, neg_inf)
    gmax = jnp.max(gl, axis=1, keepdims=True)
    gidx = jnp.min(jnp.where(gl == gmax, lane, LANES), axis=1, keepdims=True)
    gprob = 1.0 / jnp.sum(jnp.exp(gl - gmax), axis=1, keepdims=True)
    lo = EXPERT_LANE0 + EXPERTS_PER_GROUP * gidx
    el = jnp.where(lane >= lo, jnp.where(lane < lo + EXPERTS_PER_GROUP, z, neg_inf), neg_inf)
    m1 = jnp.max(el, axis=1, keepdims=True)
    j1 = jnp.min(jnp.where(el == m1, lane, LANES), axis=1, keepdims=True)
    el2 = jnp.where(lane == j1, neg_inf, el)
    m2 = jnp.max(el2, axis=1, keepdims=True)
    j2 = jnp.min(jnp.where(el2 == m2, lane, LANES), axis=1, keepdims=True)
    e21 = jnp.exp(m2 - m1)
    w1 = 1.0 / (1.0 + e21)
    gate_ref[...] = jnp.where(lane == j1, w1 * gprob, jnp.where(lane == j2, e21 * w1 * gprob, 0.0))


def out_proj_router(attn, rnn, x, g1, sc2, sh2, attn_g, rnn_g, w_out, norm2_g, router_w, router_b, tt):
    n, t, d = x.shape
    c = attn.shape[2]
    per_row = g1.shape[1] != 1
    mod_spec = (pl.BlockSpec((None, tt, d), lambda b, i: (b, i, 0)) if per_row
                else pl.BlockSpec((None, 1, d), lambda b, i: (b, 0, 0)))
    row = lambda w_: pl.BlockSpec((None, tt, w_), lambda b, i: (b, i, 0))
    const = lambda shape: pl.BlockSpec(shape, lambda b, i: (0,) * len(shape))
    return pl.pallas_call(
        _out_router_body,
        grid=(n, t // tt),
        in_specs=[row(c), row(c), row(d), mod_spec, mod_spec, mod_spec, const((1, c)), const((1, c)),
                  const(w_out.shape), const((1, d)), const(router_w.shape), const((1, LANES))],
        out_specs=[row(d), row(d), row(LANES)],
        out_shape=[jax.ShapeDtypeStruct((n, t, d), F32), jax.ShapeDtypeStruct((n, t, d), BF16),
                   jax.ShapeDtypeStruct((n, t, LANES), F32)],
        compiler_params=_cparams("parallel", "arbitrary"),
        name="out_proj_router",
    )(attn, rnn, x, g1, sc2, sh2, attn_g.reshape(1, c), rnn_g.reshape(1, c), w_out, norm2_g.reshape(1, d),
      router_w, router_b)


def _moe_body(h2_ref, gate_ref, x1_ref, g2_ref, wgu_ref, wd_ref, fg_ref, y_ref, acc_ref):
    e = pl.program_id(2)
    d_exp = wd_ref.shape[0]

    @pl.when(e == 0)
    def _():
        acc_ref[...] = jnp.zeros(acc_ref.shape, F32)

    gates = gate_ref[...]
    lane = lax.broadcasted_iota(I32, gates.shape, 1)
    gcol = jnp.sum(jnp.where(lane == e + EXPERT_LANE0, gates, 0.0), axis=1, keepdims=True)
    gu = jnp.dot(h2_ref[...], wgu_ref[...], preferred_element_type=F32)
    gpart, upart = gu[:, :d_exp], gu[:, d_exp:]
    act = (gpart * _sigmoid(gpart) * upart).astype(BF16)
    acc_ref[...] += gcol * jnp.dot(act, wd_ref[...], preferred_element_type=F32)

    @pl.when(e == pl.num_programs(2) - 1)
    def _():
        x2 = x1_ref[...] + g2_ref[...] * acc_ref[...]
        y_ref[...] = _rms(x2, fg_ref[...])


def moe_final(h2, gates, x1, g2, w_gu, w_down, final_g, tt):
    n, t, d = x1.shape
    n_exp = w_gu.shape[0]
    per_row = g2.shape[1] != 1
    mod_spec = (pl.BlockSpec((None, tt, d), lambda b, i, e: (b, i, 0)) if per_row
                else pl.BlockSpec((None, 1, d), lambda b, i, e: (b, 0, 0)))
    row = lambda w_: pl.BlockSpec((None, tt, w_), lambda b, i, e: (b, i, 0))
    return pl.pallas_call(
        _moe_body,
        grid=(n, t // tt, n_exp),
        in_specs=[row(d), row(LANES), row(d), mod_spec,
                  pl.BlockSpec((None,) + w_gu.shape[1:], lambda b, i, e: (e, 0, 0)),
                  pl.BlockSpec((None,) + w_down.shape[1:], lambda b, i, e: (e, 0, 0)),
                  pl.BlockSpec((1, d), lambda b, i, e: (0, 0))],
        out_specs=row(d),
        out_shape=jax.ShapeDtypeStruct((n, t, d), F32),
        scratch_shapes=[pltpu.VMEM((tt, d), F32)],
        compiler_params=_cparams("parallel", "arbitrary", "arbitrary"),
        name="moe_final",
    )(h2, gates, x1, g2, w_gu, w_down, final_g.reshape(1, d))


def _tile(t, pref):
    return pref if t % pref == 0 else t


def _layer(x, mods, pos, attend, conv_prev, h_prev, n_seq, lw, final_g):
    sh1, sc1, g1, sh2, sc2, g2 = mods
    n, t, d = x.shape
    t_seq = n * t // n_seq
    seqs = lambda a: a.reshape(n_seq, t_seq, a.shape[-1])
    q, k, v, kb, vt, iq, ikw, ikb, xr, yg = in_proj(x, sc1, sh1, lw["norm1_g"], lw["w_in"], rope_tables(pos),
                                                     _tile(t, 512))
    attn = attend(q, iq, ikw, k, v, kb, vt, ikb)
    rnn, h_new = rglru(seqs(xr), seqs(yg), conv_prev, h_prev, lw["conv_w"], lw["conv_b"], lw["rg_wa"], lw["rg_ba"],
                       lw["rg_wx"], lw["rg_bx"], lw["rg_lambda"], _tile(t_seq, 512))
    x1, h2, gates = out_proj_router(attn, rnn.reshape(xr.shape), x, g1, sc2, sh2, lw["attn_out_g"], lw["rnn_out_g"],
                                    lw["w_out"], lw["norm2_g"], lw["router_w"], lw["router_b"], _tile(t, 512))
    y = moe_final(h2, gates, x1, g2, lw["w_gu"], lw["w_down"], final_g, _tile(t, 1024))
    conv_new = seqs(xr)[:, t_seq - (CONV_WIDTH - 1):, :]
    return (seqs(y), seqs(k).reshape(n_seq, t_seq, N_KV_HEADS, HEAD_DIM),
            seqs(v).reshape(n_seq, t_seq, N_KV_HEADS, HEAD_DIM), seqs(ikw)[:, :, :IDX_HEAD_DIM], conv_new,
            h_new[:, 0, :])


def kernel(x_prompt, x_sample, cache_k, cache_v, cache_idx_k, state_conv, state_rglru, page_table, c_prompt,
           c_sample, ada_w, ada_b, norm1_g, w_in, conv_w, conv_b, rg_wa, rg_ba, rg_wx, rg_bx, rg_lambda, attn_out_g,
           rnn_out_g, w_out, norm2_g, router_grp_w, router_grp_b, router_exp_w, router_exp_b, exp_w_gate, exp_w_up,
           exp_w_down, final_g):
    depth = ada_w.shape[0]
    assert depth == 1, "this implementation handles the single-layer configuration"
    n_p, t_p, d = x_prompt.shape
    n_s, t_s, _ = x_sample.shape
    n_pages = page_table.shape[1]
    past = n_pages * PAGE_SIZE
    l = 0
    router_w = jnp.concatenate([router_grp_w[l], router_exp_w[l],
                                jnp.zeros((d, LANES - N_GROUPS - N_EXPERTS), F32)], axis=1)
    router_b = jnp.concatenate([router_grp_b[l], router_exp_b[l],
                                jnp.zeros((LANES - N_GROUPS - N_EXPERTS,), F32)]).reshape(1, LANES)
    lw = dict(norm1_g=norm1_g[l], w_in=prep_w_in(w_in[l]), conv_w=conv_w[l], conv_b=conv_b[l], rg_wa=rg_wa[l],
              rg_ba=rg_ba[l], rg_wx=rg_wx[l], rg_bx=rg_bx[l], rg_lambda=rg_lambda[l], attn_out_g=attn_out_g[l],
              rnn_out_g=rnn_out_g[l], w_out=w_out[l].astype(BF16), norm2_g=norm2_g[l], router_w=router_w,
              router_b=router_b,
              w_gu=jnp.concatenate([exp_w_gate[l], exp_w_up[l]], axis=2).astype(BF16),
              w_down=exp_w_down[l].astype(BF16))

    mod = ada_mod(jnp.concatenate([c_prompt, c_sample], axis=0), ada_w[l], ada_b[l])
    mods_p = [m[:n_p, None, :] for m in jnp.split(mod, 6, axis=1)]
    mods_s = [jnp.repeat(m[n_p:], t_s, axis=0)[None] for m in jnp.split(mod, 6, axis=1)]

    def attend_prompt(q, iq, ikw, k, v, kb, vt, ikb):
        iw_t = jnp.swapaxes(ikw[:, :, IDX_HEAD_DIM:IDX_HEAD_DIM + N_IDX_HEADS], 1, 2)
        return prompt_attention(q, iq, iw_t, kb, vt, ikb, _tile(t_p, 256))

    def attend_sample(q, iq, ikw, k, v, kb, vt, ikb):
        seqs = lambda a: a.reshape(n_s, t_s, a.shape[-1])
        n_pool = cache_k.shape[1]
        out = sample_attention(page_table, seqs(q), seqs(iq), seqs(ikw), seqs(k), seqs(v),
                               cache_k[l].reshape(n_pool, PAGE_SIZE, N_KV_HEADS * HEAD_DIM),
                               cache_v[l].reshape(n_pool, PAGE_SIZE, N_KV_HEADS * HEAD_DIM),
                               cache_idx_k[l])
        return out.reshape(1, n_s * t_s, N_HEADS * HEAD_DIM)

    c_rnn = conv_w.shape[2]
    yp, kp, vp, ikp, cvp, hp = _layer(
        x_prompt, mods_p, jnp.arange(t_p), attend_prompt,
        jnp.zeros((n_p, CONV_WIDTH - 1, c_rnn), F32), jnp.zeros((n_p, c_rnn), F32), n_p, lw, final_g)
    ys, ks, vs, iks, cvs, hs = _layer(
        x_sample.reshape(1, n_s * t_s, d), mods_s, jnp.tile(past + jnp.arange(t_s), n_s), attend_sample,
        state_conv[l], state_rglru[l], n_s, lw, final_g)

    st = lambda a: a[None]
    return (yp, ys, st(kp), st(vp), st(ikp), st(cvp), st(hp), st(ks), st(vs), st(iks), st(cvs), st(hs))
```

```python
import functools

import jax
import jax.numpy as jnp
from jax import lax
from jax.experimental import pallas as pl
from jax.experimental.pallas import tpu as pltpu

F32, BF16, I32, I16 = jnp.float32, jnp.bfloat16, jnp.int32, jnp.int16
HIGHEST = lax.Precision.HIGHEST

HEAD_DIM = 64
N_HEADS = 8
N_KV_HEADS = 2
HEADS_PER_KV = N_HEADS // N_KV_HEADS
N_IDX_HEADS = 8
IDX_HEAD_DIM = 32
TOPK_MAX = 256
ROPE_THETA = 10000.0
CONV_WIDTH = 4
RG_BLOCKS = 8
RG_C = 8.0
N_GROUPS = 4
EXPERTS_PER_GROUP = 8
N_EXPERTS = N_GROUPS * EXPERTS_PER_GROUP
PAGE_SIZE = 128
NORM_EPS = 1e-6

LANES = 128
SUBLANES = 8
VMEM_LIMIT = 48 * 1024 * 1024
INT_MIN = -(2 ** 31)
I16_MIN = -(2 ** 15)
NEG_BIG = -1e30
EXPERT_LANE0 = N_GROUPS
BF16_SUBLANES = 16
V_AUG_ROWS = HEAD_DIM + BF16_SUBLANES
ATT_CHUNK_HEADS = 2
LOG2_E = 1.4426950408889634
Q_SCALE = HEAD_DIM ** -0.5 * LOG2_E


def _cparams(*sem):
    return pltpu.CompilerParams(dimension_semantics=sem, vmem_limit_bytes=VMEM_LIMIT)


def _rms(x, g):
    return x * lax.rsqrt(jnp.mean(x * x, axis=-1, keepdims=True) + NORM_EPS) * g


def _sigmoid(x):
    return 1.0 / (1.0 + jnp.exp(-x))


def _float_key(s):
    b = pltpu.bitcast(s + 0.0, I32)
    return b ^ ((b >> 31) & 0x7FFFFFFF)


def _ada_body(c_ref, w_ref, b_ref, o_ref):
    c = c_ref[...]
    s = c * _sigmoid(c)
    o_ref[...] = jnp.dot(s, w_ref[...], preferred_element_type=F32, precision=HIGHEST) + b_ref[...]


def ada_mod(c, w, b):
    m, d = c.shape
    n_chunks = w.shape[1] // d
    return pl.pallas_call(
        _ada_body,
        grid=(n_chunks,),
        in_specs=[pl.BlockSpec((m, d), lambda j: (0, 0)),
                  pl.BlockSpec((d, d), lambda j: (0, j)),
                  pl.BlockSpec((1, d), lambda j: (0, j))],
        out_specs=pl.BlockSpec((m, d), lambda j: (0, j)),
        out_shape=jax.ShapeDtypeStruct((m, w.shape[1]), F32),
        compiler_params=_cparams("arbitrary"),
        name="ada_mod",
    )(c, w, b.reshape(1, -1))


_C_Q, _C_QR, _C_K, _C_KR, _C_V, _C_IQ, _C_IQR, _C_IKW, _C_IKR, _C_XR, _C_YG, _C_END = (
    0, 512, 1024, 1152, 1280, 1408, 1664, 1920, 2048, 2176, 2688, 3200)


def _rot_half_cols(w, head_dim):
    d_in, n = w.shape
    wh = w.reshape(d_in, n // head_dim, 2, head_dim // 2)
    return jnp.stack([-wh[:, :, 1], wh[:, :, 0]], axis=2).reshape(d_in, n)


def prep_w_in(w_in):
    d_in = w_in.shape[0]
    q, k, v, iq, ik, iw, xr, yg = jnp.split(w_in, [512, 640, 768, 1024, 1056, 1064, 1576], axis=1)
    z = lambda n: jnp.zeros((d_in, n), w_in.dtype)
    ikw = jnp.concatenate([ik, iw, z(LANES - IDX_HEAD_DIM - N_IDX_HEADS)], axis=1)
    ikr = jnp.concatenate([_rot_half_cols(ik, IDX_HEAD_DIM), z(LANES - IDX_HEAD_DIM)], axis=1)
    w = jnp.concatenate([q, _rot_half_cols(q, HEAD_DIM), k, _rot_half_cols(k, HEAD_DIM), v,
                         iq, _rot_half_cols(iq, IDX_HEAD_DIM), ikw, ikr, xr, yg], axis=1)
    return w.astype(BF16)


def rope_tables(pos):
    pos = pos.astype(F32)[:, None]

    def cs(half, reps):
        inv = ROPE_THETA ** (-jnp.arange(half, dtype=F32) / half)
        ang = pos * inv[None, :]
        return jnp.tile(jnp.cos(ang), (1, 2 * reps)), jnp.tile(jnp.sin(ang), (1, 2 * reps))

    cq, sq = cs(HEAD_DIM // 2, LANES // HEAD_DIM)
    ci, si = cs(IDX_HEAD_DIM // 2, LANES // IDX_HEAD_DIM)
    lane = jnp.arange(LANES)[None, :]
    w_scale = (IDX_HEAD_DIM ** -0.5) * (N_IDX_HEADS ** -0.5)
    cx = jnp.where(lane < IDX_HEAD_DIM, ci, jnp.where(lane < IDX_HEAD_DIM + N_IDX_HEADS, w_scale, 0.0))
    sx = jnp.where(lane < IDX_HEAD_DIM, si, 0.0)
    return cq, sq, ci, si, cx.astype(F32), sx.astype(F32)


def _in_proj_body(x_ref, sc_ref, sh_ref, g_ref, w_ref, cq_ref, sq_ref, ci_ref, si_ref, cx_ref, sx_ref,
                  q_ref, k_ref, v_ref, kb_ref, vt_ref, iq_ref, ikw_ref, ikb_ref, xr_ref, yg_ref):
    x = x_ref[...]
    h = (_rms(x, g_ref[...]) * (1.0 + sc_ref[...]) + sh_ref[...]).astype(BF16)

    def seg(a, b):
        return jnp.dot(h, w_ref[:, a:b], preferred_element_type=F32)

    cq, sq = cq_ref[...], sq_ref[...]
    cq4 = jnp.concatenate([cq] * 4, axis=1)
    sq4 = jnp.concatenate([sq] * 4, axis=1)
    q = (seg(_C_Q, _C_QR) * cq4 + seg(_C_QR, _C_K) * sq4) * Q_SCALE
    q_ref[...] = q.astype(BF16)
    k = seg(_C_K, _C_KR) * cq + seg(_C_KR, _C_V) * sq
    k_ref[...] = k
    kb_ref[...] = k.astype(BF16)
    v = seg(_C_V, _C_IQ)
    v_ref[...] = v
    vt_ref[...] = v.T.astype(BF16)
    ci2 = jnp.concatenate([ci_ref[...]] * 2, axis=1)
    si2 = jnp.concatenate([si_ref[...]] * 2, axis=1)
    iq_ref[...] = (seg(_C_IQ, _C_IQR) * ci2 + seg(_C_IQR, _C_IKW) * si2).astype(BF16)
    ikw = seg(_C_IKW, _C_IKR) * cx_ref[...] + seg(_C_IKR, _C_XR) * sx_ref[...]
    ikw_ref[...] = ikw
    ikb_ref[...] = ikw.astype(BF16)
    xr_ref[...] = seg(_C_XR, _C_YG)
    yg_ref[...] = seg(_C_YG, _C_END)


def in_proj(x, sc, sh, g, w, tables, tt):
    n, t, d = x.shape
    per_row = sc.shape[1] != 1
    mod_spec = (pl.BlockSpec((None, tt, d), lambda b, i: (b, i, 0)) if per_row
                else pl.BlockSpec((None, 1, d), lambda b, i: (b, 0, 0)))
    row = lambda w_: pl.BlockSpec((None, tt, w_), lambda b, i: (b, i, 0))
    tab = pl.BlockSpec((tt, LANES), lambda b, i: (i, 0))
    outs = [(512, BF16), (128, F32), (128, F32), (128, BF16), None, (256, BF16), (128, F32), (128, BF16),
            (512, F32), (512, F32)]
    out_shape, out_specs = [], []
    for o in outs:
        if o is None:
            out_shape.append(jax.ShapeDtypeStruct((n, LANES, t), BF16))
            out_specs.append(pl.BlockSpec((None, LANES, tt), lambda b, i: (b, 0, i)))
        else:
            out_shape.append(jax.ShapeDtypeStruct((n, t, o[0]), o[1]))
            out_specs.append(row(o[0]))
    return pl.pallas_call(
        _in_proj_body,
        grid=(n, t // tt),
        in_specs=[row(d), mod_spec, mod_spec,
                  pl.BlockSpec((1, d), lambda b, i: (0, 0)),
                  pl.BlockSpec(w.shape, lambda b, i: (0, 0))] + [tab] * 6,
        out_specs=out_specs,
        out_shape=out_shape,
        compiler_params=_cparams("parallel", "arbitrary"),
        name="in_proj",
    )(x, sc, sh, g.reshape(1, d), w, *tables)


def _pattn_body(q_ref, iq_ref, iwt_ref, k_ref, vt_ref, ik_ref, o_ref,
                hi_ref, lo_ref, lb_ref, p_ref, m_ref, acc_ref, *, n_sel, idx_bits):
    qi = pl.program_id(1)
    tq = q_ref.shape[0]
    tk = tq
    n_tiles = qi + 1

    iq_t = iq_ref[...].astype(F32).T
    iq_st = jnp.concatenate([iq_t[IDX_HEAD_DIM * h:IDX_HEAD_DIM * (h + 1)] for h in range(N_IDX_HEADS)],
                            axis=1).astype(BF16)
    w_t = iwt_ref[...]
    q_t = q_ref[...].astype(F32).T
    zero_half = jnp.zeros((HEAD_DIM, tq), F32)
    q_pad = []
    for g in range(N_KV_HEADS):
        cols = []
        for hh in range(HEADS_PER_KV):
            h = g * HEADS_PER_KV + hh
            qh = q_t[HEAD_DIM * h:HEAD_DIM * (h + 1)]
            cols.append(jnp.concatenate([qh, zero_half] if g == 0 else [zero_half, qh], axis=0))
        q_pad.append(jnp.concatenate(cols, axis=1).astype(BF16))

    row_s = lax.broadcasted_iota(I32, (tk, tq), 0)
    col_t = lax.broadcasted_iota(I32, (tk, tq), 1)
    causal = row_s <= col_t

    def tile_off(j):
        return pl.multiple_of(j * tk, tk)

    def score_keys(j):
        ik = ik_ref[pl.ds(tile_off(j), tk), :][:, :IDX_HEAD_DIM]
        d = jnp.dot(ik, iq_st, preferred_element_type=F32)
        sc = w_t[0:1, :] * jnp.maximum(d[:, 0:tq], 0.0)
        for h in range(1, N_IDX_HEADS):
            sc = sc + w_t[h:h + 1, :] * jnp.maximum(d[:, h * tq:(h + 1) * tq], 0.0)
        return _float_key(sc)

    def store_keys(j, key):
        rows = pl.ds(tile_off(j), tk)
        hi_ref[rows, :] = (key >> 16).astype(I16)
        lo_ref[rows, :] = ((key & 0xFFFF) + I16_MIN).astype(I16)

    def p1(j, c):
        store_keys(j, score_keys(j))
        return c

    lax.fori_loop(0, qi, p1, 0)
    store_keys(qi, jnp.where(causal, score_keys(qi), INT_MIN))

    groups = tk // BF16_SUBLANES

    def rows16(v):
        return jnp.broadcast_to(v.astype(I16), (BF16_SUBLANES, tq))[None]

    def tile16(ref, j):
        return ref[pl.ds(tile_off(j), tk), :].reshape(groups, BF16_SUBLANES, tq)

    one16, zero16 = jnp.int16(1), jnp.int16(0)

    def count(pred):
        def body(j, acc):
            marks = pred(j)
            for g in range(groups):
                acc = acc + marks[g]
            return acc

        acc = lax.fori_loop(0, n_tiles, body, jnp.zeros((BF16_SUBLANES, tq), I16))
        return jnp.sum(acc.astype(I32).astype(F32), axis=0, keepdims=True)

    def bisect16(ref, need):
        def step(b, thr):
            cand = thr + jnp.left_shift(jnp.int32(1), 15 - b)
            cand_b = rows16(cand)
            c = count(lambda j: jnp.where(tile16(ref, j) >= cand_b, one16, zero16))
            return jnp.where(c >= need, cand, thr)
        return lax.fori_loop(0, 16, step, jnp.full((1, tq), I16_MIN, I32))

    thr_hi = bisect16(hi_ref, n_sel)
    hi_b = rows16(thr_hi)
    c_above = count(lambda j: jnp.where(tile16(hi_ref, j) > hi_b, one16, zero16))

    def p2(j, c):
        rows = pl.ds(tile_off(j), tk)
        lb_ref[rows, :] = jnp.where(hi_ref[rows, :] == thr_hi.astype(I16), lo_ref[rows, :], jnp.int16(I16_MIN))
        return c

    lax.fori_loop(0, n_tiles, p2, 0)
    thr_lo = bisect16(lb_ref, n_sel - c_above)
    lo_b = rows16(thr_lo)
    c_gt = c_above + count(lambda j: jnp.where(tile16(lb_ref, j) > lo_b, one16, zero16))
    ties_left = n_sel - c_gt

    def is_tie(j):
        return jnp.where(tile16(hi_ref, j) == hi_b, jnp.where(tile16(lo_ref, j) == lo_b, one16, zero16), zero16)

    c_tie = count(is_tie)
    p_ref[...] = jnp.full((1, tq), 2 ** idx_bits - 1, I32)
    has_thr = jnp.where(thr_hi > I16_MIN, 1, jnp.where(thr_lo > I16_MIN, 1, 0))
    excess = jnp.max(jnp.where(has_thr > 0, c_tie - ties_left, 0.0)) > 0.0

    idx16 = row_s.astype(I16)

    @pl.when(excess)
    def _():
        sub = idx16.reshape(groups, BF16_SUBLANES, tq)

        def tb(b, p):
            cand = p | jnp.left_shift(jnp.int32(1), idx_bits - 1 - b)
            c = count(lambda j: jnp.where(sub < rows16(cand - j * tk), is_tie(j), zero16))
            return jnp.where(c < ties_left, cand, p)

        p_ref[...] = lax.fori_loop(0, idx_bits, tb, jnp.zeros((1, tq), I32))

    p_last = p_ref[...]
    thr_hi16, thr_lo16 = thr_hi.astype(I16), thr_lo.astype(I16)

    m_ref[...] = jnp.full(m_ref.shape, NEG_BIG, F32)
    acc_ref[...] = jnp.zeros(acc_ref.shape, F32)
    ones_rows = jnp.ones((V_AUG_ROWS - HEAD_DIM, tk), BF16)

    def attend(j, diag):
        off = tile_off(j)
        hi, lo = hi_ref[pl.ds(off, tk), :], lo_ref[pl.ds(off, tk), :]
        keep_tie = jnp.where(idx16 <= (p_last - j * tk).astype(I16), one16, zero16)
        sel = jnp.where(hi > thr_hi16, one16,
                        jnp.where(hi == thr_hi16,
                                  jnp.where(lo > thr_lo16, one16, jnp.where(lo == thr_lo16, keep_tie, zero16)),
                                  zero16))
        bias = jnp.where(sel.astype(I32) > 0, 0.0, NEG_BIG)
        if diag:
            bias = jnp.where(causal, bias, NEG_BIG)
        kt = k_ref[pl.ds(off, tk), :]
        vt = vt_ref[:, pl.ds(off, tk)]
        v_aug = [jnp.concatenate([vt[HEAD_DIM * g:HEAD_DIM * (g + 1)], ones_rows], axis=0)
                 for g in range(N_KV_HEADS)]
        w_c = ATT_CHUNK_HEADS * tq
        bias_c = jnp.concatenate([bias] * ATT_CHUNK_HEADS, axis=1)
        chunks = [(g, c * w_c) for g in range(N_KV_HEADS) for c in range(HEADS_PER_KV // ATT_CHUNK_HEADS)]

        def scores(g, c0):
            return jnp.dot(kt, q_pad[g][:, c0:c0 + w_c], preferred_element_type=F32) + bias_c

        def softmax_pv(g, c0, s):
            cols = slice(c0, c0 + w_c)
            m_old = m_ref[g:g + 1, cols]
            m_new = jnp.maximum(m_old, jnp.max(s, axis=0, keepdims=True))
            alpha = jnp.exp2(m_old - m_new)
            p = jnp.exp2(s - m_new).astype(BF16)
            acc_ref[g, :, cols] = alpha * acc_ref[g, :, cols] + jnp.dot(v_aug[g], p, preferred_element_type=F32)
            m_ref[g:g + 1, cols] = m_new

        s_next = scores(*chunks[0])
        for i, (g, c0) in enumerate(chunks):
            s_cur = s_next
            if i + 1 < len(chunks):
                s_next = scores(*chunks[i + 1])
            softmax_pv(g, c0, s_cur)

    def p3(j, c):
        attend(j, False)
        return c

    lax.fori_loop(0, qi, p3, 0)
    attend(qi, True)

    rows = []
    for g in range(N_KV_HEADS):
        acc = acc_ref[g]
        o_g = acc[:HEAD_DIM] / acc[HEAD_DIM:HEAD_DIM + 1]
        rows += [o_g[:, hh * tq:(hh + 1) * tq] for hh in range(HEADS_PER_KV)]
    o_ref[...] = jnp.concatenate(rows, axis=0).T


def prompt_attention(q, iq, iw_t, kb, vt, ikb, tq):
    n, t, _ = q.shape
    n_sel = min(TOPK_MAX, t // 4)
    idx_bits = max(1, (t - 1).bit_length())
    body = functools.partial(_pattn_body, n_sel=n_sel, idx_bits=idx_bits)
    return pl.pallas_call(
        body,
        grid=(n, t // tq),
        in_specs=[pl.BlockSpec((None, tq, q.shape[2]), lambda b, i: (b, i, 0)),
                  pl.BlockSpec((None, tq, iq.shape[2]), lambda b, i: (b, i, 0)),
                  pl.BlockSpec((None, N_IDX_HEADS, tq), lambda b, i: (b, 0, i)),
                  pl.BlockSpec((None, t, LANES), lambda b, i: (b, 0, 0)),
                  pl.BlockSpec((None, LANES, t), lambda b, i: (b, 0, 0)),
                  pl.BlockSpec((None, t, LANES), lambda b, i: (b, 0, 0))],
        out_specs=pl.BlockSpec((None, tq, N_HEADS * HEAD_DIM), lambda b, i: (b, i, 0)),
        out_shape=jax.ShapeDtypeStruct((n, t, N_HEADS * HEAD_DIM), F32),
        scratch_shapes=[pltpu.VMEM((t, tq), I16),
                        pltpu.VMEM((t, tq), I16),
                        pltpu.VMEM((t, tq), I16),
                        pltpu.VMEM((1, tq), I32),
                        pltpu.VMEM((N_KV_HEADS, HEADS_PER_KV * tq), F32),
                        pltpu.VMEM((N_KV_HEADS, V_AUG_ROWS, HEADS_PER_KV * tq), F32)],
        compiler_params=_cparams("parallel", "arbitrary"),
        name="prompt_attention",
    )(q, iq, iw_t, kb, vt, ikb)


Q_PAD = SUBLANES


def _sattn_body(pt_ref, q_ref, iq_ref, iw_ref, kn_ref, vn_ref, ikn_ref, ck_hbm, cv_hbm, cik_hbm, o_ref,
                kbuf, vbuf, ikbuf, s_ref, p_ref, sem, *, n_pages, n_sel, idx_bits):
    b = pl.program_id(0)
    slot = b % 2
    past = n_pages * PAGE_SIZE
    lk = past + LANES

    streams = ((cik_hbm, ikbuf), (ck_hbm, kbuf), (cv_hbm, vbuf))

    def page_copy(kind, p, seq, slt):
        src, dst = streams[kind]
        return pltpu.make_async_copy(src.at[pt_ref[seq, p]], dst.at[slt, :, pl.ds(p * PAGE_SIZE, PAGE_SIZE)],
                                     sem.at[slt, kind])

    def start_pages(seq, slt):
        for kind in range(len(streams)):
            for p in range(n_pages):
                page_copy(kind, p, seq, slt).start()

    def wait_pages(kind):
        for p in range(n_pages):
            page_copy(kind, p, b, slot).wait()

    @pl.when(b == 0)
    def _():
        start_pages(0, 0)

    @pl.when(b + 1 < pl.num_programs(0))
    def _():
        start_pages(b + 1, 1 - slot)

    def new_cols(ref):
        rows = jnp.concatenate([ref[...], jnp.zeros((LANES - Q_PAD, LANES), F32)], axis=0)
        return rows.T

    new_tile = pl.ds(past, LANES)
    kbuf[slot, :, new_tile] = new_cols(kn_ref)
    vbuf[slot, :, new_tile] = new_cols(vn_ref)
    ikbuf[slot, :, new_tile] = new_cols(ikn_ref)[:IDX_HEAD_DIM]

    iq = iq_ref[...].astype(F32)
    iq_st = jnp.concatenate([iq[:, IDX_HEAD_DIM * h:IDX_HEAD_DIM * (h + 1)] for h in range(N_IDX_HEADS)],
                            axis=0).astype(BF16)
    qf = q_ref[...].astype(F32)
    lane128 = lax.broadcasted_iota(I32, (Q_PAD, LANES), 1)
    q_rows = []
    for h in range(N_HEADS):
        g = h // HEADS_PER_KV
        slab = qf[:, LANES * (h // 2):LANES * (h // 2 + 1)]
        if (h % 2) != g:
            slab = pltpu.roll(slab, HEAD_DIM, axis=1)
        keep = (lane128 < HEAD_DIM) if g == 0 else (lane128 >= HEAD_DIM)
        q_rows.append(jnp.where(keep, slab, 0.0))
    q_pad = jnp.concatenate(q_rows, axis=0).astype(BF16)
    iw = iw_ref[...]

    key_pos = lax.broadcasted_iota(I32, (Q_PAD, lk), 1)
    q_row = lax.broadcasted_iota(I32, (Q_PAD, lk), 0)
    admissible = key_pos <= past + q_row

    wait_pages(0)
    d = jnp.dot(iq_st, ikbuf[slot].astype(BF16), preferred_element_type=F32)
    sc = None
    for h in range(N_IDX_HEADS):
        w_h = iw[:, IDX_HEAD_DIM + h:IDX_HEAD_DIM + h + 1]
        term = w_h * jnp.maximum(d[Q_PAD * h:Q_PAD * (h + 1)], 0.0)
        sc = term if sc is None else sc + term
    s_ref[...] = jnp.where(admissible, _float_key(sc), INT_MIN)

    lane_pos = lax.broadcasted_iota(I32, (Q_PAD, LANES), 1)

    def count(pred):
        terms = [pred(s_ref[:, c * LANES:(c + 1) * LANES], c * LANES) for c in range(lk // LANES)]
        while len(terms) > 1:
            terms = [terms[i] + terms[i + 1] if i + 1 < len(terms) else terms[i] for i in range(0, len(terms), 2)]
        return jnp.sum(terms[0], axis=1, keepdims=True)

    def bisect(i, thr):
        cand = jnp.broadcast_to(thr + jnp.left_shift(jnp.int32(1), 31 - i), (Q_PAD, LANES))
        c = count(lambda t, c0: jnp.where(t >= cand, 1.0, 0.0))
        return jnp.where(c >= n_sel, cand[:, :1], thr)

    thr = lax.fori_loop(0, 32, bisect, jnp.full((Q_PAD, 1), INT_MIN, I32))
    thr_b = jnp.broadcast_to(thr, (Q_PAD, LANES))
    ties_left = n_sel - count(lambda t, c0: jnp.where(t > thr_b, 1.0, 0.0))
    c_tie = count(lambda t, c0: jnp.where(t == thr_b, 1.0, 0.0))
    p_ref[...] = jnp.full((Q_PAD, 1), 2 ** idx_bits - 1, I32)

    @pl.when(jnp.max(jnp.where(thr > INT_MIN, c_tie - ties_left, 0.0)) > 0.0)
    def _():
        def tb(i, pidx):
            cand = jnp.broadcast_to(pidx | jnp.left_shift(jnp.int32(1), idx_bits - 1 - i), (Q_PAD, LANES))
            c = count(lambda t, c0: jnp.where(t == thr_b, jnp.where(lane_pos + c0 < cand, 1.0, 0.0), 0.0))
            return jnp.where(c < ties_left, cand[:, :1], pidx)

        p_ref[...] = lax.fori_loop(0, idx_bits, tb, jnp.zeros((Q_PAD, 1), I32))

    p_last = p_ref[...]
    t = s_ref[...]
    keep_tie = jnp.where(key_pos <= p_last, 0.0, NEG_BIG)
    bias = jnp.where(t > thr, 0.0, jnp.where(t == thr, keep_tie, NEG_BIG))
    bias = jnp.where(admissible, bias, NEG_BIG)

    wait_pages(1)
    s = jnp.dot(q_pad, kbuf[slot].astype(BF16), preferred_element_type=F32)
    s = s + jnp.concatenate([bias] * N_HEADS, axis=0)
    m = jnp.max(s, axis=1, keepdims=True)
    pexp = jnp.exp2(s - m)
    l = jnp.sum(pexp, axis=1, keepdims=True)
    wait_pages(2)
    o = lax.dot_general(pexp.astype(BF16), vbuf[slot].astype(BF16), (((1,), (1,)), ((), ())),
                        preferred_element_type=F32)
    o_ref[...] = o / l


def sample_attention(page_table, q, iq, ikw, k_new, v_new, cache_k, cache_v, cache_ik):
    n, n_new, _ = q.shape
    assert n_new <= Q_PAD
    n_pages = page_table.shape[1]
    past = n_pages * PAGE_SIZE
    lk = past + LANES
    n_sel = min(TOPK_MAX, (past + n_new) // 4)
    idx_bits = max(1, (lk - 1).bit_length())
    body = functools.partial(_sattn_body, n_pages=n_pages, n_sel=n_sel, idx_bits=idx_bits)
    pad = lambda a: jnp.pad(a, ((0, 0), (0, Q_PAD - n_new), (0, 0)))
    q, iq, ikw, k_new, v_new = pad(q), pad(iq), pad(ikw), pad(k_new), pad(v_new)
    seq = lambda w_: pl.BlockSpec((None, Q_PAD, w_), lambda b, pt: (b, 0, 0))
    hbm = pl.BlockSpec(memory_space=pl.ANY)
    grid_spec = pltpu.PrefetchScalarGridSpec(
        num_scalar_prefetch=1,
        grid=(n,),
        in_specs=[seq(q.shape[2]), seq(iq.shape[2]), seq(LANES), seq(LANES), seq(LANES), seq(LANES), hbm, hbm, hbm],
        out_specs=pl.BlockSpec((None, N_HEADS * Q_PAD, LANES), lambda b, pt: (b, 0, 0)),
        scratch_shapes=[pltpu.VMEM((2, LANES, lk), F32),
                        pltpu.VMEM((2, LANES, lk), F32),
                        pltpu.VMEM((2, IDX_HEAD_DIM, lk), F32),
                        pltpu.VMEM((Q_PAD, lk), I32),
                        pltpu.VMEM((Q_PAD, 1), I32),
                        pltpu.SemaphoreType.DMA((2, 3))],
    )
    raw = pl.pallas_call(
        body,
        grid_spec=grid_spec,
        out_shape=jax.ShapeDtypeStruct((n, N_HEADS * Q_PAD, LANES), F32),
        compiler_params=_cparams("arbitrary"),
        name="sample_attention",
    )(page_table, q, iq, ikw, k_new, v_new, ikw, cache_k, cache_v, cache_ik)
    raw = raw.reshape(n, N_HEADS, Q_PAD, N_KV_HEADS, HEAD_DIM)[:, :, :n_new]
    per_head = jnp.stack([raw[:, h, :, h // HEADS_PER_KV] for h in range(N_HEADS)], axis=2)
    return per_head.reshape(n, n_new, N_HEADS * HEAD_DIM)


def _rglru_body(xr_ref, yg_ref, cprev_ref, hprev_ref, cw_ref, cb_ref, wa_ref, ba_ref, wx_ref, bx_ref, lam_ref,
                y_ref, hlast_ref, xbuf, a_buf, b_buf, h_buf, h_carry):
    i = pl.program_id(1)
    tt = xr_ref.shape[0]
    halo = SUBLANES

    @pl.when(i == 0)
    def _():
        xbuf[pl.ds(0, halo), :] = jnp.zeros((halo, xbuf.shape[1]), F32)
        xbuf[pl.ds(halo - (CONV_WIDTH - 1), CONV_WIDTH - 1), :] = cprev_ref[...]
        h_carry[...] = hprev_ref[...]

    x = xr_ref[...]
    xbuf[pl.ds(halo, tt), :] = x
    xc = cb_ref[...] + cw_ref[CONV_WIDTH - 1:CONV_WIDTH, :] * x
    for j in range(CONV_WIDTH - 1):
        xc = xc + cw_ref[j:j + 1, :] * xbuf[pl.ds(halo - (CONV_WIDTH - 1) + j, tt), :]
    xcb = xc.astype(BF16)
    r = _sigmoid(jnp.dot(xcb, wa_ref[...], preferred_element_type=F32) + ba_ref[...])
    ig = _sigmoid(jnp.dot(xcb, wx_ref[...], preferred_element_type=F32) + bx_ref[...])
    z = -lam_ref[...]
    softplus = jnp.maximum(z, 0.0) + jnp.log1p(jnp.exp(-jnp.abs(z)))
    log_a = -RG_C * r * softplus
    a = jnp.exp(log_a)
    a_buf[...] = a
    b_buf[...] = jnp.sqrt(jnp.tanh(-log_a) * (1.0 + a * a)) * (ig * xc)

    def step(t, h):
        h = a_buf[pl.ds(t, 1), :] * h + b_buf[pl.ds(t, 1), :]
        h_buf[pl.ds(t, 1), :] = h
        return h

    h_fin = lax.fori_loop(0, tt, step, h_carry[...], unroll=min(8, tt))
    h_carry[...] = h_fin
    hlast_ref[...] = h_fin
    y_ref[...] = h_buf[...] * jax.nn.gelu(yg_ref[...])
    xbuf[pl.ds(halo - (CONV_WIDTH - 1), CONV_WIDTH - 1), :] = xbuf[pl.ds(halo + tt - (CONV_WIDTH - 1), CONV_WIDTH - 1), :]


def _block_diag(w):
    k, c, d = w.shape
    eye = jnp.eye(k, dtype=w.dtype)
    return (eye[:, None, :, None] * w[:, :, None, :]).reshape(k * c, k * d)


def rglru(xr, yg, conv_prev, h_prev, conv_w, conv_b, rg_wa, rg_ba, rg_wx, rg_bx, rg_lambda, tt):
    n, t, c = xr.shape
    row = pl.BlockSpec((None, tt, c), lambda b, i: (b, i, 0))
    const = lambda shape: pl.BlockSpec(shape, lambda b, i: (0,) * len(shape))
    vec = lambda a: a.reshape(1, c)
    return pl.pallas_call(
        _rglru_body,
        grid=(n, t // tt),
        in_specs=[row, row,
                  pl.BlockSpec((None, CONV_WIDTH - 1, c), lambda b, i: (b, 0, 0)),
                  pl.BlockSpec((None, 1, c), lambda b, i: (b, 0, 0)),
                  const((CONV_WIDTH, c)), const((1, c)), const((c, c)), const((1, c)), const((c, c)),
                  const((1, c)), const((1, c))],
        out_specs=[row, pl.BlockSpec((None, 1, c), lambda b, i: (b, 0, 0))],
        out_shape=[jax.ShapeDtypeStruct((n, t, c), F32), jax.ShapeDtypeStruct((n, 1, c), F32)],
        scratch_shapes=[pltpu.VMEM((tt + SUBLANES, c), F32), pltpu.VMEM((tt, c), F32), pltpu.VMEM((tt, c), F32),
                        pltpu.VMEM((tt, c), F32), pltpu.VMEM((1, c), F32)],
        compiler_params=_cparams("parallel", "arbitrary"),
        name="rglru",
    )(xr, yg, conv_prev, h_prev.reshape(n, 1, c), conv_w, vec(conv_b), _block_diag(rg_wa).astype(BF16), vec(rg_ba),
      _block_diag(rg_wx).astype(BF16), vec(rg_bx), vec(rg_lambda))


def _out_router_body(attn_ref, rnn_ref, x_ref, g1_ref, sc2_ref, sh2_ref, ag_ref, rg_ref, wo_ref, n2_ref,
                     rw_ref, rb_ref, x1_ref, h2_ref, gate_ref):
    mixed = jnp.concatenate([_rms(attn_ref[...], ag_ref[...]), _rms(rnn_ref[...], rg_ref[...])], axis=1)
    x1 = x_ref[...] + g1_ref[...] * jnp.dot(mixed.astype(BF16), wo_ref[...], preferred_element_type=F32)
    x1_ref[...] = x1
    h2 = _rms(x1, n2_ref[...]) * (1.0 + sc2_ref[...]) + sh2_ref[...]
    h2_ref[...] = h2.astype(BF16)
    z = jnp.dot(h2, rw_ref[...], preferred_element_type=F32, precision=HIGHEST) + rb_ref[...]
    lane = lax.broadcasted_iota(I32, z.shape, 1)
    neg_inf = -jnp.inf
    gl = jnp.where(lane < N_GROUPS, z, neg_inf)
    gmax = jnp.max(gl, axis=1, keepdims=True)
    gidx = jnp.min(jnp.where(gl == gmax, lane, LANES), axis=1, keepdims=True)
    gprob = 1.0 / jnp.sum(jnp.exp(gl - gmax), axis=1, keepdims=True)
    lo = EXPERT_LANE0 + EXPERTS_PER_GROUP * gidx
    el = jnp.where(lane >= lo, jnp.where(lane < lo + EXPERTS_PER_GROUP, z, neg_inf), neg_inf)
    m1 = jnp.max(el, axis=1, keepdims=True)
    j1 = jnp.min(jnp.where(el == m1, lane, LANES), axis=1, keepdims=True)
    el2 = jnp.where(lane == j1, neg_inf, el)
    m2 = jnp.max(el2, axis=1, keepdims=True)
    j2 = jnp.min(jnp.where(el2 == m2, lane, LANES), axis=1, keepdims=True)
    e21 = jnp.exp(m2 - m1)
    w1 = 1.0 / (1.0 + e21)
    gate_ref[...] = jnp.where(lane == j1, w1 * gprob, jnp.where(lane == j2, e21 * w1 * gprob, 0.0))


def out_proj_router(attn, rnn, x, g1, sc2, sh2, attn_g, rnn_g, w_out, norm2_g, router_w, router_b, tt):
    n, t, d = x.shape
    c = attn.shape[2]
    per_row = g1.shape[1] != 1
    mod_spec = (pl.BlockSpec((None, tt, d), lambda b, i: (b, i, 0)) if per_row
                else pl.BlockSpec((None, 1, d), lambda b, i: (b, 0, 0)))
    row = lambda w_: pl.BlockSpec((None, tt, w_), lambda b, i: (b, i, 0))
    const = lambda shape: pl.BlockSpec(shape, lambda b, i: (0,) * len(shape))
    return pl.pallas_call(
        _out_router_body,
        grid=(n, t // tt),
        in_specs=[row(c), row(c), row(d), mod_spec, mod_spec, mod_spec, const((1, c)), const((1, c)),
                  const(w_out.shape), const((1, d)), const(router_w.shape), const((1, LANES))],
        out_specs=[row(d), row(d), row(LANES)],
        out_shape=[jax.ShapeDtypeStruct((n, t, d), F32), jax.ShapeDtypeStruct((n, t, d), BF16),
                   jax.ShapeDtypeStruct((n, t, LANES), F32)],
        compiler_params=_cparams("parallel", "arbitrary"),
        name="out_proj_router",
    )(attn, rnn, x, g1, sc2, sh2, attn_g.reshape(1, c), rnn_g.reshape(1, c), w_out, norm2_g.reshape(1, d),
      router_w, router_b)


def _moe_body(h2_ref, gate_ref, x1_ref, g2_ref, wgu_ref, wd_ref, fg_ref, y_ref, acc_ref):
    e = pl.program_id(2)
    d_exp = wd_ref.shape[0]

    @pl.when(e == 0)
    def _():
        acc_ref[...] = jnp.zeros(acc_ref.shape, F32)

    gates = gate_ref[...]
    lane = lax.broadcasted_iota(I32, gates.shape, 1)
    gcol = jnp.sum(jnp.where(lane == e + EXPERT_LANE0, gates, 0.0), axis=1, keepdims=True)
    gu = jnp.dot(h2_ref[...], wgu_ref[...], preferred_element_type=F32)
    gpart, upart = gu[:, :d_exp], gu[:, d_exp:]
    act = (gpart * _sigmoid(gpart) * upart).astype(BF16)
    acc_ref[...] += gcol * jnp.dot(act, wd_ref[...], preferred_element_type=F32)

    @pl.when(e == pl.num_programs(2) - 1)
    def _():
        x2 = x1_ref[...] + g2_ref[...] * acc_ref[...]
        y_ref[...] = _rms(x2, fg_ref[...])


def moe_final(h2, gates, x1, g2, w_gu, w_down, final_g, tt):
    n, t, d = x1.shape
    n_exp = w_gu.shape[0]
    per_row = g2.shape[1] != 1
    mod_spec = (pl.BlockSpec((None, tt, d), lambda b, i, e: (b, i, 0)) if per_row
                else pl.BlockSpec((None, 1, d), lambda b, i, e: (b, 0, 0)))
    row = lambda w_: pl.BlockSpec((None, tt, w_), lambda b, i, e: (b, i, 0))
    return pl.pallas_call(
        _moe_body,
        grid=(n, t // tt, n_exp),
        in_specs=[row(d), row(LANES), row(d), mod_spec,
                  pl.BlockSpec((None,) + w_gu.shape[1:], lambda b, i, e: (e, 0, 0)),
                  pl.BlockSpec((None,) + w_down.shape[1:], lambda b, i, e: (e, 0, 0)),
                  pl.BlockSpec((1, d), lambda b, i, e: (0, 0))],
        out_specs=row(d),
        out_shape=jax.ShapeDtypeStruct((n, t, d), F32),
        scratch_shapes=[pltpu.VMEM((tt, d), F32)],
        compiler_params=_cparams("parallel", "arbitrary", "arbitrary"),
        name="moe_final",
    )(h2, gates, x1, g2, w_gu, w_down, final_g.reshape(1, d))


def _tile(t, pref):
    return pref if t % pref == 0 else t


def _layer(x, mods, pos, attend, conv_prev, h_prev, n_seq, lw, final_g):
    sh1, sc1, g1, sh2, sc2, g2 = mods
    n, t, d = x.shape
    t_seq = n * t // n_seq
    seqs = lambda a: a.reshape(n_seq, t_seq, a.shape[-1])
    q, k, v, kb, vt, iq, ikw, ikb, xr, yg = in_proj(x, sc1, sh1, lw["norm1_g"], lw["w_in"], rope_tables(pos),
                                                     _tile(t, 512))
    attn = attend(q, iq, ikw, k, v, kb, vt, ikb)
    rnn, h_new = rglru(seqs(xr), seqs(yg), conv_prev, h_prev, lw["conv_w"], lw["conv_b"], lw["rg_wa"], lw["rg_ba"],
                       lw["rg_wx"], lw["rg_bx"], lw["rg_lambda"], _tile(t_seq, 512))
    x1, h2, gates = out_proj_router(attn, rnn.reshape(xr.shape), x, g1, sc2, sh2, lw["attn_out_g"], lw["rnn_out_g"],
                                    lw["w_out"], lw["norm2_g"], lw["router_w"], lw["router_b"], _tile(t, 512))
    y = moe_final(h2, gates, x1, g2, lw["w_gu"], lw["w_down"], final_g, _tile(t, 1024))
    conv_new = seqs(xr)[:, t_seq - (CONV_WIDTH - 1):, :]
    return (seqs(y), seqs(k).reshape(n_seq, t_seq, N_KV_HEADS, HEAD_DIM),
            seqs(v).reshape(n_seq, t_seq, N_KV_HEADS, HEAD_DIM), seqs(ikw)[:, :, :IDX_HEAD_DIM], conv_new,
            h_new[:, 0, :])


def kernel(x_prompt, x_sample, cache_k, cache_v, cache_idx_k, state_conv, state_rglru, page_table, c_prompt,
           c_sample, ada_w, ada_b, norm1_g, w_in, conv_w, conv_b, rg_wa, rg_ba, rg_wx, rg_bx, rg_lambda, attn_out_g,
           rnn_out_g, w_out, norm2_g, router_grp_w, router_grp_b, router_exp_w, router_exp_b, exp_w_gate, exp_w_up,
           exp_w_down, final_g):
    depth = ada_w.shape[0]
    assert depth == 1, "this implementation handles the single-layer configuration"
    n_p, t_p, d = x_prompt.shape
    n_s, t_s, _ = x_sample.shape
    n_pages = page_table.shape[1]
    past = n_pages * PAGE_SIZE
    l = 0
    router_w = jnp.concatenate([router_grp_w[l], router_exp_w[l],
                                jnp.zeros((d, LANES - N_GROUPS - N_EXPERTS), F32)], axis=1)
    router_b = jnp.concatenate([router_grp_b[l], router_exp_b[l],
                                jnp.zeros((LANES - N_GROUPS - N_EXPERTS,), F32)]).reshape(1, LANES)
    lw = dict(norm1_g=norm1_g[l], w_in=prep_w_in(w_in[l]), conv_w=conv_w[l], conv_b=conv_b[l], rg_wa=rg_wa[l],
              rg_ba=rg_ba[l], rg_wx=rg_wx[l], rg_bx=rg_bx[l], rg_lambda=rg_lambda[l], attn_out_g=attn_out_g[l],
              rnn_out_g=rnn_out_g[l], w_out=w_out[l].astype(BF16), norm2_g=norm2_g[l], router_w=router_w,
              router_b=router_b,
              w_gu=jnp.concatenate([exp_w_gate[l], exp_w_up[l]], axis=2).astype(BF16),
              w_down=exp_w_down[l].astype(BF16))

    mod = ada_mod(jnp.concatenate([c_prompt, c_sample], axis=0), ada_w[l], ada_b[l])
    mods_p = [m[:n_p, None, :] for m in jnp.split(mod, 6, axis=1)]
    mods_s = [jnp.repeat(m[n_p:], t_s, axis=0)[None] for m in jnp.split(mod, 6, axis=1)]

    def attend_prompt(q, iq, ikw, k, v, kb, vt, ikb):
        iw_t = jnp.swapaxes(ikw[:, :, IDX_HEAD_DIM:IDX_HEAD_DIM + N_IDX_HEADS], 1, 2)
        return prompt_attention(q, iq, iw_t, kb, vt, ikb, _tile(t_p, 256))

    def attend_sample(q, iq, ikw, k, v, kb, vt, ikb):
        seqs = lambda a: a.reshape(n_s, t_s, a.shape[-1])
        n_pool = cache_k.shape[1]
        feat_major = lambda c: jnp.swapaxes(c.reshape(n_pool, PAGE_SIZE, -1), 1, 2)
        out = sample_attention(page_table, seqs(q), seqs(iq), seqs(ikw), seqs(k), seqs(v),
                               feat_major(cache_k[l]), feat_major(cache_v[l]), feat_major(cache_idx_k[l]))
        return out.reshape(1, n_s * t_s, N_HEADS * HEAD_DIM)

    c_rnn = conv_w.shape[2]
    ys, ks, vs, iks, cvs, hs = _layer(
        x_sample.reshape(1, n_s * t_s, d), mods_s, jnp.tile(past + jnp.arange(t_s), n_s), attend_sample,
        state_conv[l], state_rglru[l], n_s, lw, final_g)
    yp, kp, vp, ikp, cvp, hp = _layer(
        x_prompt, mods_p, jnp.arange(t_p), attend_prompt,
        jnp.zeros((n_p, CONV_WIDTH - 1, c_rnn), F32), jnp.zeros((n_p, c_rnn), F32), n_p, lw, final_g)

    st = lambda a: a[None]
    return (yp, ys, st(kp), st(vp), st(ikp), st(cvp), st(hp), st(ks), st(vs), st(iks), st(cvs), st(hs))
```

```python
import functools

import jax
import jax.numpy as jnp
from jax import lax
from jax.experimental import pallas as pl
from jax.experimental.pallas import tpu as pltpu

F32, BF16, I32, I16 = jnp.float32, jnp.bfloat16, jnp.int32, jnp.int16
HIGHEST = lax.Precision.HIGHEST

HEAD_DIM = 64
N_HEADS = 8
N_KV_HEADS = 2
HEADS_PER_KV = N_HEADS // N_KV_HEADS
N_IDX_HEADS = 8
IDX_HEAD_DIM = 32
TOPK_MAX = 256
ROPE_THETA = 10000.0
CONV_WIDTH = 4
RG_BLOCKS = 8
RG_C = 8.0
N_GROUPS = 4
EXPERTS_PER_GROUP = 8
N_EXPERTS = N_GROUPS * EXPERTS_PER_GROUP
PAGE_SIZE = 128
NORM_EPS = 1e-6

LANES = 128
SUBLANES = 8
VMEM_LIMIT = 48 * 1024 * 1024
INT_MIN = -(2 ** 31)
I16_MIN = -(2 ** 15)
NEG_BIG = -1e30
EXPERT_LANE0 = N_GROUPS
BF16_SUBLANES = 16
V_AUG_ROWS = HEAD_DIM + BF16_SUBLANES
ATT_CHUNK_HEADS = 2
COUNT_TILES = 2
LOG2_E = 1.4426950408889634
Q_SCALE = HEAD_DIM ** -0.5 * LOG2_E


def _cparams(*sem):
    return pltpu.CompilerParams(dimension_semantics=sem, vmem_limit_bytes=VMEM_LIMIT)


def _rms(x, g):
    return x * lax.rsqrt(jnp.mean(x * x, axis=-1, keepdims=True) + NORM_EPS) * g


def _sigmoid(x):
    return 1.0 / (1.0 + jnp.exp(-x))


def _float_key(s):
    b = pltpu.bitcast(s + 0.0, I32)
    return b ^ ((b >> 31) & 0x7FFFFFFF)


def _ada_body(c_ref, w_ref, b_ref, o_ref):
    c = c_ref[...]
    s = c * _sigmoid(c)
    o_ref[...] = jnp.dot(s, w_ref[...], preferred_element_type=F32, precision=HIGHEST) + b_ref[...]


def ada_mod(c, w, b):
    m, d = c.shape
    n_chunks = w.shape[1] // d
    return pl.pallas_call(
        _ada_body,
        grid=(n_chunks,),
        in_specs=[pl.BlockSpec((m, d), lambda j: (0, 0)),
                  pl.BlockSpec((d, d), lambda j: (0, j)),
                  pl.BlockSpec((1, d), lambda j: (0, j))],
        out_specs=pl.BlockSpec((m, d), lambda j: (0, j)),
        out_shape=jax.ShapeDtypeStruct((m, w.shape[1]), F32),
        compiler_params=_cparams("arbitrary"),
        name="ada_mod",
    )(c, w, b.reshape(1, -1))


_C_Q, _C_QR, _C_K, _C_KR, _C_V, _C_IQ, _C_IQR, _C_IKW, _C_IKR, _C_XR, _C_YG, _C_END = (
    0, 512, 1024, 1152, 1280, 1408, 1664, 1920, 2048, 2176, 2688, 3200)


def _rot_half_cols(w, head_dim):
    d_in, n = w.shape
    wh = w.reshape(d_in, n // head_dim, 2, head_dim // 2)
    return jnp.stack([-wh[:, :, 1], wh[:, :, 0]], axis=2).reshape(d_in, n)


def prep_w_in(w_in):
    d_in = w_in.shape[0]
    q, k, v, iq, ik, iw, xr, yg = jnp.split(w_in, [512, 640, 768, 1024, 1056, 1064, 1576], axis=1)
    z = lambda n: jnp.zeros((d_in, n), w_in.dtype)
    ikw = jnp.concatenate([ik, iw, z(LANES - IDX_HEAD_DIM - N_IDX_HEADS)], axis=1)
    ikr = jnp.concatenate([_rot_half_cols(ik, IDX_HEAD_DIM), z(LANES - IDX_HEAD_DIM)], axis=1)
    w = jnp.concatenate([q, _rot_half_cols(q, HEAD_DIM), k, _rot_half_cols(k, HEAD_DIM), v,
                         iq, _rot_half_cols(iq, IDX_HEAD_DIM), ikw, ikr, xr, yg], axis=1)
    return w.astype(BF16)


def rope_tables(pos):
    pos = pos.astype(F32)[:, None]

    def cs(half, reps):
        inv = ROPE_THETA ** (-jnp.arange(half, dtype=F32) / half)
        ang = pos * inv[None, :]
        return jnp.tile(jnp.cos(ang), (1, 2 * reps)), jnp.tile(jnp.sin(ang), (1, 2 * reps))

    cq, sq = cs(HEAD_DIM // 2, LANES // HEAD_DIM)
    ci, si = cs(IDX_HEAD_DIM // 2, LANES // IDX_HEAD_DIM)
    lane = jnp.arange(LANES)[None, :]
    w_scale = (IDX_HEAD_DIM ** -0.5) * (N_IDX_HEADS ** -0.5)
    cx = jnp.where(lane < IDX_HEAD_DIM, ci, jnp.where(lane < IDX_HEAD_DIM + N_IDX_HEADS, w_scale, 0.0))
    sx = jnp.where(lane < IDX_HEAD_DIM, si, 0.0)
    return cq, sq, ci, si, cx.astype(F32), sx.astype(F32)


def _in_proj_body(x_ref, sc_ref, sh_ref, g_ref, w_ref, cq_ref, sq_ref, ci_ref, si_ref, cx_ref, sx_ref,
                  q_ref, k_ref, v_ref, kb_ref, vt_ref, iq_ref, ikw_ref, ikb_ref, xr_ref, yg_ref):
    x = x_ref[...]
    h = (_rms(x, g_ref[...]) * (1.0 + sc_ref[...]) + sh_ref[...]).astype(BF16)

    def seg(a, b):
        return jnp.dot(h, w_ref[:, a:b], preferred_element_type=F32)

    cq, sq = cq_ref[...], sq_ref[...]
    cq4 = jnp.concatenate([cq] * 4, axis=1)
    sq4 = jnp.concatenate([sq] * 4, axis=1)
    q = (seg(_C_Q, _C_QR) * cq4 + seg(_C_QR, _C_K) * sq4) * Q_SCALE
    q_ref[...] = q.astype(BF16)
    k = seg(_C_K, _C_KR) * cq + seg(_C_KR, _C_V) * sq
    k_ref[...] = k
    kb_ref[...] = k.astype(BF16)
    v = seg(_C_V, _C_IQ)
    v_ref[...] = v
    vt_ref[...] = v.T.astype(BF16)
    ci2 = jnp.concatenate([ci_ref[...]] * 2, axis=1)
    si2 = jnp.concatenate([si_ref[...]] * 2, axis=1)
    iq_ref[...] = (seg(_C_IQ, _C_IQR) * ci2 + seg(_C_IQR, _C_IKW) * si2).astype(BF16)
    ikw = seg(_C_IKW, _C_IKR) * cx_ref[...] + seg(_C_IKR, _C_XR) * sx_ref[...]
    ikw_ref[...] = ikw
    ikb_ref[...] = ikw.astype(BF16)
    xr_ref[...] = seg(_C_XR, _C_YG)
    yg_ref[...] = seg(_C_YG, _C_END)


def in_proj(x, sc, sh, g, w, tables, tt):
    n, t, d = x.shape
    per_row = sc.shape[1] != 1
    mod_spec = (pl.BlockSpec((None, tt, d), lambda b, i: (b, i, 0)) if per_row
                else pl.BlockSpec((None, 1, d), lambda b, i: (b, 0, 0)))
    row = lambda w_: pl.BlockSpec((None, tt, w_), lambda b, i: (b, i, 0))
    tab = pl.BlockSpec((tt, LANES), lambda b, i: (i, 0))
    outs = [(512, BF16), (128, F32), (128, F32), (128, BF16), None, (256, BF16), (128, F32), (128, BF16),
            (512, F32), (512, F32)]
    out_shape, out_specs = [], []
    for o in outs:
        if o is None:
            out_shape.append(jax.ShapeDtypeStruct((n, LANES, t), BF16))
            out_specs.append(pl.BlockSpec((None, LANES, tt), lambda b, i: (b, 0, i)))
        else:
            out_shape.append(jax.ShapeDtypeStruct((n, t, o[0]), o[1]))
            out_specs.append(row(o[0]))
    return pl.pallas_call(
        _in_proj_body,
        grid=(n, t // tt),
        in_specs=[row(d), mod_spec, mod_spec,
                  pl.BlockSpec((1, d), lambda b, i: (0, 0)),
                  pl.BlockSpec(w.shape, lambda b, i: (0, 0))] + [tab] * 6,
        out_specs=out_specs,
        out_shape=out_shape,
        compiler_params=_cparams("parallel", "arbitrary"),
        name="in_proj",
    )(x, sc, sh, g.reshape(1, d), w, *tables)


def _pattn_body(q_ref, iq_ref, iwt_ref, k_ref, vt_ref, ik_ref, o_ref,
                hi_ref, lo_ref, lb_ref, p_ref, m_ref, acc_ref, mm_ref, *, n_sel, idx_bits):
    qi = pl.program_id(1)
    tq = q_ref.shape[0]
    tk = tq
    n_tiles = qi + 1

    iq_t = iq_ref[...].astype(F32).T
    iq_st = jnp.concatenate([iq_t[IDX_HEAD_DIM * h:IDX_HEAD_DIM * (h + 1)] for h in range(N_IDX_HEADS)],
                            axis=1).astype(BF16)
    w_t = iwt_ref[...]
    q_t = q_ref[...].astype(F32).T
    zero_half = jnp.zeros((HEAD_DIM, tq), F32)
    q_pad = []
    for g in range(N_KV_HEADS):
        cols = []
        for hh in range(HEADS_PER_KV):
            h = g * HEADS_PER_KV + hh
            qh = q_t[HEAD_DIM * h:HEAD_DIM * (h + 1)]
            cols.append(jnp.concatenate([qh, zero_half] if g == 0 else [zero_half, qh], axis=0))
        q_pad.append(jnp.concatenate(cols, axis=1).astype(BF16))

    row_s = lax.broadcasted_iota(I32, (tk, tq), 0)
    col_t = lax.broadcasted_iota(I32, (tk, tq), 1)
    causal = row_s <= col_t

    def tile_off(j):
        return pl.multiple_of(j * tk, tk)

    def walk_tiles(step):
        def pair(p, c):
            step(2 * p, 0, False)
            step(2 * p + 1, 1, False)
            return c

        lax.fori_loop(0, qi // 2, pair, 0)

        @pl.when(qi % 2 == 0)
        def _():
            step(qi, 0, True)

        @pl.when(qi % 2 == 1)
        def _():
            step(qi - 1, 0, False)
            step(qi, 1, True)

    def index_dots(j, slot):
        ik = ik_ref[pl.ds(tile_off(j), tk), :][:, :IDX_HEAD_DIM]
        mm_ref[slot] = jnp.dot(ik, iq_st, preferred_element_type=F32)

    def p1(j, slot, last):
        if not last:
            index_dots(j + 1, 1 - slot)
        sc = w_t[0:1, :] * jnp.maximum(mm_ref[slot, :, 0:tq], 0.0)
        for h in range(1, N_IDX_HEADS):
            sc = sc + w_t[h:h + 1, :] * jnp.maximum(mm_ref[slot, :, h * tq:(h + 1) * tq], 0.0)
        key = _float_key(sc)
        if last:
            key = jnp.where(causal, key, INT_MIN)
        rows = pl.ds(tile_off(j), tk)
        hi_ref[rows, :] = (key >> 16).astype(I16)
        lo_ref[rows, :] = ((key & 0xFFFF) + I16_MIN).astype(I16)

    index_dots(0, 0)
    walk_tiles(p1)

    span = COUNT_TILES * tk
    n_spans = (n_tiles + COUNT_TILES - 1) // COUNT_TILES
    groups = span // BF16_SUBLANES
    pad_rows = pl.ds(tile_off(n_tiles), tk)
    hi_ref[pad_rows, :] = jnp.full((tk, tq), I16_MIN, I16)
    lo_ref[pad_rows, :] = jnp.full((tk, tq), I16_MIN, I16)

    def rows16(v):
        return jnp.broadcast_to(v.astype(I16), (BF16_SUBLANES, tq))[None]

    def span_rows(j):
        return pl.ds(pl.multiple_of(j * span, span), span)

    def tile16(ref, j):
        return ref[span_rows(j), :].reshape(groups, BF16_SUBLANES, tq)

    one16, zero16 = jnp.int16(1), jnp.int16(0)

    def count(pred):
        def body(j, acc):
            marks = pred(j)
            parts = [marks[g] for g in range(groups)]
            while len(parts) > 1:
                parts = [parts[i] + parts[i + 1] for i in range(0, len(parts), 2)]
            return acc + parts[0]

        acc = lax.fori_loop(0, n_spans, body, jnp.zeros((BF16_SUBLANES, tq), I16))
        return jnp.sum(acc.astype(I32).astype(F32), axis=0, keepdims=True)

    def bisect16(ref, need):
        def step(b, thr):
            cand = thr + jnp.left_shift(jnp.int32(1), 15 - b)
            cand_b = rows16(cand)
            c = count(lambda j: jnp.where(tile16(ref, j) >= cand_b, one16, zero16))
            return jnp.where(c >= need, cand, thr)
        return lax.fori_loop(0, 16, step, jnp.full((1, tq), I16_MIN, I32))

    thr_hi = bisect16(hi_ref, n_sel)
    hi_b = rows16(thr_hi)
    c_above = count(lambda j: jnp.where(tile16(hi_ref, j) > hi_b, one16, zero16))

    def p2(j, c):
        rows = span_rows(j)
        lb_ref[rows, :] = jnp.where(hi_ref[rows, :] == thr_hi.astype(I16), lo_ref[rows, :], jnp.int16(I16_MIN))
        return c

    lax.fori_loop(0, n_spans, p2, 0)
    thr_lo = bisect16(lb_ref, n_sel - c_above)
    lo_b = rows16(thr_lo)
    c_gt = c_above + count(lambda j: jnp.where(tile16(lb_ref, j) > lo_b, one16, zero16))
    ties_left = n_sel - c_gt

    def is_tie(j):
        return jnp.where(tile16(hi_ref, j) == hi_b, jnp.where(tile16(lo_ref, j) == lo_b, one16, zero16), zero16)

    c_tie = count(is_tie)
    p_ref[...] = jnp.full((1, tq), 2 ** idx_bits - 1, I32)
    has_thr = jnp.where(thr_hi > I16_MIN, 1, jnp.where(thr_lo > I16_MIN, 1, 0))
    excess = jnp.max(jnp.where(has_thr > 0, c_tie - ties_left, 0.0)) > 0.0

    idx16 = row_s.astype(I16)

    @pl.when(excess)
    def _():
        sub = lax.broadcasted_iota(I32, (span, tq), 0).astype(I16).reshape(groups, BF16_SUBLANES, tq)

        def tb(b, p):
            cand = p | jnp.left_shift(jnp.int32(1), idx_bits - 1 - b)
            c = count(lambda j: jnp.where(sub < rows16(cand - j * span), is_tie(j), zero16))
            return jnp.where(c < ties_left, cand, p)

        p_ref[...] = lax.fori_loop(0, idx_bits, tb, jnp.zeros((1, tq), I32))

    p_last = p_ref[...]
    thr_hi16, thr_lo16 = thr_hi.astype(I16), thr_lo.astype(I16)

    m_ref[...] = jnp.full(m_ref.shape, NEG_BIG, F32)
    acc_ref[...] = jnp.zeros(acc_ref.shape, F32)
    ones_rows = jnp.ones((V_AUG_ROWS - HEAD_DIM, tk), BF16)

    w_c = ATT_CHUNK_HEADS * tq
    chunks = [(g, c * w_c) for g in range(N_KV_HEADS) for c in range(HEADS_PER_KV // ATT_CHUNK_HEADS)]

    def score_chunk(kt, slot, ci):
        g, c0 = chunks[ci]
        mm_ref[slot, :, ci * w_c:(ci + 1) * w_c] = jnp.dot(kt, q_pad[g][:, c0:c0 + w_c],
                                                             preferred_element_type=F32)

    kt0 = k_ref[pl.ds(0, tk), :]
    for ci in range(len(chunks)):
        score_chunk(kt0, 0, ci)

    def attend(j, slot, diag):
        off = tile_off(j)
        hi, lo = hi_ref[pl.ds(off, tk), :], lo_ref[pl.ds(off, tk), :]
        keep_tie = jnp.where(idx16 <= (p_last - j * tk).astype(I16), one16, zero16)
        sel = jnp.where(hi > thr_hi16, one16,
                        jnp.where(hi == thr_hi16,
                                  jnp.where(lo > thr_lo16, one16, jnp.where(lo == thr_lo16, keep_tie, zero16)),
                                  zero16))
        bias = jnp.where(sel.astype(I32) > 0, 0.0, NEG_BIG)
        if diag:
            bias = jnp.where(causal, bias, NEG_BIG)
        vt = vt_ref[:, pl.ds(off, tk)]
        v_aug = [jnp.concatenate([vt[HEAD_DIM * g:HEAD_DIM * (g + 1)], ones_rows], axis=0)
                 for g in range(N_KV_HEADS)]
        bias_c = jnp.concatenate([bias] * ATT_CHUNK_HEADS, axis=1)
        if not diag:
            kt_next = k_ref[pl.ds(tile_off(j + 1), tk), :]
        for ci, (g, c0) in enumerate(chunks):
            if not diag:
                score_chunk(kt_next, 1 - slot, ci)
            cols = slice(c0, c0 + w_c)
            s = mm_ref[slot, :, ci * w_c:(ci + 1) * w_c] + bias_c
            m_old = m_ref[g:g + 1, cols]
            m_new = jnp.maximum(m_old, jnp.max(s, axis=0, keepdims=True))
            alpha = jnp.exp2(m_old - m_new)
            p = jnp.exp2(s - m_new).astype(BF16)
            acc_ref[g, :, cols] = alpha * acc_ref[g, :, cols] + jnp.dot(v_aug[g], p, preferred_element_type=F32)
            m_ref[g:g + 1, cols] = m_new

    walk_tiles(attend)

    rows = []
    for g in range(N_KV_HEADS):
        acc = acc_ref[g]
        o_g = acc[:HEAD_DIM] / acc[HEAD_DIM:HEAD_DIM + 1]
        rows += [o_g[:, hh * tq:(hh + 1) * tq] for hh in range(HEADS_PER_KV)]
    o_ref[...] = jnp.concatenate(rows, axis=0).T


def prompt_attention(q, iq, iw_t, kb, vt, ikb, tq):
    n, t, _ = q.shape
    n_sel = min(TOPK_MAX, t // 4)
    idx_bits = max(1, (t - 1).bit_length())
    body = functools.partial(_pattn_body, n_sel=n_sel, idx_bits=idx_bits)
    return pl.pallas_call(
        body,
        grid=(n, t // tq),
        in_specs=[pl.BlockSpec((None, tq, q.shape[2]), lambda b, i: (b, i, 0)),
                  pl.BlockSpec((None, tq, iq.shape[2]), lambda b, i: (b, i, 0)),
                  pl.BlockSpec((None, N_IDX_HEADS, tq), lambda b, i: (b, 0, i)),
                  pl.BlockSpec((None, t, LANES), lambda b, i: (b, 0, 0)),
                  pl.BlockSpec((None, LANES, t), lambda b, i: (b, 0, 0)),
                  pl.BlockSpec((None, t, LANES), lambda b, i: (b, 0, 0))],
        out_specs=pl.BlockSpec((None, tq, N_HEADS * HEAD_DIM), lambda b, i: (b, i, 0)),
        out_shape=jax.ShapeDtypeStruct((n, t, N_HEADS * HEAD_DIM), F32),
        scratch_shapes=[pltpu.VMEM((t + tq, tq), I16),
                        pltpu.VMEM((t + tq, tq), I16),
                        pltpu.VMEM((t + tq, tq), I16),
                        pltpu.VMEM((1, tq), I32),
                        pltpu.VMEM((N_KV_HEADS, HEADS_PER_KV * tq), F32),
                        pltpu.VMEM((N_KV_HEADS, V_AUG_ROWS, HEADS_PER_KV * tq), F32),
                        pltpu.VMEM((2, tq, N_HEADS * tq), F32)],
        compiler_params=_cparams("parallel", "arbitrary"),
        name="prompt_attention",
    )(q, iq, iw_t, kb, vt, ikb)


Q_PAD = SUBLANES


def _sattn_body(pt_ref, q_ref, iq_ref, iw_ref, kn_ref, vn_ref, ikn_ref, ck_hbm, cv_hbm, cik_hbm, o_ref,
                kbuf, vbuf, ikbuf, s_ref, p_ref, sem, *, n_pages, n_sel, idx_bits):
    b = pl.program_id(0)
    slot = b % 2
    past = n_pages * PAGE_SIZE
    lk = past + LANES

    streams = ((cik_hbm, ikbuf), (ck_hbm, kbuf), (cv_hbm, vbuf))

    def page_copy(kind, p, seq, slt):
        src, dst = streams[kind]
        return pltpu.make_async_copy(src.at[pt_ref[seq, p]], dst.at[slt, :, pl.ds(p * PAGE_SIZE, PAGE_SIZE)],
                                     sem.at[slt, kind])

    def start_pages(seq, slt):
        for kind in range(len(streams)):
            for p in range(n_pages):
                page_copy(kind, p, seq, slt).start()

    def wait_pages(kind):
        for p in range(n_pages):
            page_copy(kind, p, b, slot).wait()

    @pl.when(b == 0)
    def _():
        start_pages(0, 0)

    @pl.when(b + 1 < pl.num_programs(0))
    def _():
        start_pages(b + 1, 1 - slot)

    def new_cols(ref):
        rows = jnp.concatenate([ref[...], jnp.zeros((LANES - Q_PAD, LANES), F32)], axis=0)
        return rows.T

    new_tile = pl.ds(past, LANES)
    kbuf[slot, :, new_tile] = new_cols(kn_ref)
    vbuf[slot, :, new_tile] = new_cols(vn_ref)
    ikbuf[slot, :, new_tile] = new_cols(ikn_ref)[:IDX_HEAD_DIM]

    iq = iq_ref[...].astype(F32)
    iq_st = jnp.concatenate([iq[:, IDX_HEAD_DIM * h:IDX_HEAD_DIM * (h + 1)] for h in range(N_IDX_HEADS)],
                            axis=0).astype(BF16)
    qf = q_ref[...].astype(F32)
    lane128 = lax.broadcasted_iota(I32, (Q_PAD, LANES), 1)
    q_rows = []
    for h in range(N_HEADS):
        g = h // HEADS_PER_KV
        slab = qf[:, LANES * (h // 2):LANES * (h // 2 + 1)]
        if (h % 2) != g:
            slab = pltpu.roll(slab, HEAD_DIM, axis=1)
        keep = (lane128 < HEAD_DIM) if g == 0 else (lane128 >= HEAD_DIM)
        q_rows.append(jnp.where(keep, slab, 0.0))
    q_pad = jnp.concatenate(q_rows, axis=0).astype(BF16)
    iw = iw_ref[...]

    key_pos = lax.broadcasted_iota(I32, (Q_PAD, lk), 1)
    q_row = lax.broadcasted_iota(I32, (Q_PAD, lk), 0)
    admissible = key_pos <= past + q_row

    wait_pages(0)
    d = jnp.dot(iq_st, ikbuf[slot].astype(BF16), preferred_element_type=F32)
    sc = None
    for h in range(N_IDX_HEADS):
        w_h = iw[:, IDX_HEAD_DIM + h:IDX_HEAD_DIM + h + 1]
        term = w_h * jnp.maximum(d[Q_PAD * h:Q_PAD * (h + 1)], 0.0)
        sc = term if sc is None else sc + term
    s_ref[...] = jnp.where(admissible, _float_key(sc), INT_MIN)

    lane_pos = lax.broadcasted_iota(I32, (Q_PAD, LANES), 1)

    def count(pred):
        terms = [pred(s_ref[:, c * LANES:(c + 1) * LANES], c * LANES) for c in range(lk // LANES)]
        while len(terms) > 1:
            terms = [terms[i] + terms[i + 1] if i + 1 < len(terms) else terms[i] for i in range(0, len(terms), 2)]
        return jnp.sum(terms[0], axis=1, keepdims=True)

    def bisect(i, thr):
        cand = jnp.broadcast_to(thr + jnp.left_shift(jnp.int32(1), 31 - i), (Q_PAD, LANES))
        c = count(lambda t, c0: jnp.where(t >= cand, 1.0, 0.0))
        return jnp.where(c >= n_sel, cand[:, :1], thr)

    thr = lax.fori_loop(0, 32, bisect, jnp.full((Q_PAD, 1), INT_MIN, I32))
    thr_b = jnp.broadcast_to(thr, (Q_PAD, LANES))
    ties_left = n_sel - count(lambda t, c0: jnp.where(t > thr_b, 1.0, 0.0))
    c_tie = count(lambda t, c0: jnp.where(t == thr_b, 1.0, 0.0))
    p_ref[...] = jnp.full((Q_PAD, 1), 2 ** idx_bits - 1, I32)

    @pl.when(jnp.max(jnp.where(thr > INT_MIN, c_tie - ties_left, 0.0)) > 0.0)
    def _():
        def tb(i, pidx):
            cand = jnp.broadcast_to(pidx | jnp.left_shift(jnp.int32(1), idx_bits - 1 - i), (Q_PAD, LANES))
            c = count(lambda t, c0: jnp.where(t == thr_b, jnp.where(lane_pos + c0 < cand, 1.0, 0.0), 0.0))
            return jnp.where(c < ties_left, cand[:, :1], pidx)

        p_ref[...] = lax.fori_loop(0, idx_bits, tb, jnp.zeros((Q_PAD, 1), I32))

    p_last = p_ref[...]
    t = s_ref[...]
    keep_tie = jnp.where(key_pos <= p_last, 0.0, NEG_BIG)
    bias = jnp.where(t > thr, 0.0, jnp.where(t == thr, keep_tie, NEG_BIG))
    bias = jnp.where(admissible, bias, NEG_BIG)

    wait_pages(1)
    s = jnp.dot(q_pad, kbuf[slot].astype(BF16), preferred_element_type=F32)
    s = s + jnp.concatenate([bias] * N_HEADS, axis=0)
    m = jnp.max(s, axis=1, keepdims=True)
    pexp = jnp.exp2(s - m)
    l = jnp.sum(pexp, axis=1, keepdims=True)
    wait_pages(2)
    o = lax.dot_general(pexp.astype(BF16), vbuf[slot].astype(BF16), (((1,), (1,)), ((), ())),
                        preferred_element_type=F32)
    o_ref[...] = o / l


def sample_attention(page_table, q, iq, ikw, k_new, v_new, cache_k, cache_v, cache_ik):
    n, n_new, _ = q.shape
    assert n_new <= Q_PAD
    n_pages = page_table.shape[1]
    past = n_pages * PAGE_SIZE
    lk = past + LANES
    n_sel = min(TOPK_MAX, (past + n_new) // 4)
    idx_bits = max(1, (lk - 1).bit_length())
    body = functools.partial(_sattn_body, n_pages=n_pages, n_sel=n_sel, idx_bits=idx_bits)
    pad = lambda a: jnp.pad(a, ((0, 0), (0, Q_PAD - n_new), (0, 0)))
    q, iq, ikw, k_new, v_new = pad(q), pad(iq), pad(ikw), pad(k_new), pad(v_new)
    seq = lambda w_: pl.BlockSpec((None, Q_PAD, w_), lambda b, pt: (b, 0, 0))
    hbm = pl.BlockSpec(memory_space=pl.ANY)
    grid_spec = pltpu.PrefetchScalarGridSpec(
        num_scalar_prefetch=1,
        grid=(n,),
        in_specs=[seq(q.shape[2]), seq(iq.shape[2]), seq(LANES), seq(LANES), seq(LANES), seq(LANES), hbm, hbm, hbm],
        out_specs=pl.BlockSpec((None, N_HEADS * Q_PAD, LANES), lambda b, pt: (b, 0, 0)),
        scratch_shapes=[pltpu.VMEM((2, LANES, lk), F32),
                        pltpu.VMEM((2, LANES, lk), F32),
                        pltpu.VMEM((2, IDX_HEAD_DIM, lk), F32),
                        pltpu.VMEM((Q_PAD, lk), I32),
                        pltpu.VMEM((Q_PAD, 1), I32),
                        pltpu.SemaphoreType.DMA((2, 3))],
    )
    raw = pl.pallas_call(
        body,
        grid_spec=grid_spec,
        out_shape=jax.ShapeDtypeStruct((n, N_HEADS * Q_PAD, LANES), F32),
        compiler_params=_cparams("arbitrary"),
        name="sample_attention",
    )(page_table, q, iq, ikw, k_new, v_new, ikw, cache_k, cache_v, cache_ik)
    raw = raw.reshape(n, N_HEADS, Q_PAD, N_KV_HEADS, HEAD_DIM)[:, :, :n_new]
    per_head = jnp.stack([raw[:, h, :, h // HEADS_PER_KV] for h in range(N_HEADS)], axis=2)
    return per_head.reshape(n, n_new, N_HEADS * HEAD_DIM)


def _rglru_body(xr_ref, yg_ref, cprev_ref, hprev_ref, cw_ref, cb_ref, wa_ref, ba_ref, wx_ref, bx_ref, lam_ref,
                y_ref, hlast_ref, xbuf, a_buf, b_buf, h_buf, h_carry):
    i = pl.program_id(1)
    tt = xr_ref.shape[0]
    halo = SUBLANES

    @pl.when(i == 0)
    def _():
        xbuf[pl.ds(0, halo), :] = jnp.zeros((halo, xbuf.shape[1]), F32)
        xbuf[pl.ds(halo - (CONV_WIDTH - 1), CONV_WIDTH - 1), :] = cprev_ref[...]
        h_carry[...] = hprev_ref[...]

    x = xr_ref[...]
    xbuf[pl.ds(halo, tt), :] = x
    xc = cb_ref[...] + cw_ref[CONV_WIDTH - 1:CONV_WIDTH, :] * x
    for j in range(CONV_WIDTH - 1):
        xc = xc + cw_ref[j:j + 1, :] * xbuf[pl.ds(halo - (CONV_WIDTH - 1) + j, tt), :]
    xcb = xc.astype(BF16)
    r = _sigmoid(jnp.dot(xcb, wa_ref[...], preferred_element_type=F32) + ba_ref[...])
    ig = _sigmoid(jnp.dot(xcb, wx_ref[...], preferred_element_type=F32) + bx_ref[...])
    z = -lam_ref[...]
    softplus = jnp.maximum(z, 0.0) + jnp.log1p(jnp.exp(-jnp.abs(z)))
    log_a = -RG_C * r * softplus
    a = jnp.exp(log_a)
    a_buf[...] = a
    b_buf[...] = jnp.sqrt(jnp.tanh(-log_a) * (1.0 + a * a)) * (ig * xc)

    def step(t, h):
        h = a_buf[pl.ds(t, 1), :] * h + b_buf[pl.ds(t, 1), :]
        h_buf[pl.ds(t, 1), :] = h
        return h

    h_fin = lax.fori_loop(0, tt, step, h_carry[...], unroll=min(8, tt))
    h_carry[...] = h_fin
    hlast_ref[...] = h_fin
    y_ref[...] = h_buf[...] * jax.nn.gelu(yg_ref[...])
    xbuf[pl.ds(halo - (CONV_WIDTH - 1), CONV_WIDTH - 1), :] = xbuf[pl.ds(halo + tt - (CONV_WIDTH - 1), CONV_WIDTH - 1), :]


def _block_diag(w):
    k, c, d = w.shape
    eye = jnp.eye(k, dtype=w.dtype)
    return (eye[:, None, :, None] * w[:, :, None, :]).reshape(k * c, k * d)


def rglru(xr, yg, conv_prev, h_prev, conv_w, conv_b, rg_wa, rg_ba, rg_wx, rg_bx, rg_lambda, tt):
    n, t, c = xr.shape
    row = pl.BlockSpec((None, tt, c), lambda b, i: (b, i, 0))
    const = lambda shape: pl.BlockSpec(shape, lambda b, i: (0,) * len(shape))
    vec = lambda a: a.reshape(1, c)
    return pl.pallas_call(
        _rglru_body,
        grid=(n, t // tt),
        in_specs=[row, row,
                  pl.BlockSpec((None, CONV_WIDTH - 1, c), lambda b, i: (b, 0, 0)),
                  pl.BlockSpec((None, 1, c), lambda b, i: (b, 0, 0)),
                  const((CONV_WIDTH, c)), const((1, c)), const((c, c)), const((1, c)), const((c, c)),
                  const((1, c)), const((1, c))],
        out_specs=[row, pl.BlockSpec((None, 1, c), lambda b, i: (b, 0, 0))],
        out_shape=[jax.ShapeDtypeStruct((n, t, c), F32), jax.ShapeDtypeStruct((n, 1, c), F32)],
        scratch_shapes=[pltpu.VMEM((tt + SUBLANES, c), F32), pltpu.VMEM((tt, c), F32), pltpu.VMEM((tt, c), F32),
                        pltpu.VMEM((tt, c), F32), pltpu.VMEM((1, c), F32)],
        compiler_params=_cparams("parallel", "arbitrary"),
        name="rglru",
    )(xr, yg, conv_prev, h_prev.reshape(n, 1, c), conv_w, vec(conv_b), _block_diag(rg_wa).astype(BF16), vec(rg_ba),
      _block_diag(rg_wx).astype(BF16), vec(rg_bx), vec(rg_lambda))


def _out_router_body(attn_ref, rnn_ref, x_ref, g1_ref, sc2_ref, sh2_ref, ag_ref, rg_ref, wo_ref, n2_ref,
                     rw_ref, rb_ref, x1_ref, h2_ref, gate_ref):
    mixed = jnp.concatenate([_rms(attn_ref[...], ag_ref[...]), _rms(rnn_ref[...], rg_ref[...])], axis=1)
    x1 = x_ref[...] + g1_ref[...] * jnp.dot(mixed.astype(BF16), wo_ref[...], preferred_element_type=F32)
    x1_ref[...] = x1
    h2 = _rms(x1, n2_ref[...]) * (1.0 + sc2_ref[...]) + sh2_ref[...]
    h2_hi = h2.astype(BF16)
    h2_ref[...] = h2_hi
    h2_lo = (h2 - h2_hi.astype(F32)).astype(BF16)
    z = (jnp.dot(h2_hi, rw_ref[0], preferred_element_type=F32) + jnp.dot(h2_hi, rw_ref[1], preferred_element_type=F32)
         + jnp.dot(h2_lo, rw_ref[0], preferred_element_type=F32)) + rb_ref[...]
    lane = lax.broadcasted_iota(I32, z.shape, 1)
    neg_inf = -jnp.inf
    gl = jnp.where(lane < N_GROUPS, z, neg_inf)
    gmax = jnp.max(gl, axis=1, keepdims=True)
    gidx = jnp.min(jnp.where(gl == gmax, lane, LANES), axis=1, keepdims=True)
    gprob = 1.0 / jnp.sum(jnp.exp(gl - gmax), axis=1, keepdims=True)
    lo = EXPERT_LANE0 + EXPERTS_PER_GROUP * gidx
    el = jnp.where(lane >= lo, jnp.where(lane < lo + EXPERTS_PER_GROUP, z, neg_inf), neg_inf)
    m1 = jnp.max(el, axis=1, keepdims=True)
    j1 = jnp.min(jnp.where(el == m1, lane, LANES), axis=1, keepdims=True)
    el2 = jnp.where(lane == j1, neg_inf, el)
    m2 = jnp.max(el2, axis=1, keepdims=True)
    j2 = jnp.min(jnp.where(el2 == m2, lane, LANES), axis=1, keepdims=True)
    e21 = jnp.exp(m2 - m1)
    w1 = 1.0 / (1.0 + e21)
    gate_ref[...] = jnp.where(lane == j1, w1 * gprob, jnp.where(lane == j2, e21 * w1 * gprob, 0.0))


def out_proj_router(attn, rnn, x, g1, sc2, sh2, attn_g, rnn_g, w_out, norm2_g, router_w, router_b, tt):
    n, t, d = x.shape
    c = attn.shape[2]
    per_row = g1.shape[1] != 1
    mod_spec = (pl.BlockSpec((None, tt, d), lambda b, i: (b, i, 0)) if per_row
                else pl.BlockSpec((None, 1, d), lambda b, i: (b, 0, 0)))
    row = lambda w_: pl.BlockSpec((None, tt, w_), lambda b, i: (b, i, 0))
    const = lambda shape: pl.BlockSpec(shape, lambda b, i: (0,) * len(shape))
    return pl.pallas_call(
        _out_router_body,
        grid=(n, t // tt),
        in_specs=[row(c), row(c), row(d), mod_spec, mod_spec, mod_spec, const((1, c)), const((1, c)),
                  const(w_out.shape), const((1, d)), const(router_w.shape), const((1, LANES))],
        out_specs=[row(d), row(d), row(LANES)],
        out_shape=[jax.ShapeDtypeStruct((n, t, d), F32), jax.ShapeDtypeStruct((n, t, d), BF16),
                   jax.ShapeDtypeStruct((n, t, LANES), F32)],
        compiler_params=_cparams("parallel", "arbitrary"),
        name="out_proj_router",
    )(attn, rnn, x, g1, sc2, sh2, attn_g.reshape(1, c), rnn_g.reshape(1, c), w_out, norm2_g.reshape(1, d),
      router_w, router_b)


def split_bf16(w):
    hi = w.astype(BF16)
    return jnp.stack([hi, (w - hi.astype(F32)).astype(BF16)])


def _moe_body(h2_ref, gate_ref, x1_ref, g2_ref, wgu_ref, wd_ref, fg_ref, y_ref, acc_ref):
    e = pl.program_id(2)
    d_exp = wd_ref.shape[0]

    @pl.when(e == 0)
    def _():
        acc_ref[...] = jnp.zeros(acc_ref.shape, F32)

    gates = gate_ref[...]
    lane = lax.broadcasted_iota(I32, gates.shape, 1)
    gcol = jnp.sum(jnp.where(lane == e + EXPERT_LANE0, gates, 0.0), axis=1, keepdims=True)
    gu = jnp.dot(h2_ref[...], wgu_ref[...], preferred_element_type=F32)
    gpart, upart = gu[:, :d_exp], gu[:, d_exp:]
    act = (gpart * _sigmoid(gpart) * upart).astype(BF16)
    acc_ref[...] += gcol * jnp.dot(act, wd_ref[...], preferred_element_type=F32)

    @pl.when(e == pl.num_programs(2) - 1)
    def _():
        x2 = x1_ref[...] + g2_ref[...] * acc_ref[...]
        y_ref[...] = _rms(x2, fg_ref[...])


def moe_final(h2, gates, x1, g2, w_gu, w_down, final_g, tt):
    n, t, d = x1.shape
    n_exp = w_gu.shape[0]
    per_row = g2.shape[1] != 1
    mod_spec = (pl.BlockSpec((None, tt, d), lambda b, i, e: (b, i, 0)) if per_row
                else pl.BlockSpec((None, 1, d), lambda b, i, e: (b, 0, 0)))
    row = lambda w_: pl.BlockSpec((None, tt, w_), lambda b, i, e: (b, i, 0))
    return pl.pallas_call(
        _moe_body,
        grid=(n, t // tt, n_exp),
        in_specs=[row(d), row(LANES), row(d), mod_spec,
                  pl.BlockSpec((None,) + w_gu.shape[1:], lambda b, i, e: (e, 0, 0)),
                  pl.BlockSpec((None,) + w_down.shape[1:], lambda b, i, e: (e, 0, 0)),
                  pl.BlockSpec((1, d), lambda b, i, e: (0, 0))],
        out_specs=row(d),
        out_shape=jax.ShapeDtypeStruct((n, t, d), F32),
        scratch_shapes=[pltpu.VMEM((tt, d), F32)],
        compiler_params=_cparams("parallel", "arbitrary", "arbitrary"),
        name="moe_final",
    )(h2, gates, x1, g2, w_gu, w_down, final_g.reshape(1, d))


def _tile(t, pref):
    return pref if t % pref == 0 else t


def _layer(x, mods, pos, attend, conv_prev, h_prev, n_seq, lw, final_g):
    sh1, sc1, g1, sh2, sc2, g2 = mods
    n, t, d = x.shape
    t_seq = n * t // n_seq
    seqs = lambda a: a.reshape(n_seq, t_seq, a.shape[-1])
    q, k, v, kb, vt, iq, ikw, ikb, xr, yg = in_proj(x, sc1, sh1, lw["norm1_g"], lw["w_in"], rope_tables(pos),
                                                     _tile(t, 512))
    attn = attend(q, iq, ikw, k, v, kb, vt, ikb)
    rnn, h_new = rglru(seqs(xr), seqs(yg), conv_prev, h_prev, lw["conv_w"], lw["conv_b"], lw["rg_wa"], lw["rg_ba"],
                       lw["rg_wx"], lw["rg_bx"], lw["rg_lambda"], _tile(t_seq, 512))
    x1, h2, gates = out_proj_router(attn, rnn.reshape(xr.shape), x, g1, sc2, sh2, lw["attn_out_g"], lw["rnn_out_g"],
                                    lw["w_out"], lw["norm2_g"], lw["router_w"], lw["router_b"], _tile(t, 512))
    y = moe_final(h2, gates, x1, g2, lw["w_gu"], lw["w_down"], final_g, _tile(t, 1024))
    conv_new = seqs(xr)[:, t_seq - (CONV_WIDTH - 1):, :]
    return (seqs(y), seqs(k).reshape(n_seq, t_seq, N_KV_HEADS, HEAD_DIM),
            seqs(v).reshape(n_seq, t_seq, N_KV_HEADS, HEAD_DIM), seqs(ikw)[:, :, :IDX_HEAD_DIM], conv_new,
            h_new[:, 0, :])


def kernel(x_prompt, x_sample, cache_k, cache_v, cache_idx_k, state_conv, state_rglru, page_table, c_prompt,
           c_sample, ada_w, ada_b, norm1_g, w_in, conv_w, conv_b, rg_wa, rg_ba, rg_wx, rg_bx, rg_lambda, attn_out_g,
           rnn_out_g, w_out, norm2_g, router_grp_w, router_grp_b, router_exp_w, router_exp_b, exp_w_gate, exp_w_up,
           exp_w_down, final_g):
    depth = ada_w.shape[0]
    assert depth == 1, "this implementation handles the single-layer configuration"
    n_p, t_p, d = x_prompt.shape
    n_s, t_s, _ = x_sample.shape
    n_pages = page_table.shape[1]
    past = n_pages * PAGE_SIZE
    l = 0
    router_w = jnp.concatenate([router_grp_w[l], router_exp_w[l],
                                jnp.zeros((d, LANES - N_GROUPS - N_EXPERTS), F32)], axis=1)
    router_b = jnp.concatenate([router_grp_b[l], router_exp_b[l],
                                jnp.zeros((LANES - N_GROUPS - N_EXPERTS,), F32)]).reshape(1, LANES)
    lw = dict(norm1_g=norm1_g[l], w_in=prep_w_in(w_in[l]), conv_w=conv_w[l], conv_b=conv_b[l], rg_wa=rg_wa[l],
              rg_ba=rg_ba[l], rg_wx=rg_wx[l], rg_bx=rg_bx[l], rg_lambda=rg_lambda[l], attn_out_g=attn_out_g[l],
              rnn_out_g=rnn_out_g[l], w_out=w_out[l].astype(BF16), norm2_g=norm2_g[l],
              router_w=split_bf16(router_w), router_b=router_b,
              w_gu=jnp.concatenate([exp_w_gate[l], exp_w_up[l]], axis=2).astype(BF16),
              w_down=exp_w_down[l].astype(BF16))

    mod = ada_mod(jnp.concatenate([c_prompt, c_sample], axis=0), ada_w[l], ada_b[l])
    mods_p = [m[:n_p, None, :] for m in jnp.split(mod, 6, axis=1)]
    mods_s = [jnp.repeat(m[n_p:], t_s, axis=0)[None] for m in jnp.split(mod, 6, axis=1)]

    def attend_prompt(q, iq, ikw, k, v, kb, vt, ikb):
        iw_t = jnp.swapaxes(ikw[:, :, IDX_HEAD_DIM:IDX_HEAD_DIM + N_IDX_HEADS], 1, 2)
        return prompt_attention(q, iq, iw_t, kb, vt, ikb, _tile(t_p, 256))

    def attend_sample(q, iq, ikw, k, v, kb, vt, ikb):
        seqs = lambda a: a.reshape(n_s, t_s, a.shape[-1])
        n_pool = cache_k.shape[1]
        feat_major = lambda c: jnp.swapaxes(c.reshape(n_pool, PAGE_SIZE, -1), 1, 2)
        out = sample_attention(page_table, seqs(q), seqs(iq), seqs(ikw), seqs(k), seqs(v),
                               feat_major(cache_k[l]), feat_major(cache_v[l]), feat_major(cache_idx_k[l]))
        return out.reshape(1, n_s * t_s, N_HEADS * HEAD_DIM)

    c_rnn = conv_w.shape[2]
    ys, ks, vs, iks, cvs, hs = _layer(
        x_sample.reshape(1, n_s * t_s, d), mods_s, jnp.tile(past + jnp.arange(t_s), n_s), attend_sample,
        state_conv[l], state_rglru[l], n_s, lw, final_g)
    yp, kp, vp, ikp, cvp, hp = _layer(
        x_prompt, mods_p, jnp.arange(t_p), attend_prompt,
        jnp.zeros((n_p, CONV_WIDTH - 1, c_rnn), F32), jnp.zeros((n_p, c_rnn), F32), n_p, lw, final_g)

    st = lambda a: a[None]
    return (yp, ys, st(kp), st(vp), st(ikp), st(cvp), st(hp), st(ks), st(vs), st(iks), st(cvs), st(hs))
```

```python
import functools

import jax
import jax.numpy as jnp
from jax import lax
from jax.experimental import pallas as pl
from jax.experimental.pallas import tpu as pltpu

F32, BF16, I32, I16, U32 = jnp.float32, jnp.bfloat16, jnp.int32, jnp.int16, jnp.uint32
HIGHEST = lax.Precision.HIGHEST

HEAD_DIM = 64
N_HEADS = 8
N_KV_HEADS = 2
HEADS_PER_KV = N_HEADS // N_KV_HEADS
N_IDX_HEADS = 8
IDX_HEAD_DIM = 32
TOPK_MAX = 256
ROPE_THETA = 10000.0
CONV_WIDTH = 4
RG_BLOCKS = 8
RG_C = 8.0
N_GROUPS = 4
EXPERTS_PER_GROUP = 8
N_EXPERTS = N_GROUPS * EXPERTS_PER_GROUP
PAGE_SIZE = 128
NORM_EPS = 1e-6

LANES = 128
SUBLANES = 8
VMEM_LIMIT = 48 * 1024 * 1024
INT_MIN = -(2 ** 31)
I16_MIN = -(2 ** 15)
NEG_BIG = -1e30
EXPERT_LANE0 = N_GROUPS
BF16_SUBLANES = 16
V_AUG_ROWS = HEAD_DIM + BF16_SUBLANES
ATT_CHUNK_HEADS = 2
COUNT_TILES = 2
MOE_ROW_BLOCK = 256
MOE_EXPERTS_PER_STEP = 2
LOG2_E = 1.4426950408889634
Q_SCALE = HEAD_DIM ** -0.5 * LOG2_E


def _cparams(*sem):
    return pltpu.CompilerParams(dimension_semantics=sem, vmem_limit_bytes=VMEM_LIMIT)


def _rms(x, g):
    return x * lax.rsqrt(jnp.mean(x * x, axis=-1, keepdims=True) + NORM_EPS) * g


def _sigmoid(x):
    return 1.0 / (1.0 + jnp.exp(-x))


def _float_key(s):
    b = pltpu.bitcast(s + 0.0, I32)
    return b ^ ((b >> 31) & 0x7FFFFFFF)


def _ada_body(c_ref, w_ref, b_ref, o_ref):
    c = c_ref[...]
    s = c * _sigmoid(c)
    o_ref[...] = jnp.dot(s, w_ref[...], preferred_element_type=F32, precision=HIGHEST) + b_ref[...]


def ada_mod(c, w, b):
    m, d = c.shape
    n_chunks = w.shape[1] // d
    return pl.pallas_call(
        _ada_body,
        grid=(n_chunks,),
        in_specs=[pl.BlockSpec((m, d), lambda j: (0, 0)),
                  pl.BlockSpec((d, d), lambda j: (0, j)),
                  pl.BlockSpec((1, d), lambda j: (0, j))],
        out_specs=pl.BlockSpec((m, d), lambda j: (0, j)),
        out_shape=jax.ShapeDtypeStruct((m, w.shape[1]), F32),
        compiler_params=_cparams("arbitrary"),
        name="ada_mod",
    )(c, w, b.reshape(1, -1))


_C_Q, _C_QR, _C_K, _C_KR, _C_V, _C_IQ, _C_IQR, _C_IKW, _C_IKR, _C_XR, _C_YG, _C_END = (
    0, 512, 1024, 1152, 1280, 1408, 1664, 1920, 2048, 2176, 2688, 3200)


def _rot_half_cols(w, head_dim):
    d_in, n = w.shape
    wh = w.reshape(d_in, n // head_dim, 2, head_dim // 2)
    return jnp.stack([-wh[:, :, 1], wh[:, :, 0]], axis=2).reshape(d_in, n)


def prep_w_in(w_in):
    d_in = w_in.shape[0]
    q, k, v, iq, ik, iw, xr, yg = jnp.split(w_in, [512, 640, 768, 1024, 1056, 1064, 1576], axis=1)
    z = lambda n: jnp.zeros((d_in, n), w_in.dtype)
    ikw = jnp.concatenate([ik, iw, z(LANES - IDX_HEAD_DIM - N_IDX_HEADS)], axis=1)
    ikr = jnp.concatenate([_rot_half_cols(ik, IDX_HEAD_DIM), z(LANES - IDX_HEAD_DIM)], axis=1)
    w = jnp.concatenate([q, _rot_half_cols(q, HEAD_DIM), k, _rot_half_cols(k, HEAD_DIM), v,
                         iq, _rot_half_cols(iq, IDX_HEAD_DIM), ikw, ikr, xr, yg], axis=1)
    return w.astype(BF16)


def rope_tables(pos):
    pos = pos.astype(F32)[:, None]

    def cs(half, reps):
        inv = ROPE_THETA ** (-jnp.arange(half, dtype=F32) / half)
        ang = pos * inv[None, :]
        return jnp.tile(jnp.cos(ang), (1, 2 * reps)), jnp.tile(jnp.sin(ang), (1, 2 * reps))

    cq, sq = cs(HEAD_DIM // 2, LANES // HEAD_DIM)
    ci, si = cs(IDX_HEAD_DIM // 2, LANES // IDX_HEAD_DIM)
    lane = jnp.arange(LANES)[None, :]
    w_scale = (IDX_HEAD_DIM ** -0.5) * (N_IDX_HEADS ** -0.5)
    cx = jnp.where(lane < IDX_HEAD_DIM, ci, jnp.where(lane < IDX_HEAD_DIM + N_IDX_HEADS, w_scale, 0.0))
    sx = jnp.where(lane < IDX_HEAD_DIM, si, 0.0)
    return cq, sq, ci, si, cx.astype(F32), sx.astype(F32)


def _in_proj_body(x_ref, sc_ref, sh_ref, g_ref, w_ref, cq_ref, sq_ref, ci_ref, si_ref, cx_ref, sx_ref,
                  q_ref, k_ref, v_ref, kb_ref, vt_ref, iq_ref, ikw_ref, ikb_ref, xr_ref, yg_ref):
    x = x_ref[...]
    h = (_rms(x, g_ref[...]) * (1.0 + sc_ref[...]) + sh_ref[...]).astype(BF16)

    def seg(a, b):
        return jnp.dot(h, w_ref[:, a:b], preferred_element_type=F32)

    cq, sq = cq_ref[...], sq_ref[...]
    cq4 = jnp.concatenate([cq] * 4, axis=1)
    sq4 = jnp.concatenate([sq] * 4, axis=1)
    q = (seg(_C_Q, _C_QR) * cq4 + seg(_C_QR, _C_K) * sq4) * Q_SCALE
    q_ref[...] = q.astype(BF16)
    k = seg(_C_K, _C_KR) * cq + seg(_C_KR, _C_V) * sq
    k_ref[...] = k
    kb_ref[...] = k.astype(BF16)
    v = seg(_C_V, _C_IQ)
    v_ref[...] = v
    vt_ref[...] = v.T.astype(BF16)
    ci2 = jnp.concatenate([ci_ref[...]] * 2, axis=1)
    si2 = jnp.concatenate([si_ref[...]] * 2, axis=1)
    iq_ref[...] = (seg(_C_IQ, _C_IQR) * ci2 + seg(_C_IQR, _C_IKW) * si2).astype(BF16)
    ikw = seg(_C_IKW, _C_IKR) * cx_ref[...] + seg(_C_IKR, _C_XR) * sx_ref[...]
    ikw_ref[...] = ikw
    ikb_ref[...] = ikw.astype(BF16)
    xr_ref[...] = seg(_C_XR, _C_YG)
    yg_ref[...] = seg(_C_YG, _C_END)


def in_proj(x, sc, sh, g, w, tables, tt):
    n, t, d = x.shape
    per_row = sc.shape[1] != 1
    mod_spec = (pl.BlockSpec((None, tt, d), lambda b, i: (b, i, 0)) if per_row
                else pl.BlockSpec((None, 1, d), lambda b, i: (b, 0, 0)))
    row = lambda w_: pl.BlockSpec((None, tt, w_), lambda b, i: (b, i, 0))
    tab = pl.BlockSpec((tt, LANES), lambda b, i: (i, 0))
    outs = [(512, BF16), (128, F32), (128, F32), (128, BF16), None, (256, BF16), (128, F32), (128, BF16),
            (512, F32), (512, F32)]
    out_shape, out_specs = [], []
    for o in outs:
        if o is None:
            out_shape.append(jax.ShapeDtypeStruct((n, LANES, t), BF16))
            out_specs.append(pl.BlockSpec((None, LANES, tt), lambda b, i: (b, 0, i)))
        else:
            out_shape.append(jax.ShapeDtypeStruct((n, t, o[0]), o[1]))
            out_specs.append(row(o[0]))
    return pl.pallas_call(
        _in_proj_body,
        grid=(n, t // tt),
        in_specs=[row(d), mod_spec, mod_spec,
                  pl.BlockSpec((1, d), lambda b, i: (0, 0)),
                  pl.BlockSpec(w.shape, lambda b, i: (0, 0))] + [tab] * 6,
        out_specs=out_specs,
        out_shape=out_shape,
        compiler_params=_cparams("parallel", "arbitrary"),
        name="in_proj",
    )(x, sc, sh, g.reshape(1, d), w, *tables)


def _pattn_body(q_ref, iq_ref, iwt_ref, k_ref, vt_ref, ik_ref, o_ref,
                hi_ref, lo_ref, lb_ref, p_ref, m_ref, acc_ref, mm_ref, *, n_sel, idx_bits):
    qi = pl.program_id(1)
    tq = q_ref.shape[0]
    tk = tq
    n_tiles = qi + 1

    iq_t = iq_ref[...].astype(F32).T
    iq_st = jnp.concatenate([iq_t[IDX_HEAD_DIM * h:IDX_HEAD_DIM * (h + 1)] for h in range(N_IDX_HEADS)],
                            axis=1).astype(BF16)
    w_t = iwt_ref[...]
    q_t = q_ref[...].astype(F32).T
    zero_half = jnp.zeros((HEAD_DIM, tq), F32)
    q_pad = []
    for g in range(N_KV_HEADS):
        cols = []
        for hh in range(HEADS_PER_KV):
            h = g * HEADS_PER_KV + hh
            qh = q_t[HEAD_DIM * h:HEAD_DIM * (h + 1)]
            cols.append(jnp.concatenate([qh, zero_half] if g == 0 else [zero_half, qh], axis=0))
        q_pad.append(jnp.concatenate(cols, axis=1).astype(BF16))

    row_s = lax.broadcasted_iota(I32, (tk, tq), 0)
    col_t = lax.broadcasted_iota(I32, (tk, tq), 1)
    causal = row_s <= col_t

    def tile_off(j):
        return pl.multiple_of(j * tk, tk)

    def walk_tiles(step):
        def pair(p, c):
            step(2 * p, 0, False)
            step(2 * p + 1, 1, False)
            return c

        lax.fori_loop(0, qi // 2, pair, 0)

        @pl.when(qi % 2 == 0)
        def _():
            step(qi, 0, True)

        @pl.when(qi % 2 == 1)
        def _():
            step(qi - 1, 0, False)
            step(qi, 1, True)

    def index_dots(j, slot):
        ik = ik_ref[pl.ds(tile_off(j), tk), :][:, :IDX_HEAD_DIM]
        mm_ref[slot] = jnp.dot(ik, iq_st, preferred_element_type=F32)

    def p1(j, slot, last):
        if not last:
            index_dots(j + 1, 1 - slot)
        sc = w_t[0:1, :] * jnp.maximum(mm_ref[slot, :, 0:tq], 0.0)
        for h in range(1, N_IDX_HEADS):
            sc = sc + w_t[h:h + 1, :] * jnp.maximum(mm_ref[slot, :, h * tq:(h + 1) * tq], 0.0)
        key = _float_key(sc)
        if last:
            key = jnp.where(causal, key, INT_MIN)
        rows = pl.ds(tile_off(j), tk)
        hi_ref[rows, :] = (key >> 16).astype(I16)
        lo_ref[rows, :] = ((key & 0xFFFF) + I16_MIN).astype(I16)

    index_dots(0, 0)
    walk_tiles(p1)

    span = COUNT_TILES * tk
    n_spans = (n_tiles + COUNT_TILES - 1) // COUNT_TILES
    groups = span // BF16_SUBLANES
    pad_rows = pl.ds(tile_off(n_tiles), tk)
    hi_ref[pad_rows, :] = jnp.full((tk, tq), I16_MIN, I16)
    lo_ref[pad_rows, :] = jnp.full((tk, tq), I16_MIN, I16)

    def rows16(v):
        return jnp.broadcast_to(v.astype(I16), (BF16_SUBLANES, tq))[None]

    def span_rows(j):
        return pl.ds(pl.multiple_of(j * span, span), span)

    def tile16(ref, j):
        return ref[span_rows(j), :].reshape(groups, BF16_SUBLANES, tq)

    one16, zero16 = jnp.int16(1), jnp.int16(0)

    def count(pred):
        def body(j, acc):
            marks = pred(j)
            parts = [marks[g] for g in range(groups)]
            while len(parts) > 1:
                parts = [parts[i] + parts[i + 1] for i in range(0, len(parts), 2)]
            return acc + parts[0]

        acc = lax.fori_loop(0, n_spans, body, jnp.zeros((BF16_SUBLANES, tq), I16))
        return jnp.sum(acc.astype(I32).astype(F32), axis=0, keepdims=True)

    def bisect16(ref, need):
        def step(b, thr):
            cand = thr + jnp.left_shift(jnp.int32(1), 15 - b)
            cand_b = rows16(cand)
            c = count(lambda j: jnp.where(tile16(ref, j) >= cand_b, one16, zero16))
            return jnp.where(c >= need, cand, thr)
        return lax.fori_loop(0, 16, step, jnp.full((1, tq), I16_MIN, I32))

    thr_hi = bisect16(hi_ref, n_sel)
    hi_b = rows16(thr_hi)
    c_above = count(lambda j: jnp.where(tile16(hi_ref, j) > hi_b, one16, zero16))

    def p2(j, c):
        rows = span_rows(j)
        lb_ref[rows, :] = jnp.where(hi_ref[rows, :] == thr_hi.astype(I16), lo_ref[rows, :], jnp.int16(I16_MIN))
        return c

    lax.fori_loop(0, n_spans, p2, 0)
    thr_lo = bisect16(lb_ref, n_sel - c_above)
    lo_b = rows16(thr_lo)
    c_gt = c_above + count(lambda j: jnp.where(tile16(lb_ref, j) > lo_b, one16, zero16))
    ties_left = n_sel - c_gt

    def is_tie(j):
        return jnp.where(tile16(hi_ref, j) == hi_b, jnp.where(tile16(lo_ref, j) == lo_b, one16, zero16), zero16)

    c_tie = count(is_tie)
    p_ref[...] = jnp.full((1, tq), 2 ** idx_bits - 1, I32)
    has_thr = jnp.where(thr_hi > I16_MIN, 1, jnp.where(thr_lo > I16_MIN, 1, 0))
    excess = jnp.max(jnp.where(has_thr > 0, c_tie - ties_left, 0.0)) > 0.0

    idx16 = row_s.astype(I16)

    @pl.when(excess)
    def _():
        sub = lax.broadcasted_iota(I32, (span, tq), 0).astype(I16).reshape(groups, BF16_SUBLANES, tq)

        def tb(b, p):
            cand = p | jnp.left_shift(jnp.int32(1), idx_bits - 1 - b)
            c = count(lambda j: jnp.where(sub < rows16(cand - j * span), is_tie(j), zero16))
            return jnp.where(c < ties_left, cand, p)

        p_ref[...] = lax.fori_loop(0, idx_bits, tb, jnp.zeros((1, tq), I32))

    p_last = p_ref[...]
    thr_hi16, thr_lo16 = thr_hi.astype(I16), thr_lo.astype(I16)

    m_ref[...] = jnp.full(m_ref.shape, NEG_BIG, F32)
    acc_ref[...] = jnp.zeros(acc_ref.shape, F32)
    ones_rows = jnp.ones((V_AUG_ROWS - HEAD_DIM, tk), BF16)

    w_c = ATT_CHUNK_HEADS * tq
    chunks = [(g, c * w_c) for g in range(N_KV_HEADS) for c in range(HEADS_PER_KV // ATT_CHUNK_HEADS)]

    def score_chunk(kt, slot, ci):
        g, c0 = chunks[ci]
        mm_ref[slot, :, ci * w_c:(ci + 1) * w_c] = jnp.dot(kt, q_pad[g][:, c0:c0 + w_c],
                                                             preferred_element_type=F32)

    kt0 = k_ref[pl.ds(0, tk), :]
    for ci in range(len(chunks)):
        score_chunk(kt0, 0, ci)

    def attend(j, slot, diag):
        off = tile_off(j)
        hi, lo = hi_ref[pl.ds(off, tk), :], lo_ref[pl.ds(off, tk), :]
        keep_tie = jnp.where(idx16 <= (p_last - j * tk).astype(I16), one16, zero16)
        sel = jnp.where(hi > thr_hi16, one16,
                        jnp.where(hi == thr_hi16,
                                  jnp.where(lo > thr_lo16, one16, jnp.where(lo == thr_lo16, keep_tie, zero16)),
                                  zero16))
        bias = jnp.where(sel.astype(I32) > 0, 0.0, NEG_BIG)
        if diag:
            bias = jnp.where(causal, bias, NEG_BIG)
        vt = vt_ref[:, pl.ds(off, tk)]
        v_aug = [jnp.concatenate([vt[HEAD_DIM * g:HEAD_DIM * (g + 1)], ones_rows], axis=0)
                 for g in range(N_KV_HEADS)]
        bias_c = jnp.concatenate([bias] * ATT_CHUNK_HEADS, axis=1)
        if not diag:
            kt_next = k_ref[pl.ds(tile_off(j + 1), tk), :]
        for ci, (g, c0) in enumerate(chunks):
            if not diag:
                score_chunk(kt_next, 1 - slot, ci)
            cols = slice(c0, c0 + w_c)
            s = mm_ref[slot, :, ci * w_c:(ci + 1) * w_c] + bias_c
            m_old = m_ref[g:g + 1, cols]
            m_new = jnp.maximum(m_old, jnp.max(s, axis=0, keepdims=True))
            alpha = jnp.exp2(m_old - m_new)
            p = jnp.exp2(s - m_new).astype(BF16)
            acc_ref[g, :, cols] = alpha * acc_ref[g, :, cols] + jnp.dot(v_aug[g], p, preferred_element_type=F32)
            m_ref[g:g + 1, cols] = m_new

    walk_tiles(attend)

    rows = []
    for g in range(N_KV_HEADS):
        acc = acc_ref[g]
        o_g = acc[:HEAD_DIM] / acc[HEAD_DIM:HEAD_DIM + 1]
        rows += [o_g[:, hh * tq:(hh + 1) * tq] for hh in range(HEADS_PER_KV)]
    o_ref[...] = jnp.concatenate(rows, axis=0).T


def prompt_attention(q, iq, iw_t, kb, vt, ikb, tq):
    n, t, _ = q.shape
    n_sel = min(TOPK_MAX, t // 4)
    idx_bits = max(1, (t - 1).bit_length())
    body = functools.partial(_pattn_body, n_sel=n_sel, idx_bits=idx_bits)
    return pl.pallas_call(
        body,
        grid=(n, t // tq),
        in_specs=[pl.BlockSpec((None, tq, q.shape[2]), lambda b, i: (b, i, 0)),
                  pl.BlockSpec((None, tq, iq.shape[2]), lambda b, i: (b, i, 0)),
                  pl.BlockSpec((None, N_IDX_HEADS, tq), lambda b, i: (b, 0, i)),
                  pl.BlockSpec((None, t, LANES), lambda b, i: (b, 0, 0)),
                  pl.BlockSpec((None, LANES, t), lambda b, i: (b, 0, 0)),
                  pl.BlockSpec((None, t, LANES), lambda b, i: (b, 0, 0))],
        out_specs=pl.BlockSpec((None, tq, N_HEADS * HEAD_DIM), lambda b, i: (b, i, 0)),
        out_shape=jax.ShapeDtypeStruct((n, t, N_HEADS * HEAD_DIM), F32),
        scratch_shapes=[pltpu.VMEM((t + tq, tq), I16),
                        pltpu.VMEM((t + tq, tq), I16),
                        pltpu.VMEM((t + tq, tq), I16),
                        pltpu.VMEM((1, tq), I32),
                        pltpu.VMEM((N_KV_HEADS, HEADS_PER_KV * tq), F32),
                        pltpu.VMEM((N_KV_HEADS, V_AUG_ROWS, HEADS_PER_KV * tq), F32),
                        pltpu.VMEM((2, tq, N_HEADS * tq), F32)],
        compiler_params=_cparams("parallel", "arbitrary"),
        name="prompt_attention",
    )(q, iq, iw_t, kb, vt, ikb)


Q_PAD = SUBLANES


def _sattn_body(pt_ref, q_ref, iq_ref, iw_ref, kn_ref, vn_ref, ikn_ref, ck_hbm, cv_hbm, cik_hbm, o_ref,
                kbuf, vbuf, ikbuf, s_ref, p_ref, sem, *, n_pages, n_sel, idx_bits):
    b = pl.program_id(0)
    slot = b % 2
    past = n_pages * PAGE_SIZE
    lk = past + LANES

    streams = ((cik_hbm, ikbuf), (ck_hbm, kbuf), (cv_hbm, vbuf))

    def page_copy(kind, p, seq, slt):
        src, dst = streams[kind]
        return pltpu.make_async_copy(src.at[pt_ref[seq, p]], dst.at[slt, :, pl.ds(p * PAGE_SIZE, PAGE_SIZE)],
                                     sem.at[slt, kind])

    def start_pages(seq, slt):
        for kind in range(len(streams)):
            for p in range(n_pages):
                page_copy(kind, p, seq, slt).start()

    def wait_pages(kind):
        for p in range(n_pages):
            page_copy(kind, p, b, slot).wait()

    @pl.when(b == 0)
    def _():
        start_pages(0, 0)

    @pl.when(b + 1 < pl.num_programs(0))
    def _():
        start_pages(b + 1, 1 - slot)

    def new_cols(ref):
        rows = jnp.concatenate([ref[...], jnp.zeros((LANES - Q_PAD, LANES), F32)], axis=0)
        return rows.T

    new_tile = pl.ds(past, LANES)
    kbuf[slot, :, new_tile] = new_cols(kn_ref)
    vbuf[slot, :, new_tile] = new_cols(vn_ref)
    ikbuf[slot, :, new_tile] = new_cols(ikn_ref)[:IDX_HEAD_DIM]

    iq = iq_ref[...].astype(F32)
    iq_st = jnp.concatenate([iq[:, IDX_HEAD_DIM * h:IDX_HEAD_DIM * (h + 1)] for h in range(N_IDX_HEADS)],
                            axis=0).astype(BF16)
    qf = q_ref[...].astype(F32)
    lane128 = lax.broadcasted_iota(I32, (Q_PAD, LANES), 1)
    q_rows = []
    for h in range(N_HEADS):
        g = h // HEADS_PER_KV
        slab = qf[:, LANES * (h // 2):LANES * (h // 2 + 1)]
        if (h % 2) != g:
            slab = pltpu.roll(slab, HEAD_DIM, axis=1)
        keep = (lane128 < HEAD_DIM) if g == 0 else (lane128 >= HEAD_DIM)
        q_rows.append(jnp.where(keep, slab, 0.0))
    q_pad = jnp.concatenate(q_rows, axis=0).astype(BF16)
    iw = iw_ref[...]

    key_pos = lax.broadcasted_iota(I32, (Q_PAD, lk), 1)
    q_row = lax.broadcasted_iota(I32, (Q_PAD, lk), 0)
    admissible = key_pos <= past + q_row

    wait_pages(0)
    d = jnp.dot(iq_st, ikbuf[slot].astype(BF16), preferred_element_type=F32)
    sc = None
    for h in range(N_IDX_HEADS):
        w_h = iw[:, IDX_HEAD_DIM + h:IDX_HEAD_DIM + h + 1]
        term = w_h * jnp.maximum(d[Q_PAD * h:Q_PAD * (h + 1)], 0.0)
        sc = term if sc is None else sc + term
    s_ref[...] = jnp.where(admissible, _float_key(sc), INT_MIN)

    lane_pos = lax.broadcasted_iota(I32, (Q_PAD, LANES), 1)

    def count(pred):
        terms = [pred(s_ref[:, c * LANES:(c + 1) * LANES], c * LANES) for c in range(lk // LANES)]
        while len(terms) > 1:
            terms = [terms[i] + terms[i + 1] if i + 1 < len(terms) else terms[i] for i in range(0, len(terms), 2)]
        return jnp.sum(terms[0], axis=1, keepdims=True)

    def bisect(i, thr):
        cand = jnp.broadcast_to(thr + jnp.left_shift(jnp.int32(1), 31 - i), (Q_PAD, LANES))
        c = count(lambda t, c0: jnp.where(t >= cand, 1.0, 0.0))
        return jnp.where(c >= n_sel, cand[:, :1], thr)

    thr = lax.fori_loop(0, 32, bisect, jnp.full((Q_PAD, 1), INT_MIN, I32))
    thr_b = jnp.broadcast_to(thr, (Q_PAD, LANES))
    ties_left = n_sel - count(lambda t, c0: jnp.where(t > thr_b, 1.0, 0.0))
    c_tie = count(lambda t, c0: jnp.where(t == thr_b, 1.0, 0.0))
    p_ref[...] = jnp.full((Q_PAD, 1), 2 ** idx_bits - 1, I32)

    @pl.when(jnp.max(jnp.where(thr > INT_MIN, c_tie - ties_left, 0.0)) > 0.0)
    def _():
        def tb(i, pidx):
            cand = jnp.broadcast_to(pidx | jnp.left_shift(jnp.int32(1), idx_bits - 1 - i), (Q_PAD, LANES))
            c = count(lambda t, c0: jnp.where(t == thr_b, jnp.where(lane_pos + c0 < cand, 1.0, 0.0), 0.0))
            return jnp.where(c < ties_left, cand[:, :1], pidx)

        p_ref[...] = lax.fori_loop(0, idx_bits, tb, jnp.zeros((Q_PAD, 1), I32))

    p_last = p_ref[...]
    t = s_ref[...]
    keep_tie = jnp.where(key_pos <= p_last, 0.0, NEG_BIG)
    bias = jnp.where(t > thr, 0.0, jnp.where(t == thr, keep_tie, NEG_BIG))
    bias = jnp.where(admissible, bias, NEG_BIG)

    wait_pages(1)
    s = jnp.dot(q_pad, kbuf[slot].astype(BF16), preferred_element_type=F32)
    s = s + jnp.concatenate([bias] * N_HEADS, axis=0)
    m = jnp.max(s, axis=1, keepdims=True)
    pexp = jnp.exp2(s - m)
    l = jnp.sum(pexp, axis=1, keepdims=True)
    wait_pages(2)
    o = lax.dot_general(pexp.astype(BF16), vbuf[slot].astype(BF16), (((1,), (1,)), ((), ())),
                        preferred_element_type=F32)
    o_ref[...] = o / l


def sample_attention(page_table, q, iq, ikw, k_new, v_new, cache_k, cache_v, cache_ik):
    n, n_new, _ = q.shape
    assert n_new <= Q_PAD
    n_pages = page_table.shape[1]
    past = n_pages * PAGE_SIZE
    lk = past + LANES
    n_sel = min(TOPK_MAX, (past + n_new) // 4)
    idx_bits = max(1, (lk - 1).bit_length())
    body = functools.partial(_sattn_body, n_pages=n_pages, n_sel=n_sel, idx_bits=idx_bits)
    pad = lambda a: jnp.pad(a, ((0, 0), (0, Q_PAD - n_new), (0, 0)))
    q, iq, ikw, k_new, v_new = pad(q), pad(iq), pad(ikw), pad(k_new), pad(v_new)
    seq = lambda w_: pl.BlockSpec((None, Q_PAD, w_), lambda b, pt: (b, 0, 0))
    hbm = pl.BlockSpec(memory_space=pl.ANY)
    grid_spec = pltpu.PrefetchScalarGridSpec(
        num_scalar_prefetch=1,
        grid=(n,),
        in_specs=[seq(q.shape[2]), seq(iq.shape[2]), seq(LANES), seq(LANES), seq(LANES), seq(LANES), hbm, hbm, hbm],
        out_specs=pl.BlockSpec((None, N_HEADS * Q_PAD, LANES), lambda b, pt: (b, 0, 0)),
        scratch_shapes=[pltpu.VMEM((2, LANES, lk), F32),
                        pltpu.VMEM((2, LANES, lk), F32),
                        pltpu.VMEM((2, IDX_HEAD_DIM, lk), F32),
                        pltpu.VMEM((Q_PAD, lk), I32),
                        pltpu.VMEM((Q_PAD, 1), I32),
                        pltpu.SemaphoreType.DMA((2, 3))],
    )
    raw = pl.pallas_call(
        body,
        grid_spec=grid_spec,
        out_shape=jax.ShapeDtypeStruct((n, N_HEADS * Q_PAD, LANES), F32),
        compiler_params=_cparams("arbitrary"),
        name="sample_attention",
    )(page_table, q, iq, ikw, k_new, v_new, ikw, cache_k, cache_v, cache_ik)
    raw = raw.reshape(n, N_HEADS, Q_PAD, N_KV_HEADS, HEAD_DIM)[:, :, :n_new]
    per_head = jnp.stack([raw[:, h, :, h // HEADS_PER_KV] for h in range(N_HEADS)], axis=2)
    return per_head.reshape(n, n_new, N_HEADS * HEAD_DIM)


def _rglru_body(xr_ref, yg_ref, cprev_ref, hprev_ref, cw_ref, cb_ref, wa_ref, ba_ref, wx_ref, bx_ref, lam_ref,
                y_ref, hlast_ref, xbuf, a_buf, b_buf, h_buf, h_carry):
    i = pl.program_id(1)
    tt = xr_ref.shape[0]
    halo = SUBLANES

    @pl.when(i == 0)
    def _():
        xbuf[pl.ds(0, halo), :] = jnp.zeros((halo, xbuf.shape[1]), F32)
        xbuf[pl.ds(halo - (CONV_WIDTH - 1), CONV_WIDTH - 1), :] = cprev_ref[...]
        h_carry[...] = hprev_ref[...]

    x = xr_ref[...]
    xbuf[pl.ds(halo, tt), :] = x
    xc = cb_ref[...] + cw_ref[CONV_WIDTH - 1:CONV_WIDTH, :] * x
    for j in range(CONV_WIDTH - 1):
        xc = xc + cw_ref[j:j + 1, :] * xbuf[pl.ds(halo - (CONV_WIDTH - 1) + j, tt), :]
    xcb = xc.astype(BF16)
    r = _sigmoid(jnp.dot(xcb, wa_ref[...], preferred_element_type=F32) + ba_ref[...])
    ig = _sigmoid(jnp.dot(xcb, wx_ref[...], preferred_element_type=F32) + bx_ref[...])
    z = -lam_ref[...]
    softplus = jnp.maximum(z, 0.0) + jnp.log1p(jnp.exp(-jnp.abs(z)))
    log_a = -RG_C * r * softplus
    a = jnp.exp(log_a)
    a_buf[...] = a
    b_buf[...] = jnp.sqrt(jnp.tanh(-log_a) * (1.0 + a * a)) * (ig * xc)

    def step(t, h):
        h = a_buf[pl.ds(t, 1), :] * h + b_buf[pl.ds(t, 1), :]
        h_buf[pl.ds(t, 1), :] = h
        return h

    h_fin = lax.fori_loop(0, tt, step, h_carry[...], unroll=min(8, tt))
    h_carry[...] = h_fin
    hlast_ref[...] = h_fin
    y_ref[...] = h_buf[...] * jax.nn.gelu(yg_ref[...])
    xbuf[pl.ds(halo - (CONV_WIDTH - 1), CONV_WIDTH - 1), :] = xbuf[pl.ds(halo + tt - (CONV_WIDTH - 1), CONV_WIDTH - 1), :]


def _block_diag(w):
    k, c, d = w.shape
    eye = jnp.eye(k, dtype=w.dtype)
    return (eye[:, None, :, None] * w[:, :, None, :]).reshape(k * c, k * d)


def rglru(xr, yg, conv_prev, h_prev, conv_w, conv_b, rg_wa, rg_ba, rg_wx, rg_bx, rg_lambda, tt):
    n, t, c = xr.shape
    row = pl.BlockSpec((None, tt, c), lambda b, i: (b, i, 0))
    const = lambda shape: pl.BlockSpec(shape, lambda b, i: (0,) * len(shape))
    vec = lambda a: a.reshape(1, c)
    return pl.pallas_call(
        _rglru_body,
        grid=(n, t // tt),
        in_specs=[row, row,
                  pl.BlockSpec((None, CONV_WIDTH - 1, c), lambda b, i: (b, 0, 0)),
                  pl.BlockSpec((None, 1, c), lambda b, i: (b, 0, 0)),
                  const((CONV_WIDTH, c)), const((1, c)), const((c, c)), const((1, c)), const((c, c)),
                  const((1, c)), const((1, c))],
        out_specs=[row, pl.BlockSpec((None, 1, c), lambda b, i: (b, 0, 0))],
        out_shape=[jax.ShapeDtypeStruct((n, t, c), F32), jax.ShapeDtypeStruct((n, 1, c), F32)],
        scratch_shapes=[pltpu.VMEM((tt + SUBLANES, c), F32), pltpu.VMEM((tt, c), F32), pltpu.VMEM((tt, c), F32),
                        pltpu.VMEM((tt, c), F32), pltpu.VMEM((1, c), F32)],
        compiler_params=_cparams("parallel", "arbitrary"),
        name="rglru",
    )(xr, yg, conv_prev, h_prev.reshape(n, 1, c), conv_w, vec(conv_b), _block_diag(rg_wa).astype(BF16), vec(rg_ba),
      _block_diag(rg_wx).astype(BF16), vec(rg_bx), vec(rg_lambda))


def _pack_bf16_pairs(x):
    c = x.shape[1] // 2
    a = pltpu.bitcast(x[:, :c].astype(BF16).astype(F32), U32)
    b = pltpu.bitcast(x[:, c:].astype(BF16).astype(F32), U32)
    return (a & jnp.uint32(0xFFFF0000)) | (b >> 16)


def _unpack_bf16_pairs(u):
    a = pltpu.bitcast(u & jnp.uint32(0xFFFF0000), F32)
    b = pltpu.bitcast(u << 16, F32)
    return jnp.concatenate([a, b], axis=1).astype(BF16)


def _out_router_body(attn_ref, rnn_ref, x_ref, g1_ref, sc2_ref, sh2_ref, ag_ref, rg_ref, wo_ref, n2_ref,
                     rw_ref, rb_ref, tri_ref, x1_ref, h2_ref, gate_ref, code_ref, count_ref):
    mixed = jnp.concatenate([_rms(attn_ref[...], ag_ref[...]), _rms(rnn_ref[...], rg_ref[...])], axis=1)
    x1 = x_ref[...] + g1_ref[...] * jnp.dot(mixed.astype(BF16), wo_ref[...], preferred_element_type=F32)
    x1_ref[...] = x1
    h2 = _rms(x1, n2_ref[...]) * (1.0 + sc2_ref[...]) + sh2_ref[...]
    h2_hi = h2.astype(BF16)
    h2_ref[...] = _pack_bf16_pairs(h2)
    h2_lo = (h2 - h2_hi.astype(F32)).astype(BF16)
    z = (jnp.dot(h2_hi, rw_ref[0], preferred_element_type=F32) + jnp.dot(h2_hi, rw_ref[1], preferred_element_type=F32)
         + jnp.dot(h2_lo, rw_ref[0], preferred_element_type=F32)) + rb_ref[...]
    lane = lax.broadcasted_iota(I32, z.shape, 1)
    neg_inf = -jnp.inf
    gl = jnp.where(lane < N_GROUPS, z, neg_inf)
    gmax = jnp.max(gl, axis=1, keepdims=True)
    gidx = jnp.min(jnp.where(gl == gmax, lane, LANES), axis=1, keepdims=True)
    gprob = 1.0 / jnp.sum(jnp.exp(gl - gmax), axis=1, keepdims=True)
    lo = EXPERT_LANE0 + EXPERTS_PER_GROUP * gidx
    el = jnp.where(lane >= lo, jnp.where(lane < lo + EXPERTS_PER_GROUP, z, neg_inf), neg_inf)
    m1 = jnp.max(el, axis=1, keepdims=True)
    j1 = jnp.min(jnp.where(el == m1, lane, LANES), axis=1, keepdims=True)
    el2 = jnp.where(lane == j1, neg_inf, el)
    m2 = jnp.max(el2, axis=1, keepdims=True)
    j2 = jnp.min(jnp.where(el2 == m2, lane, LANES), axis=1, keepdims=True)
    e21 = jnp.exp(m2 - m1)
    w1 = 1.0 / (1.0 + e21)
    gate_ref[...] = jnp.where(lane == j1, w1 * gprob, jnp.where(lane == j2, e21 * w1 * gprob, 0.0))
    tri = tri_ref[...]
    onehot = jnp.where(lane == gidx, 1.0, 0.0)
    ranks = jnp.dot(tri, onehot.astype(BF16), preferred_element_type=F32)
    counts = jnp.sum(onehot, axis=0, keepdims=True)
    blocks = jnp.floor((counts + (MOE_ROW_BLOCK - 1)) * (1.0 / MOE_ROW_BLOCK))
    blocks8 = jnp.broadcast_to(blocks, (SUBLANES, LANES)).astype(BF16)
    below = jnp.where(lax.broadcasted_iota(I32, (LANES, LANES), 0) < lax.broadcasted_iota(I32, (LANES, LANES), 1),
                      1.0, 0.0).astype(BF16)
    starts = jnp.dot(blocks8, below, preferred_element_type=F32)[0:1, :] * MOE_ROW_BLOCK
    dst = jnp.sum((ranks + starts) * onehot, axis=1, keepdims=True)
    code_ref[...] = jnp.where(lane == 0, dst, 0.0).T[0:1, :].astype(I32)
    count_ref[...] = starts.astype(I32)


def out_proj_router(attn, rnn, x, g1, sc2, sh2, attn_g, rnn_g, w_out, norm2_g, router_w, router_b, tt):
    n, t, d = x.shape
    c = attn.shape[2]
    per_row = g1.shape[1] != 1
    mod_spec = (pl.BlockSpec((None, tt, d), lambda b, i: (b, i, 0)) if per_row
                else pl.BlockSpec((None, 1, d), lambda b, i: (b, 0, 0)))
    row = lambda w_: pl.BlockSpec((None, tt, w_), lambda b, i: (b, i, 0))
    const = lambda shape: pl.BlockSpec(shape, lambda b, i: (0,) * len(shape))
    nt = t // tt
    per_tile = lambda w_: pl.BlockSpec((None, 1, w_), lambda b, i: (b * nt + i, 0, 0))
    tri = jnp.tril(jnp.ones((tt, tt), BF16), -1)
    return pl.pallas_call(
        _out_router_body,
        grid=(n, nt),
        in_specs=[row(c), row(c), row(d), mod_spec, mod_spec, mod_spec, const((1, c)), const((1, c)),
                  const(w_out.shape), const((1, d)), const(router_w.shape), const((1, LANES)), const((tt, tt))],
        out_specs=[row(d), row(d // 2), row(LANES), per_tile(tt), per_tile(LANES)],
        out_shape=[jax.ShapeDtypeStruct((n, t, d), F32), jax.ShapeDtypeStruct((n, t, d // 2), U32),
                   jax.ShapeDtypeStruct((n, t, LANES), F32), jax.ShapeDtypeStruct((n * nt, 1, tt), I32),
                   jax.ShapeDtypeStruct((n * nt, 1, LANES), I32)],
        compiler_params=_cparams("parallel", "arbitrary"),
        name="out_proj_router",
    )(attn, rnn, x, g1, sc2, sh2, attn_g.reshape(1, c), rnn_g.reshape(1, c), w_out, norm2_g.reshape(1, d),
      router_w, router_b, tri)


def split_bf16(w):
    hi = w.astype(BF16)
    return jnp.stack([hi, (w - hi.astype(F32)).astype(BF16)])


def _moe_body(dst_ref, start_ref, h2_ref, gate_ref, x1_ref, g2_ref, wgu_ref, wd_ref, fg_ref, y_ref,
              hs_ref, hb_ref, gs_ref, acc_ref):
    step = pl.program_id(2)
    tt = h2_ref.shape[0]
    d_exp = wd_ref.shape[1]
    rb = MOE_ROW_BLOCK

    @pl.when(step == 0)
    def _():
        hs_ref[...] = jnp.zeros(hs_ref.shape, U32)
        gs_ref[...] = jnp.zeros(gs_ref.shape, F32)

        def move_in(t, c):
            dst = dst_ref[0, t]
            hs_ref[pl.ds(dst, 1), :] = h2_ref[pl.ds(t, 1), :]
            gs_ref[pl.ds(dst, 1), :] = gate_ref[pl.ds(t, 1), :]
            return c

        lax.fori_loop(0, tt, move_in, 0, unroll=8)
        hb_ref[...] = _unpack_bf16_pairs(hs_ref[...])
        acc_ref[...] = jnp.zeros(acc_ref.shape, F32)

    lane = lax.broadcasted_iota(I32, (rb, LANES), 1)
    for k in range(MOE_EXPERTS_PER_STEP):
        e = step * MOE_EXPERTS_PER_STEP + k
        grp = e // EXPERTS_PER_GROUP

        def block(b, c, k=k, e=e):
            rows = pl.ds(pl.multiple_of(b * rb, rb), rb)
            gcol = jnp.sum(jnp.where(lane == e + EXPERT_LANE0, gs_ref[rows, :], 0.0), axis=1, keepdims=True)
            gu = jnp.dot(hb_ref[rows, :], wgu_ref[k], preferred_element_type=F32)
            gpart, upart = gu[:, :d_exp], gu[:, d_exp:]
            act = (gpart * _sigmoid(gpart) * upart).astype(BF16)
            acc_ref[rows, :] += gcol * jnp.dot(act, wd_ref[k], preferred_element_type=F32)
            return c

        lax.fori_loop(start_ref[0, grp] // rb, start_ref[0, grp + 1] // rb, block, 0)

    @pl.when(step == pl.num_programs(2) - 1)
    def _():
        def move_out(t, c):
            y_ref[pl.ds(t, 1), :] = acc_ref[pl.ds(dst_ref[0, t], 1), :]
            return c

        lax.fori_loop(0, tt, move_out, 0, unroll=8)
        x2 = x1_ref[...] + g2_ref[...] * y_ref[...]
        y_ref[...] = _rms(x2, fg_ref[...])


def moe_final(codes, counts, h2, gates, x1, g2, w_gu, w_down, final_g, tt):
    n, t, d = x1.shape
    n_exp = w_gu.shape[0]
    nt = t // tt
    per_row = g2.shape[1] != 1
    mod_spec = (pl.BlockSpec((None, tt, d), lambda b, i, e: (b, i, 0)) if per_row
                else pl.BlockSpec((None, 1, d), lambda b, i, e: (b, 0, 0)))
    row = lambda w_: pl.BlockSpec((None, tt, w_), lambda b, i, e: (b, i, 0))
    smem_tile = lambda w_: pl.BlockSpec((None, 1, w_), lambda b, i, e: (b * nt + i, 0, 0), memory_space=pltpu.SMEM)
    sort_rows = tt + N_GROUPS * MOE_ROW_BLOCK
    eps = MOE_EXPERTS_PER_STEP
    return pl.pallas_call(
        _moe_body,
        grid=(n, nt, n_exp // eps),
        in_specs=[smem_tile(tt), smem_tile(LANES), row(d // 2), row(LANES), row(d), mod_spec,
                  pl.BlockSpec((eps,) + w_gu.shape[1:], lambda b, i, e: (e, 0, 0)),
                  pl.BlockSpec((eps,) + w_down.shape[1:], lambda b, i, e: (e, 0, 0)),
                  pl.BlockSpec((1, d), lambda b, i, e: (0, 0))],
        out_specs=row(d),
        out_shape=jax.ShapeDtypeStruct((n, t, d), F32),
        scratch_shapes=[pltpu.VMEM((sort_rows, d // 2), U32),
                        pltpu.VMEM((sort_rows, d), BF16),
                        pltpu.VMEM((sort_rows, LANES), F32),
                        pltpu.VMEM((sort_rows, d), F32)],
        compiler_params=_cparams("parallel", "arbitrary", "arbitrary"),
        name="moe_final",
    )(codes, counts, h2, gates, x1, g2, w_gu, w_down, final_g.reshape(1, d))


def _tile(t, pref):
    return pref if t % pref == 0 else t


def _layer(x, mods, pos, attend, conv_prev, h_prev, n_seq, lw, final_g):
    sh1, sc1, g1, sh2, sc2, g2 = mods
    n, t, d = x.shape
    t_seq = n * t // n_seq
    seqs = lambda a: a.reshape(n_seq, t_seq, a.shape[-1])
    q, k, v, kb, vt, iq, ikw, ikb, xr, yg = in_proj(x, sc1, sh1, lw["norm1_g"], lw["w_in"], rope_tables(pos),
                                                     _tile(t, 512))
    attn = attend(q, iq, ikw, k, v, kb, vt, ikb)
    rnn, h_new = rglru(seqs(xr), seqs(yg), conv_prev, h_prev, lw["conv_w"], lw["conv_b"], lw["rg_wa"], lw["rg_ba"],
                       lw["rg_wx"], lw["rg_bx"], lw["rg_lambda"], _tile(t_seq, 512))
    moe_tile = _tile(t, 1024)
    x1, h2, gates, codes, counts = out_proj_router(
        attn, rnn.reshape(xr.shape), x, g1, sc2, sh2, lw["attn_out_g"], lw["rnn_out_g"], lw["w_out"], lw["norm2_g"],
        lw["router_w"], lw["router_b"], moe_tile)
    y = moe_final(codes, counts, h2, gates, x1, g2, lw["w_gu"], lw["w_down"], final_g, moe_tile)
    conv_new = seqs(xr)[:, t_seq - (CONV_WIDTH - 1):, :]
    return (seqs(y), seqs(k).reshape(n_seq, t_seq, N_KV_HEADS, HEAD_DIM),
            seqs(v).reshape(n_seq, t_seq, N_KV_HEADS, HEAD_DIM), seqs(ikw)[:, :, :IDX_HEAD_DIM], conv_new,
            h_new[:, 0, :])


def kernel(x_prompt, x_sample, cache_k, cache_v, cache_idx_k, state_conv, state_rglru, page_table, c_prompt,
           c_sample, ada_w, ada_b, norm1_g, w_in, conv_w, conv_b, rg_wa, rg_ba, rg_wx, rg_bx, rg_lambda, attn_out_g,
           rnn_out_g, w_out, norm2_g, router_grp_w, router_grp_b, router_exp_w, router_exp_b, exp_w_gate, exp_w_up,
           exp_w_down, final_g):
    depth = ada_w.shape[0]
    assert depth == 1, "this implementation handles the single-layer configuration"
    n_p, t_p, d = x_prompt.shape
    n_s, t_s, _ = x_sample.shape
    n_pages = page_table.shape[1]
    past = n_pages * PAGE_SIZE
    l = 0
    router_w = jnp.concatenate([router_grp_w[l], router_exp_w[l],
                                jnp.zeros((d, LANES - N_GROUPS - N_EXPERTS), F32)], axis=1)
    router_b = jnp.concatenate([router_grp_b[l], router_exp_b[l],
                                jnp.zeros((LANES - N_GROUPS - N_EXPERTS,), F32)]).reshape(1, LANES)
    lw = dict(norm1_g=norm1_g[l], w_in=prep_w_in(w_in[l]), conv_w=conv_w[l], conv_b=conv_b[l], rg_wa=rg_wa[l],
              rg_ba=rg_ba[l], rg_wx=rg_wx[l], rg_bx=rg_bx[l], rg_lambda=rg_lambda[l], attn_out_g=attn_out_g[l],
              rnn_out_g=rnn_out_g[l], w_out=w_out[l].astype(BF16), norm2_g=norm2_g[l],
              router_w=split_bf16(router_w), router_b=router_b,
              w_gu=jnp.concatenate([exp_w_gate[l], exp_w_up[l]], axis=2).astype(BF16),
              w_down=exp_w_down[l].astype(BF16))

    mod = ada_mod(jnp.concatenate([c_prompt, c_sample], axis=0), ada_w[l], ada_b[l])
    mods_p = [m[:n_p, None, :] for m in jnp.split(mod, 6, axis=1)]
    mods_s = [jnp.repeat(m[n_p:], t_s, axis=0)[None] for m in jnp.split(mod, 6, axis=1)]

    def attend_prompt(q, iq, ikw, k, v, kb, vt, ikb):
        iw_t = jnp.swapaxes(ikw[:, :, IDX_HEAD_DIM:IDX_HEAD_DIM + N_IDX_HEADS], 1, 2)
        return prompt_attention(q, iq, iw_t, kb, vt, ikb, _tile(t_p, 256))

    def attend_sample(q, iq, ikw, k, v, kb, vt, ikb):
        seqs = lambda a: a.reshape(n_s, t_s, a.shape[-1])
        n_pool = cache_k.shape[1]
        feat_major = lambda c: jnp.swapaxes(c.reshape(n_pool, PAGE_SIZE, -1), 1, 2)
        out = sample_attention(page_table, seqs(q), seqs(iq), seqs(ikw), seqs(k), seqs(v),
                               feat_major(cache_k[l]), feat_major(cache_v[l]), feat_major(cache_idx_k[l]))
        return out.reshape(1, n_s * t_s, N_HEADS * HEAD_DIM)

    c_rnn = conv_w.shape[2]
    ys, ks, vs, iks, cvs, hs = _layer(
        x_sample.reshape(1, n_s * t_s, d), mods_s, jnp.tile(past + jnp.arange(t_s), n_s), attend_sample,
        state_conv[l], state_rglru[l], n_s, lw, final_g)
    yp, kp, vp, ikp, cvp, hp = _layer(
        x_prompt, mods_p, jnp.arange(t_p), attend_prompt,
        jnp.zeros((n_p, CONV_WIDTH - 1, c_rnn), F32), jnp.zeros((n_p, c_rnn), F32), n_p, lw, final_g)

    st = lambda a: a[None]
    return (yp, ys, st(kp), st(vp), st(ikp), st(cvp), st(hp), st(ks), st(vs), st(iks), st(cvs), st(hs))
```

```python
import functools

import jax
import jax.numpy as jnp
from jax import lax
from jax.experimental import pallas as pl
from jax.experimental.pallas import tpu as pltpu

F32, BF16, I32, I16, U32 = jnp.float32, jnp.bfloat16, jnp.int32, jnp.int16, jnp.uint32
HIGHEST = lax.Precision.HIGHEST

HEAD_DIM = 64
N_HEADS = 8
N_KV_HEADS = 2
HEADS_PER_KV = N_HEADS // N_KV_HEADS
N_IDX_HEADS = 8
IDX_HEAD_DIM = 32
TOPK_MAX = 256
ROPE_THETA = 10000.0
CONV_WIDTH = 4
RG_BLOCKS = 8
RG_C = 8.0
N_GROUPS = 4
EXPERTS_PER_GROUP = 8
N_EXPERTS = N_GROUPS * EXPERTS_PER_GROUP
PAGE_SIZE = 128
NORM_EPS = 1e-6

LANES = 128
SUBLANES = 8
VMEM_LIMIT = 48 * 1024 * 1024
INT_MIN = -(2 ** 31)
I16_MIN = -(2 ** 15)
NEG_BIG = -1e30
EXPERT_LANE0 = N_GROUPS
BF16_SUBLANES = 16
V_AUG_ROWS = HEAD_DIM + BF16_SUBLANES
ATT_CHUNK_HEADS = 2
COUNT_TILES = 2
MOE_ROW_BLOCK = 304
MOE_EXPERTS_PER_STEP = 2
LOG2_E = 1.4426950408889634
Q_SCALE = HEAD_DIM ** -0.5 * LOG2_E


def _cparams(*sem):
    return pltpu.CompilerParams(dimension_semantics=sem, vmem_limit_bytes=VMEM_LIMIT)


def _rms(x, g):
    return x * lax.rsqrt(jnp.mean(x * x, axis=-1, keepdims=True) + NORM_EPS) * g


def _sigmoid(x):
    return 1.0 / (1.0 + jnp.exp(-x))


def _float_key(s):
    b = pltpu.bitcast(s + 0.0, I32)
    return b ^ ((b >> 31) & 0x7FFFFFFF)


def _ada_body(c_ref, w_ref, b_ref, o_ref):
    c = c_ref[...]
    s = c * _sigmoid(c)
    o_ref[...] = jnp.dot(s, w_ref[...], preferred_element_type=F32, precision=HIGHEST) + b_ref[...]


def ada_mod(c, w, b):
    m, d = c.shape
    n_chunks = w.shape[1] // d
    return pl.pallas_call(
        _ada_body,
        grid=(n_chunks,),
        in_specs=[pl.BlockSpec((m, d), lambda j: (0, 0)),
                  pl.BlockSpec((d, d), lambda j: (0, j)),
                  pl.BlockSpec((1, d), lambda j: (0, j))],
        out_specs=pl.BlockSpec((m, d), lambda j: (0, j)),
        out_shape=jax.ShapeDtypeStruct((m, w.shape[1]), F32),
        compiler_params=_cparams("arbitrary"),
        name="ada_mod",
    )(c, w, b.reshape(1, -1))


_C_Q, _C_QR, _C_K, _C_KR, _C_V, _C_IQ, _C_IQR, _C_IKW, _C_IKR, _C_XR, _C_YG, _C_END = (
    0, 512, 1024, 1152, 1280, 1408, 1664, 1920, 2048, 2176, 2688, 3200)


def _rot_half_cols(w, head_dim):
    d_in, n = w.shape
    wh = w.reshape(d_in, n // head_dim, 2, head_dim // 2)
    return jnp.stack([-wh[:, :, 1], wh[:, :, 0]], axis=2).reshape(d_in, n)


def prep_w_in(w_in):
    d_in = w_in.shape[0]
    q, k, v, iq, ik, iw, xr, yg = jnp.split(w_in, [512, 640, 768, 1024, 1056, 1064, 1576], axis=1)
    z = lambda n: jnp.zeros((d_in, n), w_in.dtype)
    ikw = jnp.concatenate([ik, iw, z(LANES - IDX_HEAD_DIM - N_IDX_HEADS)], axis=1)
    ikr = jnp.concatenate([_rot_half_cols(ik, IDX_HEAD_DIM), z(LANES - IDX_HEAD_DIM)], axis=1)
    w = jnp.concatenate([q, _rot_half_cols(q, HEAD_DIM), k, _rot_half_cols(k, HEAD_DIM), v,
                         iq, _rot_half_cols(iq, IDX_HEAD_DIM), ikw, ikr, xr, yg], axis=1)
    return w.astype(BF16)


def rope_tables(pos):
    pos = pos.astype(F32)[:, None]

    def cs(half, reps):
        inv = ROPE_THETA ** (-jnp.arange(half, dtype=F32) / half)
        ang = pos * inv[None, :]
        return jnp.tile(jnp.cos(ang), (1, 2 * reps)), jnp.tile(jnp.sin(ang), (1, 2 * reps))

    cq, sq = cs(HEAD_DIM // 2, LANES // HEAD_DIM)
    ci, si = cs(IDX_HEAD_DIM // 2, LANES // IDX_HEAD_DIM)
    lane = jnp.arange(LANES)[None, :]
    w_scale = (IDX_HEAD_DIM ** -0.5) * (N_IDX_HEADS ** -0.5)
    cx = jnp.where(lane < IDX_HEAD_DIM, ci, jnp.where(lane < IDX_HEAD_DIM + N_IDX_HEADS, w_scale, 0.0))
    sx = jnp.where(lane < IDX_HEAD_DIM, si, 0.0)
    return cq, sq, ci, si, cx.astype(F32), sx.astype(F32)


def _in_proj_body(x_ref, sc_ref, sh_ref, g_ref, w_ref, cq_ref, sq_ref, ci_ref, si_ref, cx_ref, sx_ref,
                  q_ref, k_ref, v_ref, kb_ref, vt_ref, iq_ref, ikw_ref, ikb_ref, xr_ref, yg_ref):
    x = x_ref[...]
    h = (_rms(x, g_ref[...]) * (1.0 + sc_ref[...]) + sh_ref[...]).astype(BF16)

    def seg(a, b):
        return jnp.dot(h, w_ref[:, a:b], preferred_element_type=F32)

    cq, sq = cq_ref[...], sq_ref[...]
    cq4 = jnp.concatenate([cq] * 4, axis=1)
    sq4 = jnp.concatenate([sq] * 4, axis=1)
    q = (seg(_C_Q, _C_QR) * cq4 + seg(_C_QR, _C_K) * sq4) * Q_SCALE
    q_ref[...] = q.astype(BF16)
    k = seg(_C_K, _C_KR) * cq + seg(_C_KR, _C_V) * sq
    k_ref[...] = k
    kb_ref[...] = k.astype(BF16)
    v = seg(_C_V, _C_IQ)
    v_ref[...] = v
    vt_ref[...] = v.T.astype(BF16)
    ci2 = jnp.concatenate([ci_ref[...]] * 2, axis=1)
    si2 = jnp.concatenate([si_ref[...]] * 2, axis=1)
    iq_ref[...] = (seg(_C_IQ, _C_IQR) * ci2 + seg(_C_IQR, _C_IKW) * si2).astype(BF16)
    ikw = seg(_C_IKW, _C_IKR) * cx_ref[...] + seg(_C_IKR, _C_XR) * sx_ref[...]
    ikw_ref[...] = ikw
    ikb_ref[...] = ikw.astype(BF16)
    xr_ref[...] = seg(_C_XR, _C_YG)
    yg_ref[...] = seg(_C_YG, _C_END)


def in_proj(x, sc, sh, g, w, tables, tt):
    n, t, d = x.shape
    per_row = sc.shape[1] != 1
    mod_spec = (pl.BlockSpec((None, tt, d), lambda b, i: (b, i, 0)) if per_row
                else pl.BlockSpec((None, 1, d), lambda b, i: (b, 0, 0)))
    row = lambda w_: pl.BlockSpec((None, tt, w_), lambda b, i: (b, i, 0))
    tab = pl.BlockSpec((tt, LANES), lambda b, i: (i, 0))
    outs = [(512, BF16), (128, F32), (128, F32), (128, BF16), None, (256, BF16), (128, F32), (128, BF16),
            (512, F32), (512, F32)]
    out_shape, out_specs = [], []
    for o in outs:
        if o is None:
            out_shape.append(jax.ShapeDtypeStruct((n, LANES, t), BF16))
            out_specs.append(pl.BlockSpec((None, LANES, tt), lambda b, i: (b, 0, i)))
        else:
            out_shape.append(jax.ShapeDtypeStruct((n, t, o[0]), o[1]))
            out_specs.append(row(o[0]))
    return pl.pallas_call(
        _in_proj_body,
        grid=(n, t // tt),
        in_specs=[row(d), mod_spec, mod_spec,
                  pl.BlockSpec((1, d), lambda b, i: (0, 0)),
                  pl.BlockSpec(w.shape, lambda b, i: (0, 0))] + [tab] * 6,
        out_specs=out_specs,
        out_shape=out_shape,
        compiler_params=_cparams("parallel", "arbitrary"),
        name="in_proj",
    )(x, sc, sh, g.reshape(1, d), w, *tables)


def _pattn_body(q_ref, iq_ref, iwt_ref, k_ref, vt_ref, ik_ref, o_ref,
                hi_ref, lo_ref, lb_ref, p_ref, m_ref, acc_ref, mm_ref, *, n_sel, idx_bits):
    qi = pl.program_id(1)
    tq = q_ref.shape[0]
    tk = tq
    n_tiles = qi + 1

    iq_t = iq_ref[...].astype(F32).T
    iq_st = jnp.concatenate([iq_t[IDX_HEAD_DIM * h:IDX_HEAD_DIM * (h + 1)] for h in range(N_IDX_HEADS)],
                            axis=1).astype(BF16)
    w_t = iwt_ref[...]
    q_t = q_ref[...].astype(F32).T
    zero_half = jnp.zeros((HEAD_DIM, tq), F32)
    q_pad = []
    for g in range(N_KV_HEADS):
        cols = []
        for hh in range(HEADS_PER_KV):
            h = g * HEADS_PER_KV + hh
            qh = q_t[HEAD_DIM * h:HEAD_DIM * (h + 1)]
            cols.append(jnp.concatenate([qh, zero_half] if g == 0 else [zero_half, qh], axis=0))
        q_pad.append(jnp.concatenate(cols, axis=1).astype(BF16))

    row_s = lax.broadcasted_iota(I32, (tk, tq), 0)
    col_t = lax.broadcasted_iota(I32, (tk, tq), 1)
    causal = row_s <= col_t

    def tile_off(j):
        return pl.multiple_of(j * tk, tk)

    def walk_tiles(step):
        def pair(p, c):
            step(2 * p, 0, False)
            step(2 * p + 1, 1, False)
            return c

        lax.fori_loop(0, qi // 2, pair, 0)

        @pl.when(qi % 2 == 0)
        def _():
            step(qi, 0, True)

        @pl.when(qi % 2 == 1)
        def _():
            step(qi - 1, 0, False)
            step(qi, 1, True)

    def index_dots(j, slot):
        ik = ik_ref[pl.ds(tile_off(j), tk), :][:, :IDX_HEAD_DIM]
        mm_ref[slot] = jnp.dot(ik, iq_st, preferred_element_type=F32)

    def p1(j, slot, last):
        if not last:
            index_dots(j + 1, 1 - slot)
        sc = w_t[0:1, :] * jnp.maximum(mm_ref[slot, :, 0:tq], 0.0)
        for h in range(1, N_IDX_HEADS):
            sc = sc + w_t[h:h + 1, :] * jnp.maximum(mm_ref[slot, :, h * tq:(h + 1) * tq], 0.0)
        key = _float_key(sc)
        if last:
            key = jnp.where(causal, key, INT_MIN)
        rows = pl.ds(tile_off(j), tk)
        hi_ref[rows, :] = (key >> 16).astype(I16)
        lo_ref[rows, :] = ((key & 0xFFFF) + I16_MIN).astype(I16)

    index_dots(0, 0)
    walk_tiles(p1)

    span = COUNT_TILES * tk
    n_spans = (n_tiles + COUNT_TILES - 1) // COUNT_TILES
    groups = span // BF16_SUBLANES
    pad_rows = pl.ds(tile_off(n_tiles), tk)
    hi_ref[pad_rows, :] = jnp.full((tk, tq), I16_MIN, I16)
    lo_ref[pad_rows, :] = jnp.full((tk, tq), I16_MIN, I16)

    def rows16(v):
        return jnp.broadcast_to(v.astype(I16), (BF16_SUBLANES, tq))[None]

    def span_rows(j):
        return pl.ds(pl.multiple_of(j * span, span), span)

    def tile16(ref, j):
        return ref[span_rows(j), :].reshape(groups, BF16_SUBLANES, tq)

    one16, zero16 = jnp.int16(1), jnp.int16(0)

    def count(pred):
        def body(j, acc):
            marks = pred(j)
            parts = [marks[g] for g in range(groups)]
            while len(parts) > 1:
                parts = [parts[i] + parts[i + 1] for i in range(0, len(parts), 2)]
            return acc + parts[0]

        acc = lax.fori_loop(0, n_spans, body, jnp.zeros((BF16_SUBLANES, tq), I16))
        return jnp.sum(acc.astype(I32).astype(F32), axis=0, keepdims=True)

    def bisect16(ref, need):
        def step(b, thr):
            cand = thr + jnp.left_shift(jnp.int32(1), 15 - b)
            cand_b = rows16(cand)
            c = count(lambda j: jnp.where(tile16(ref, j) >= cand_b, one16, zero16))
            return jnp.where(c >= need, cand, thr)
        return lax.fori_loop(0, 16, step, jnp.full((1, tq), I16_MIN, I32))

    thr_hi = bisect16(hi_ref, n_sel)
    hi_b = rows16(thr_hi)
    c_above = count(lambda j: jnp.where(tile16(hi_ref, j) > hi_b, one16, zero16))

    def p2(j, c):
        rows = span_rows(j)
        lb_ref[rows, :] = jnp.where(hi_ref[rows, :] == thr_hi.astype(I16), lo_ref[rows, :], jnp.int16(I16_MIN))
        return c

    lax.fori_loop(0, n_spans, p2, 0)
    thr_lo = bisect16(lb_ref, n_sel - c_above)
    lo_b = rows16(thr_lo)
    c_gt = c_above + count(lambda j: jnp.where(tile16(lb_ref, j) > lo_b, one16, zero16))
    ties_left = n_sel - c_gt

    def is_tie(j):
        return jnp.where(tile16(hi_ref, j) == hi_b, jnp.where(tile16(lo_ref, j) == lo_b, one16, zero16), zero16)

    c_tie = count(is_tie)
    p_ref[...] = jnp.full((1, tq), 2 ** idx_bits - 1, I32)
    has_thr = jnp.where(thr_hi > I16_MIN, 1, jnp.where(thr_lo > I16_MIN, 1, 0))
    excess = jnp.max(jnp.where(has_thr > 0, c_tie - ties_left, 0.0)) > 0.0

    idx16 = row_s.astype(I16)

    @pl.when(excess)
    def _():
        sub = lax.broadcasted_iota(I32, (span, tq), 0).astype(I16).reshape(groups, BF16_SUBLANES, tq)

        def tb(b, p):
            cand = p | jnp.left_shift(jnp.int32(1), idx_bits - 1 - b)
            c = count(lambda j: jnp.where(sub < rows16(cand - j * span), is_tie(j), zero16))
            return jnp.where(c < ties_left, cand, p)

        p_ref[...] = lax.fori_loop(0, idx_bits, tb, jnp.zeros((1, tq), I32))

    p_last = p_ref[...]
    thr_hi16, thr_lo16 = thr_hi.astype(I16), thr_lo.astype(I16)

    m_ref[...] = jnp.full(m_ref.shape, NEG_BIG, F32)
    acc_ref[...] = jnp.zeros(acc_ref.shape, F32)
    ones_rows = jnp.ones((V_AUG_ROWS - HEAD_DIM, tk), BF16)

    w_c = ATT_CHUNK_HEADS * tq
    chunks = [(g, c * w_c) for g in range(N_KV_HEADS) for c in range(HEADS_PER_KV // ATT_CHUNK_HEADS)]

    def score_chunk(kt, slot, ci):
        g, c0 = chunks[ci]
        mm_ref[slot, :, ci * w_c:(ci + 1) * w_c] = jnp.dot(kt, q_pad[g][:, c0:c0 + w_c],
                                                             preferred_element_type=F32)

    kt0 = k_ref[pl.ds(0, tk), :]
    for ci in range(len(chunks)):
        score_chunk(kt0, 0, ci)

    def attend(j, slot, diag):
        off = tile_off(j)
        hi, lo = hi_ref[pl.ds(off, tk), :], lo_ref[pl.ds(off, tk), :]
        keep_tie = jnp.where(idx16 <= (p_last - j * tk).astype(I16), one16, zero16)
        sel = jnp.where(hi > thr_hi16, one16,
                        jnp.where(hi == thr_hi16,
                                  jnp.where(lo > thr_lo16, one16, jnp.where(lo == thr_lo16, keep_tie, zero16)),
                                  zero16))
        bias = jnp.where(sel.astype(I32) > 0, 0.0, NEG_BIG)
        if diag:
            bias = jnp.where(causal, bias, NEG_BIG)
        vt = vt_ref[:, pl.ds(off, tk)]
        v_aug = [jnp.concatenate([vt[HEAD_DIM * g:HEAD_DIM * (g + 1)], ones_rows], axis=0)
                 for g in range(N_KV_HEADS)]
        bias_c = jnp.concatenate([bias] * ATT_CHUNK_HEADS, axis=1)
        if not diag:
            kt_next = k_ref[pl.ds(tile_off(j + 1), tk), :]
        for ci, (g, c0) in enumerate(chunks):
            if not diag:
                score_chunk(kt_next, 1 - slot, ci)
            cols = slice(c0, c0 + w_c)
            s = mm_ref[slot, :, ci * w_c:(ci + 1) * w_c] + bias_c
            m_old = m_ref[g:g + 1, cols]
            m_new = jnp.maximum(m_old, jnp.max(s, axis=0, keepdims=True))
            alpha = jnp.exp2(m_old - m_new)
            p = jnp.exp2(s - m_new).astype(BF16)
            acc_ref[g, :, cols] = alpha * acc_ref[g, :, cols] + jnp.dot(v_aug[g], p, preferred_element_type=F32)
            m_ref[g:g + 1, cols] = m_new

    walk_tiles(attend)

    rows = []
    for g in range(N_KV_HEADS):
        acc = acc_ref[g]
        o_g = acc[:HEAD_DIM] / acc[HEAD_DIM:HEAD_DIM + 1]
        rows += [o_g[:, hh * tq:(hh + 1) * tq] for hh in range(HEADS_PER_KV)]
    o_ref[...] = jnp.concatenate(rows, axis=0).T


def prompt_attention(q, iq, iw_t, kb, vt, ikb, tq):
    n, t, _ = q.shape
    n_sel = min(TOPK_MAX, t // 4)
    idx_bits = max(1, (t - 1).bit_length())
    body = functools.partial(_pattn_body, n_sel=n_sel, idx_bits=idx_bits)
    return pl.pallas_call(
        body,
        grid=(n, t // tq),
        in_specs=[pl.BlockSpec((None, tq, q.shape[2]), lambda b, i: (b, i, 0)),
                  pl.BlockSpec((None, tq, iq.shape[2]), lambda b, i: (b, i, 0)),
                  pl.BlockSpec((None, N_IDX_HEADS, tq), lambda b, i: (b, 0, i)),
                  pl.BlockSpec((None, t, LANES), lambda b, i: (b, 0, 0)),
                  pl.BlockSpec((None, LANES, t), lambda b, i: (b, 0, 0)),
                  pl.BlockSpec((None, t, LANES), lambda b, i: (b, 0, 0))],
        out_specs=pl.BlockSpec((None, tq, N_HEADS * HEAD_DIM), lambda b, i: (b, i, 0)),
        out_shape=jax.ShapeDtypeStruct((n, t, N_HEADS * HEAD_DIM), F32),
        scratch_shapes=[pltpu.VMEM((t + tq, tq), I16),
                        pltpu.VMEM((t + tq, tq), I16),
                        pltpu.VMEM((t + tq, tq), I16),
                        pltpu.VMEM((1, tq), I32),
                        pltpu.VMEM((N_KV_HEADS, HEADS_PER_KV * tq), F32),
                        pltpu.VMEM((N_KV_HEADS, V_AUG_ROWS, HEADS_PER_KV * tq), F32),
                        pltpu.VMEM((2, tq, N_HEADS * tq), F32)],
        compiler_params=_cparams("parallel", "arbitrary"),
        name="prompt_attention",
    )(q, iq, iw_t, kb, vt, ikb)


Q_PAD = SUBLANES


def _sattn_body(pt_ref, q_ref, iq_ref, iw_ref, kn_ref, vn_ref, ikn_ref, ck_hbm, cv_hbm, cik_hbm, o_ref,
                kbuf, vbuf, ikbuf, s_ref, p_ref, sem, *, n_pages, n_sel, idx_bits):
    b = pl.program_id(0)
    slot = b % 2
    past = n_pages * PAGE_SIZE
    lk = past + LANES

    streams = ((cik_hbm, ikbuf), (ck_hbm, kbuf), (cv_hbm, vbuf))

    def page_copy(kind, p, seq, slt):
        src, dst = streams[kind]
        return pltpu.make_async_copy(src.at[pt_ref[seq, p]], dst.at[slt, :, pl.ds(p * PAGE_SIZE, PAGE_SIZE)],
                                     sem.at[slt, kind])

    def start_pages(seq, slt):
        for kind in range(len(streams)):
            for p in range(n_pages):
                page_copy(kind, p, seq, slt).start()

    def wait_pages(kind):
        for p in range(n_pages):
            page_copy(kind, p, b, slot).wait()

    @pl.when(b == 0)
    def _():
        start_pages(0, 0)

    @pl.when(b + 1 < pl.num_programs(0))
    def _():
        start_pages(b + 1, 1 - slot)

    def new_cols(ref):
        rows = jnp.concatenate([ref[...], jnp.zeros((LANES - Q_PAD, LANES), F32)], axis=0)
        return rows.T

    new_tile = pl.ds(past, LANES)
    kbuf[slot, :, new_tile] = new_cols(kn_ref)
    vbuf[slot, :, new_tile] = new_cols(vn_ref)
    ikbuf[slot, :, new_tile] = new_cols(ikn_ref)[:IDX_HEAD_DIM]

    iq = iq_ref[...].astype(F32)
    iq_st = jnp.concatenate([iq[:, IDX_HEAD_DIM * h:IDX_HEAD_DIM * (h + 1)] for h in range(N_IDX_HEADS)],
                            axis=0).astype(BF16)
    qf = q_ref[...].astype(F32)
    lane128 = lax.broadcasted_iota(I32, (Q_PAD, LANES), 1)
    q_rows = []
    for h in range(N_HEADS):
        g = h // HEADS_PER_KV
        slab = qf[:, LANES * (h // 2):LANES * (h // 2 + 1)]
        if (h % 2) != g:
            slab = pltpu.roll(slab, HEAD_DIM, axis=1)
        keep = (lane128 < HEAD_DIM) if g == 0 else (lane128 >= HEAD_DIM)
        q_rows.append(jnp.where(keep, slab, 0.0))
    q_pad = jnp.concatenate(q_rows, axis=0).astype(BF16)
    iw = iw_ref[...]

    key_pos = lax.broadcasted_iota(I32, (Q_PAD, lk), 1)
    q_row = lax.broadcasted_iota(I32, (Q_PAD, lk), 0)
    admissible = key_pos <= past + q_row

    wait_pages(0)
    d = jnp.dot(iq_st, ikbuf[slot].astype(BF16), preferred_element_type=F32)
    sc = None
    for h in range(N_IDX_HEADS):
        w_h = iw[:, IDX_HEAD_DIM + h:IDX_HEAD_DIM + h + 1]
        term = w_h * jnp.maximum(d[Q_PAD * h:Q_PAD * (h + 1)], 0.0)
        sc = term if sc is None else sc + term
    s_ref[...] = jnp.where(admissible, _float_key(sc), INT_MIN)

    lane_pos = lax.broadcasted_iota(I32, (Q_PAD, LANES), 1)

    def count(pred):
        terms = [pred(s_ref[:, c * LANES:(c + 1) * LANES], c * LANES) for c in range(lk // LANES)]
        while len(terms) > 1:
            terms = [terms[i] + terms[i + 1] if i + 1 < len(terms) else terms[i] for i in range(0, len(terms), 2)]
        return jnp.sum(terms[0], axis=1, keepdims=True)

    def counts3(c1, c2, c3):
        terms = [[], [], []]
        for c in range(lk // LANES):
            t = s_ref[:, c * LANES:(c + 1) * LANES]
            for k, cand in enumerate((c1, c2, c3)):
                terms[k].append(jnp.where(t >= cand, 1.0, 0.0))
        sums = []
        for ts in terms:
            while len(ts) > 1:
                ts = [ts[i] + ts[i + 1] if i + 1 < len(ts) else ts[i] for i in range(0, len(ts), 2)]
            sums.append(jnp.sum(ts[0], axis=1, keepdims=True))
        return sums

    def quaternary(i, thr):
        q = jnp.left_shift(jnp.int32(1), 30 - 2 * i)
        cands = [jnp.broadcast_to(thr + k * q, (Q_PAD, LANES)) for k in (1, 2, 3)]
        n1, n2, n3 = counts3(*cands)
        passed = (jnp.where(n1 >= n_sel, 1, 0) + jnp.where(n2 >= n_sel, 1, 0) + jnp.where(n3 >= n_sel, 1, 0))
        return thr + passed * q

    thr = lax.fori_loop(0, 16, quaternary, jnp.full((Q_PAD, 1), INT_MIN, I32))
    thr_b = jnp.broadcast_to(thr, (Q_PAD, LANES))
    ties_left = n_sel - count(lambda t, c0: jnp.where(t > thr_b, 1.0, 0.0))
    c_tie = count(lambda t, c0: jnp.where(t == thr_b, 1.0, 0.0))
    p_ref[...] = jnp.full((Q_PAD, 1), 2 ** idx_bits - 1, I32)

    @pl.when(jnp.max(jnp.where(thr > INT_MIN, c_tie - ties_left, 0.0)) > 0.0)
    def _():
        def tb(i, pidx):
            cand = jnp.broadcast_to(pidx | jnp.left_shift(jnp.int32(1), idx_bits - 1 - i), (Q_PAD, LANES))
            c = count(lambda t, c0: jnp.where(t == thr_b, jnp.where(lane_pos + c0 < cand, 1.0, 0.0), 0.0))
            return jnp.where(c < ties_left, cand[:, :1], pidx)

        p_ref[...] = lax.fori_loop(0, idx_bits, tb, jnp.zeros((Q_PAD, 1), I32))

    p_last = p_ref[...]
    t = s_ref[...]
    keep_tie = jnp.where(key_pos <= p_last, 0.0, NEG_BIG)
    bias = jnp.where(t > thr, 0.0, jnp.where(t == thr, keep_tie, NEG_BIG))
    bias = jnp.where(admissible, bias, NEG_BIG)

    wait_pages(1)
    s = jnp.dot(q_pad, kbuf[slot].astype(BF16), preferred_element_type=F32)
    s = s + jnp.concatenate([bias] * N_HEADS, axis=0)
    m = jnp.max(s, axis=1, keepdims=True)
    pexp = jnp.exp2(s - m)
    l = jnp.sum(pexp, axis=1, keepdims=True)
    wait_pages(2)
    o = lax.dot_general(pexp.astype(BF16), vbuf[slot].astype(BF16), (((1,), (1,)), ((), ())),
                        preferred_element_type=F32)
    o_ref[...] = o / l


def sample_attention(page_table, q, iq, ikw, k_new, v_new, cache_k, cache_v, cache_ik):
    n, n_new, _ = q.shape
    assert n_new <= Q_PAD
    n_pages = page_table.shape[1]
    past = n_pages * PAGE_SIZE
    lk = past + LANES
    n_sel = min(TOPK_MAX, (past + n_new) // 4)
    idx_bits = max(1, (lk - 1).bit_length())
    body = functools.partial(_sattn_body, n_pages=n_pages, n_sel=n_sel, idx_bits=idx_bits)
    pad = lambda a: jnp.pad(a, ((0, 0), (0, Q_PAD - n_new), (0, 0)))
    q, iq, ikw, k_new, v_new = pad(q), pad(iq), pad(ikw), pad(k_new), pad(v_new)
    seq = lambda w_: pl.BlockSpec((None, Q_PAD, w_), lambda b, pt: (b, 0, 0))
    hbm = pl.BlockSpec(memory_space=pl.ANY)
    grid_spec = pltpu.PrefetchScalarGridSpec(
        num_scalar_prefetch=1,
        grid=(n,),
        in_specs=[seq(q.shape[2]), seq(iq.shape[2]), seq(LANES), seq(LANES), seq(LANES), seq(LANES), hbm, hbm, hbm],
        out_specs=pl.BlockSpec((None, N_HEADS * Q_PAD, LANES), lambda b, pt: (b, 0, 0)),
        scratch_shapes=[pltpu.VMEM((2, LANES, lk), F32),
                        pltpu.VMEM((2, LANES, lk), F32),
                        pltpu.VMEM((2, IDX_HEAD_DIM, lk), F32),
                        pltpu.VMEM((Q_PAD, lk), I32),
                        pltpu.VMEM((Q_PAD, 1), I32),
                        pltpu.SemaphoreType.DMA((2, 3))],
    )
    raw = pl.pallas_call(
        body,
        grid_spec=grid_spec,
        out_shape=jax.ShapeDtypeStruct((n, N_HEADS * Q_PAD, LANES), F32),
        compiler_params=_cparams("arbitrary"),
        name="sample_attention",
    )(page_table, q, iq, ikw, k_new, v_new, ikw, cache_k, cache_v, cache_ik)
    raw = raw.reshape(n, N_HEADS, Q_PAD, N_KV_HEADS, HEAD_DIM)[:, :, :n_new]
    per_head = jnp.stack([raw[:, h, :, h // HEADS_PER_KV] for h in range(N_HEADS)], axis=2)
    return per_head.reshape(n, n_new, N_HEADS * HEAD_DIM)


def _rglru_body(xr_ref, yg_ref, cprev_ref, hprev_ref, cw_ref, cb_ref, wa_ref, ba_ref, wx_ref, bx_ref, lam_ref,
                y_ref, hlast_ref, xbuf, a_buf, b_buf, h_buf, h_carry):
    i = pl.program_id(1)
    tt = xr_ref.shape[0]
    halo = SUBLANES

    @pl.when(i == 0)
    def _():
        xbuf[pl.ds(0, halo), :] = jnp.zeros((halo, xbuf.shape[1]), F32)
        xbuf[pl.ds(halo - (CONV_WIDTH - 1), CONV_WIDTH - 1), :] = cprev_ref[...]
        h_carry[...] = hprev_ref[...]

    x = xr_ref[...]
    xbuf[pl.ds(halo, tt), :] = x
    xc = cb_ref[...] + cw_ref[CONV_WIDTH - 1:CONV_WIDTH, :] * x
    for j in range(CONV_WIDTH - 1):
        xc = xc + cw_ref[j:j + 1, :] * xbuf[pl.ds(halo - (CONV_WIDTH - 1) + j, tt), :]
    xcb = xc.astype(BF16)
    r = _sigmoid(jnp.dot(xcb, wa_ref[...], preferred_element_type=F32) + ba_ref[...])
    ig = _sigmoid(jnp.dot(xcb, wx_ref[...], preferred_element_type=F32) + bx_ref[...])
    z = -lam_ref[...]
    softplus = jnp.maximum(z, 0.0) + jnp.log1p(jnp.exp(-jnp.abs(z)))
    log_a = -RG_C * r * softplus
    a = jnp.exp(log_a)
    a_buf[...] = a
    b_buf[...] = jnp.sqrt(jnp.tanh(-log_a) * (1.0 + a * a)) * (ig * xc)

    def step(t, h):
        h = a_buf[pl.ds(t, 1), :] * h + b_buf[pl.ds(t, 1), :]
        h_buf[pl.ds(t, 1), :] = h
        return h

    h_fin = lax.fori_loop(0, tt, step, h_carry[...], unroll=min(8, tt))
    h_carry[...] = h_fin
    hlast_ref[...] = h_fin
    y_ref[...] = h_buf[...] * jax.nn.gelu(yg_ref[...])
    xbuf[pl.ds(halo - (CONV_WIDTH - 1), CONV_WIDTH - 1), :] = xbuf[pl.ds(halo + tt - (CONV_WIDTH - 1), CONV_WIDTH - 1), :]


def _block_diag(w):
    k, c, d = w.shape
    eye = jnp.eye(k, dtype=w.dtype)
    return (eye[:, None, :, None] * w[:, :, None, :]).reshape(k * c, k * d)


def rglru(xr, yg, conv_prev, h_prev, conv_w, conv_b, rg_wa, rg_ba, rg_wx, rg_bx, rg_lambda, tt):
    n, t, c = xr.shape
    row = pl.BlockSpec((None, tt, c), lambda b, i: (b, i, 0))
    const = lambda shape: pl.BlockSpec(shape, lambda b, i: (0,) * len(shape))
    vec = lambda a: a.reshape(1, c)
    return pl.pallas_call(
        _rglru_body,
        grid=(n, t // tt),
        in_specs=[row, row,
                  pl.BlockSpec((None, CONV_WIDTH - 1, c), lambda b, i: (b, 0, 0)),
                  pl.BlockSpec((None, 1, c), lambda b, i: (b, 0, 0)),
                  const((CONV_WIDTH, c)), const((1, c)), const((c, c)), const((1, c)), const((c, c)),
                  const((1, c)), const((1, c))],
        out_specs=[row, pl.BlockSpec((None, 1, c), lambda b, i: (b, 0, 0))],
        out_shape=[jax.ShapeDtypeStruct((n, t, c), F32), jax.ShapeDtypeStruct((n, 1, c), F32)],
        scratch_shapes=[pltpu.VMEM((tt + SUBLANES, c), F32), pltpu.VMEM((tt, c), F32), pltpu.VMEM((tt, c), F32),
                        pltpu.VMEM((tt, c), F32), pltpu.VMEM((1, c), F32)],
        compiler_params=_cparams("parallel", "arbitrary"),
        name="rglru",
    )(xr, yg, conv_prev, h_prev.reshape(n, 1, c), conv_w, vec(conv_b), _block_diag(rg_wa).astype(BF16), vec(rg_ba),
      _block_diag(rg_wx).astype(BF16), vec(rg_bx), vec(rg_lambda))


def _pack_bf16_pairs(x):
    c = x.shape[1] // 2
    a = pltpu.bitcast(x[:, :c].astype(BF16).astype(F32), U32)
    b = pltpu.bitcast(x[:, c:].astype(BF16).astype(F32), U32)
    return (a & jnp.uint32(0xFFFF0000)) | (b >> 16)


def _unpack_bf16_pairs(u):
    a = pltpu.bitcast(u & jnp.uint32(0xFFFF0000), F32)
    b = pltpu.bitcast(u << 16, F32)
    return jnp.concatenate([a, b], axis=1).astype(BF16)


def _out_router_body(attn_ref, rnn_ref, x_ref, g1_ref, sc2_ref, sh2_ref, ag_ref, rg_ref, wo_ref, n2_ref,
                     rw_ref, rb_ref, tri_ref, x1_ref, h2_ref, gate_ref, code_ref, count_ref):
    mixed = jnp.concatenate([_rms(attn_ref[...], ag_ref[...]), _rms(rnn_ref[...], rg_ref[...])], axis=1)
    x1 = x_ref[...] + g1_ref[...] * jnp.dot(mixed.astype(BF16), wo_ref[...], preferred_element_type=F32)
    x1_ref[...] = x1
    h2 = _rms(x1, n2_ref[...]) * (1.0 + sc2_ref[...]) + sh2_ref[...]
    h2_hi = h2.astype(BF16)
    h2_ref[...] = _pack_bf16_pairs(h2)
    h2_lo = (h2 - h2_hi.astype(F32)).astype(BF16)
    z = (jnp.dot(h2_hi, rw_ref[0], preferred_element_type=F32) + jnp.dot(h2_hi, rw_ref[1], preferred_element_type=F32)
         + jnp.dot(h2_lo, rw_ref[0], preferred_element_type=F32)) + rb_ref[...]
    lane = lax.broadcasted_iota(I32, z.shape, 1)
    neg_inf = -jnp.inf
    gl = jnp.where(lane < N_GROUPS, z, neg_inf)
    gmax = jnp.max(gl, axis=1, keepdims=True)
    gidx = jnp.min(jnp.where(gl == gmax, lane, LANES), axis=1, keepdims=True)
    gprob = 1.0 / jnp.sum(jnp.exp(gl - gmax), axis=1, keepdims=True)
    lo = EXPERT_LANE0 + EXPERTS_PER_GROUP * gidx
    el = jnp.where(lane >= lo, jnp.where(lane < lo + EXPERTS_PER_GROUP, z, neg_inf), neg_inf)
    m1 = jnp.max(el, axis=1, keepdims=True)
    j1 = jnp.min(jnp.where(el == m1, lane, LANES), axis=1, keepdims=True)
    el2 = jnp.where(lane == j1, neg_inf, el)
    m2 = jnp.max(el2, axis=1, keepdims=True)
    j2 = jnp.min(jnp.where(el2 == m2, lane, LANES), axis=1, keepdims=True)
    e21 = jnp.exp(m2 - m1)
    w1 = 1.0 / (1.0 + e21)
    gate_ref[...] = jnp.where(lane == j1, w1 * gprob, jnp.where(lane == j2, e21 * w1 * gprob, 0.0))
    tri = tri_ref[...]
    onehot = jnp.where(lane == gidx, 1.0, 0.0)
    ranks = jnp.dot(tri, onehot.astype(BF16), preferred_element_type=F32)
    counts = jnp.sum(onehot, axis=0, keepdims=True)
    blocks = jnp.floor((counts + (MOE_ROW_BLOCK - 0.5)) * (1.0 / MOE_ROW_BLOCK))
    blocks8 = jnp.broadcast_to(blocks, (SUBLANES, LANES)).astype(BF16)
    below = jnp.where(lax.broadcasted_iota(I32, (LANES, LANES), 0) < lax.broadcasted_iota(I32, (LANES, LANES), 1),
                      1.0, 0.0).astype(BF16)
    starts = jnp.dot(blocks8, below, preferred_element_type=F32)[0:1, :] * MOE_ROW_BLOCK
    dst = jnp.sum((ranks + starts) * onehot, axis=1, keepdims=True)
    code_ref[...] = jnp.where(lane == 0, dst, 0.0).T[0:1, :].astype(I32)
    count_ref[...] = starts.astype(I32)


def out_proj_router(attn, rnn, x, g1, sc2, sh2, attn_g, rnn_g, w_out, norm2_g, router_w, router_b, tt):
    n, t, d = x.shape
    c = attn.shape[2]
    per_row = g1.shape[1] != 1
    mod_spec = (pl.BlockSpec((None, tt, d), lambda b, i: (b, i, 0)) if per_row
                else pl.BlockSpec((None, 1, d), lambda b, i: (b, 0, 0)))
    row = lambda w_: pl.BlockSpec((None, tt, w_), lambda b, i: (b, i, 0))
    const = lambda shape: pl.BlockSpec(shape, lambda b, i: (0,) * len(shape))
    nt = t // tt
    per_tile = lambda w_: pl.BlockSpec((None, 1, w_), lambda b, i: (b * nt + i, 0, 0))
    tri = jnp.tril(jnp.ones((tt, tt), BF16), -1)
    return pl.pallas_call(
        _out_router_body,
        grid=(n, nt),
        in_specs=[row(c), row(c), row(d), mod_spec, mod_spec, mod_spec, const((1, c)), const((1, c)),
                  const(w_out.shape), const((1, d)), const(router_w.shape), const((1, LANES)), const((tt, tt))],
        out_specs=[row(d), row(d // 2), row(LANES), per_tile(tt), per_tile(LANES)],
        out_shape=[jax.ShapeDtypeStruct((n, t, d), F32), jax.ShapeDtypeStruct((n, t, d // 2), U32),
                   jax.ShapeDtypeStruct((n, t, LANES), F32), jax.ShapeDtypeStruct((n * nt, 1, tt), I32),
                   jax.ShapeDtypeStruct((n * nt, 1, LANES), I32)],
        compiler_params=_cparams("parallel", "arbitrary"),
        name="out_proj_router",
    )(attn, rnn, x, g1, sc2, sh2, attn_g.reshape(1, c), rnn_g.reshape(1, c), w_out, norm2_g.reshape(1, d),
      router_w, router_b, tri)


def split_bf16(w):
    hi = w.astype(BF16)
    return jnp.stack([hi, (w - hi.astype(F32)).astype(BF16)])


def _moe_body(dst_ref, start_ref, h2_ref, gate_ref, x1_ref, g2_ref, wgu_ref, wd_ref, fg_ref, y_ref,
              hs_ref, hb_ref, gs_ref, acc_ref):
    step = pl.program_id(2)
    tt = h2_ref.shape[0]
    d_exp = wd_ref.shape[1]
    rb = MOE_ROW_BLOCK

    @pl.when(step == 0)
    def _():
        hs_ref[...] = jnp.zeros(hs_ref.shape, U32)
        gs_ref[...] = jnp.zeros(gs_ref.shape, F32)

        def move_in(t, c):
            dst = dst_ref[0, t]
            hs_ref[pl.ds(dst, 1), :] = h2_ref[pl.ds(t, 1), :]
            gs_ref[pl.ds(dst, 1), :] = gate_ref[pl.ds(t, 1), :]
            return c

        lax.fori_loop(0, tt, move_in, 0, unroll=8)
        hb_ref[...] = _unpack_bf16_pairs(hs_ref[...])
        acc_ref[...] = jnp.zeros(acc_ref.shape, F32)

    lane = lax.broadcasted_iota(I32, (rb, LANES), 1)
    for k in range(MOE_EXPERTS_PER_STEP):
        e = step * MOE_EXPERTS_PER_STEP + k
        grp = e // EXPERTS_PER_GROUP

        def block(b, c, k=k, e=e):
            rows = pl.ds(pl.multiple_of(b * rb, rb), rb)
            gcol = jnp.sum(jnp.where(lane == e + EXPERT_LANE0, gs_ref[rows, :], 0.0), axis=1, keepdims=True)
            gu = jnp.dot(hb_ref[rows, :], wgu_ref[k], preferred_element_type=F32)
            gpart, upart = gu[:, :d_exp], gu[:, d_exp:]
            act = (gpart * _sigmoid(gpart) * upart).astype(BF16)
            acc_ref[rows, :] += gcol * jnp.dot(act, wd_ref[k], preferred_element_type=F32)
            return c

        lax.fori_loop(start_ref[0, grp] // rb, start_ref[0, grp + 1] // rb, block, 0)

    @pl.when(step == pl.num_programs(2) - 1)
    def _():
        def move_out(t, c):
            y_ref[pl.ds(t, 1), :] = acc_ref[pl.ds(dst_ref[0, t], 1), :]
            return c

        lax.fori_loop(0, tt, move_out, 0, unroll=8)
        x2 = x1_ref[...] + g2_ref[...] * y_ref[...]
        y_ref[...] = _rms(x2, fg_ref[...])


def moe_final(codes, counts, h2, gates, x1, g2, w_gu, w_down, final_g, tt):
    n, t, d = x1.shape
    n_exp = w_gu.shape[0]
    nt = t // tt
    per_row = g2.shape[1] != 1
    mod_spec = (pl.BlockSpec((None, tt, d), lambda b, i, e: (b, i, 0)) if per_row
                else pl.BlockSpec((None, 1, d), lambda b, i, e: (b, 0, 0)))
    row = lambda w_: pl.BlockSpec((None, tt, w_), lambda b, i, e: (b, i, 0))
    smem_tile = lambda w_: pl.BlockSpec((None, 1, w_), lambda b, i, e: (b * nt + i, 0, 0), memory_space=pltpu.SMEM)
    sort_rows = tt + N_GROUPS * MOE_ROW_BLOCK
    eps = MOE_EXPERTS_PER_STEP
    return pl.pallas_call(
        _moe_body,
        grid=(n, nt, n_exp // eps),
        in_specs=[smem_tile(tt), smem_tile(LANES), row(d // 2), row(LANES), row(d), mod_spec,
                  pl.BlockSpec((eps,) + w_gu.shape[1:], lambda b, i, e: (e, 0, 0)),
                  pl.BlockSpec((eps,) + w_down.shape[1:], lambda b, i, e: (e, 0, 0)),
                  pl.BlockSpec((1, d), lambda b, i, e: (0, 0))],
        out_specs=row(d),
        out_shape=jax.ShapeDtypeStruct((n, t, d), F32),
        scratch_shapes=[pltpu.VMEM((sort_rows, d // 2), U32),
                        pltpu.VMEM((sort_rows, d), BF16),
                        pltpu.VMEM((sort_rows, LANES), F32),
                        pltpu.VMEM((sort_rows, d), F32)],
        compiler_params=_cparams("parallel", "arbitrary", "arbitrary"),
        name="moe_final",
    )(codes, counts, h2, gates, x1, g2, w_gu, w_down, final_g.reshape(1, d))


def _tile(t, pref):
    return pref if t % pref == 0 else t


def _layer(x, mods, pos, attend, conv_prev, h_prev, n_seq, lw, final_g):
    sh1, sc1, g1, sh2, sc2, g2 = mods
    n, t, d = x.shape
    t_seq = n * t // n_seq
    seqs = lambda a: a.reshape(n_seq, t_seq, a.shape[-1])
    q, k, v, kb, vt, iq, ikw, ikb, xr, yg = in_proj(x, sc1, sh1, lw["norm1_g"], lw["w_in"], rope_tables(pos),
                                                     _tile(t, 512))
    attn = attend(q, iq, ikw, k, v, kb, vt, ikb)
    rnn, h_new = rglru(seqs(xr), seqs(yg), conv_prev, h_prev, lw["conv_w"], lw["conv_b"], lw["rg_wa"], lw["rg_ba"],
                       lw["rg_wx"], lw["rg_bx"], lw["rg_lambda"], _tile(t_seq, 512))
    moe_tile = _tile(t, 1024)
    x1, h2, gates, codes, counts = out_proj_router(
        attn, rnn.reshape(xr.shape), x, g1, sc2, sh2, lw["attn_out_g"], lw["rnn_out_g"], lw["w_out"], lw["norm2_g"],
        lw["router_w"], lw["router_b"], moe_tile)
    y = moe_final(codes, counts, h2, gates, x1, g2, lw["w_gu"], lw["w_down"], final_g, moe_tile)
    conv_new = seqs(xr)[:, t_seq - (CONV_WIDTH - 1):, :]
    return (seqs(y), seqs(k).reshape(n_seq, t_seq, N_KV_HEADS, HEAD_DIM),
            seqs(v).reshape(n_seq, t_seq, N_KV_HEADS, HEAD_DIM), seqs(ikw)[:, :, :IDX_HEAD_DIM], conv_new,
            h_new[:, 0, :])


def kernel(x_prompt, x_sample, cache_k, cache_v, cache_idx_k, state_conv, state_rglru, page_table, c_prompt,
           c_sample, ada_w, ada_b, norm1_g, w_in, conv_w, conv_b, rg_wa, rg_ba, rg_wx, rg_bx, rg_lambda, attn_out_g,
           rnn_out_g, w_out, norm2_g, router_grp_w, router_grp_b, router_exp_w, router_exp_b, exp_w_gate, exp_w_up,
           exp_w_down, final_g):
    depth = ada_w.shape[0]
    assert depth == 1, "this implementation handles the single-layer configuration"
    n_p, t_p, d = x_prompt.shape
    n_s, t_s, _ = x_sample.shape
    n_pages = page_table.shape[1]
    past = n_pages * PAGE_SIZE
    l = 0
    router_w = jnp.concatenate([router_grp_w[l], router_exp_w[l],
                                jnp.zeros((d, LANES - N_GROUPS - N_EXPERTS), F32)], axis=1)
    router_b = jnp.concatenate([router_grp_b[l], router_exp_b[l],
                                jnp.zeros((LANES - N_GROUPS - N_EXPERTS,), F32)]).reshape(1, LANES)
    lw = dict(norm1_g=norm1_g[l], w_in=prep_w_in(w_in[l]), conv_w=conv_w[l], conv_b=conv_b[l], rg_wa=rg_wa[l],
              rg_ba=rg_ba[l], rg_wx=rg_wx[l], rg_bx=rg_bx[l], rg_lambda=rg_lambda[l], attn_out_g=attn_out_g[l],
              rnn_out_g=rnn_out_g[l], w_out=w_out[l].astype(BF16), norm2_g=norm2_g[l],
              router_w=split_bf16(router_w), router_b=router_b,
              w_gu=jnp.concatenate([exp_w_gate[l], exp_w_up[l]], axis=2).astype(BF16),
              w_down=exp_w_down[l].astype(BF16))

    mod = ada_mod(jnp.concatenate([c_prompt, c_sample], axis=0), ada_w[l], ada_b[l])
    mods_p = [m[:n_p, None, :] for m in jnp.split(mod, 6, axis=1)]
    mods_s = [jnp.repeat(m[n_p:], t_s, axis=0)[None] for m in jnp.split(mod, 6, axis=1)]

    def attend_prompt(q, iq, ikw, k, v, kb, vt, ikb):
        iw_t = jnp.swapaxes(ikw[:, :, IDX_HEAD_DIM:IDX_HEAD_DIM + N_IDX_HEADS], 1, 2)
        return prompt_attention(q, iq, iw_t, kb, vt, ikb, _tile(t_p, 256))

    def attend_sample(q, iq, ikw, k, v, kb, vt, ikb):
        seqs = lambda a: a.reshape(n_s, t_s, a.shape[-1])
        n_pool = cache_k.shape[1]
        feat_major = lambda c: jnp.swapaxes(c.reshape(n_pool, PAGE_SIZE, -1), 1, 2)
        out = sample_attention(page_table, seqs(q), seqs(iq), seqs(ikw), seqs(k), seqs(v),
                               feat_major(cache_k[l]), feat_major(cache_v[l]), feat_major(cache_idx_k[l]))
        return out.reshape(1, n_s * t_s, N_HEADS * HEAD_DIM)

    c_rnn = conv_w.shape[2]
    ys, ks, vs, iks, cvs, hs = _layer(
        x_sample.reshape(1, n_s * t_s, d), mods_s, jnp.tile(past + jnp.arange(t_s), n_s), attend_sample,
        state_conv[l], state_rglru[l], n_s, lw, final_g)
    yp, kp, vp, ikp, cvp, hp = _layer(
        x_prompt, mods_p, jnp.arange(t_p), attend_prompt,
        jnp.zeros((n_p, CONV_WIDTH - 1, c_rnn), F32), jnp.zeros((n_p, c_rnn), F32), n_p, lw, final_g)

    st = lambda a: a[None]
    return (yp, ys, st(kp), st(vp), st(ikp), st(cvp), st(hp), st(ks), st(vs), st(iks), st(cvs), st(hs))
```

```python
import functools

import jax
import jax.numpy as jnp
from jax import lax
from jax.experimental import pallas as pl
from jax.experimental.pallas import tpu as pltpu

F32, BF16, I32, I16, U32 = jnp.float32, jnp.bfloat16, jnp.int32, jnp.int16, jnp.uint32
HIGHEST = lax.Precision.HIGHEST

HEAD_DIM = 64
N_HEADS = 8
N_KV_HEADS = 2
HEADS_PER_KV = N_HEADS // N_KV_HEADS
N_IDX_HEADS = 8
IDX_HEAD_DIM = 32
TOPK_MAX = 256
ROPE_THETA = 10000.0
CONV_WIDTH = 4
RG_BLOCKS = 8
RG_C = 8.0
N_GROUPS = 4
EXPERTS_PER_GROUP = 8
N_EXPERTS = N_GROUPS * EXPERTS_PER_GROUP
PAGE_SIZE = 128
NORM_EPS = 1e-6

LANES = 128
SUBLANES = 8
VMEM_LIMIT = 48 * 1024 * 1024
INT_MIN = -(2 ** 31)
I16_MIN = -(2 ** 15)
NEG_BIG = -1e30
EXPERT_LANE0 = N_GROUPS
BF16_SUBLANES = 16
V_AUG_ROWS = HEAD_DIM + BF16_SUBLANES
ATT_CHUNK_HEADS = 4
COUNT_TILES = 2
MOE_ROW_BLOCK = 304
MOE_EXPERTS_PER_STEP = 2
LOG2_E = 1.4426950408889634
Q_SCALE = HEAD_DIM ** -0.5 * LOG2_E


def _cparams(*sem):
    return pltpu.CompilerParams(dimension_semantics=sem, vmem_limit_bytes=VMEM_LIMIT)


def _rms(x, g):
    return x * lax.rsqrt(jnp.mean(x * x, axis=-1, keepdims=True) + NORM_EPS) * g


def _sigmoid(x):
    return 1.0 / (1.0 + jnp.exp(-x))


def _float_key(s):
    b = pltpu.bitcast(s + 0.0, I32)
    return b ^ ((b >> 31) & 0x7FFFFFFF)


def _ada_body(c_ref, w_ref, b_ref, o_ref):
    c = c_ref[...]
    s = c * _sigmoid(c)
    o_ref[...] = jnp.dot(s, w_ref[...], preferred_element_type=F32, precision=HIGHEST) + b_ref[...]


def ada_mod(c, w, b):
    m, d = c.shape
    n_chunks = w.shape[1] // d
    return pl.pallas_call(
        _ada_body,
        grid=(n_chunks,),
        in_specs=[pl.BlockSpec((m, d), lambda j: (0, 0)),
                  pl.BlockSpec((d, d), lambda j: (0, j)),
                  pl.BlockSpec((1, d), lambda j: (0, j))],
        out_specs=pl.BlockSpec((m, d), lambda j: (0, j)),
        out_shape=jax.ShapeDtypeStruct((m, w.shape[1]), F32),
        compiler_params=_cparams("arbitrary"),
        name="ada_mod",
    )(c, w, b.reshape(1, -1))


_C_Q, _C_K, _C_V, _C_IQ, _C_IKW, _C_XR, _C_YG, _C_END = (0, 512, 640, 768, 1024, 1152, 1664, 2176)


def prep_w_in(w_in):
    d_in = w_in.shape[0]
    q, k, v, iq, ik, iw, xr, yg = jnp.split(w_in, [512, 640, 768, 1024, 1056, 1064, 1576], axis=1)
    ikw = jnp.concatenate([ik, iw, jnp.zeros((d_in, LANES - IDX_HEAD_DIM - N_IDX_HEADS), w_in.dtype)], axis=1)
    return jnp.concatenate([q, k, v, iq, ikw, xr, yg], axis=1).astype(BF16)


def rope_tables(pos):
    pos = pos.astype(F32)[:, None]
    lane = jnp.arange(LANES)[None, :]

    def cs(head_dim):
        half = head_dim // 2
        inv = ROPE_THETA ** (-jnp.arange(half, dtype=F32) / half)
        ang = pos * inv[None, :]
        reps = LANES // head_dim
        cos, sin = jnp.tile(jnp.cos(ang), (1, 2 * reps)), jnp.tile(jnp.sin(ang), (1, 2 * reps))
        return cos, jnp.where(lane % head_dim < half, -sin, sin)

    cq, sq = cs(HEAD_DIM)
    ci, si = cs(IDX_HEAD_DIM)
    w_scale = (IDX_HEAD_DIM ** -0.5) * (N_IDX_HEADS ** -0.5)
    cx = jnp.where(lane < IDX_HEAD_DIM, ci, jnp.where(lane < IDX_HEAD_DIM + N_IDX_HEADS, w_scale, 0.0))
    sx = jnp.where(lane < IDX_HEAD_DIM, si, 0.0)
    return cq, sq, ci, si, cx.astype(F32), sx.astype(F32)


def _rope(x, cos, sin_signed, head_dim):
    half = head_dim // 2
    out = []
    for c in range(x.shape[1] // LANES):
        xs = x[:, c * LANES:(c + 1) * LANES]
        lane = lax.broadcasted_iota(I32, xs.shape, 1)
        partner = jnp.where(lane % head_dim < half, pltpu.roll(xs, LANES - half, axis=1), pltpu.roll(xs, half, axis=1))
        out.append(xs * cos + partner * sin_signed)
    return out[0] if len(out) == 1 else jnp.concatenate(out, axis=1)


def _in_proj_body(x_ref, sc_ref, sh_ref, g_ref, w_ref, cq_ref, sq_ref, ci_ref, si_ref, cx_ref, sx_ref,
                  q_ref, k_ref, v_ref, kb_ref, vt_ref, iq_ref, ikw_ref, ikb_ref, xr_ref, yg_ref):
    x = x_ref[...]
    h = (_rms(x, g_ref[...]) * (1.0 + sc_ref[...]) + sh_ref[...]).astype(BF16)

    def seg(a, b):
        return jnp.dot(h, w_ref[:, a:b], preferred_element_type=F32)

    cq, sq = cq_ref[...], sq_ref[...]
    q_ref[...] = (_rope(seg(_C_Q, _C_K), cq, sq, HEAD_DIM) * Q_SCALE).astype(BF16)
    k = _rope(seg(_C_K, _C_V), cq, sq, HEAD_DIM)
    k_ref[...] = k
    kb_ref[...] = k.astype(BF16)
    v = seg(_C_V, _C_IQ)
    v_ref[...] = v
    vt_ref[...] = v.T.astype(BF16)
    iq_ref[...] = _rope(seg(_C_IQ, _C_IKW), ci_ref[...], si_ref[...], IDX_HEAD_DIM).astype(BF16)
    ikw = _rope(seg(_C_IKW, _C_XR), cx_ref[...], sx_ref[...], IDX_HEAD_DIM)
    ikw_ref[...] = ikw
    ikb_ref[...] = ikw.astype(BF16)
    xr_ref[...] = seg(_C_XR, _C_YG)
    yg_ref[...] = seg(_C_YG, _C_END)


def in_proj(x, sc, sh, g, w, tables, tt):
    n, t, d = x.shape
    per_row = sc.shape[1] != 1
    mod_spec = (pl.BlockSpec((None, tt, d), lambda b, i: (b, i, 0)) if per_row
                else pl.BlockSpec((None, 1, d), lambda b, i: (b, 0, 0)))
    row = lambda w_: pl.BlockSpec((None, tt, w_), lambda b, i: (b, i, 0))
    tab = pl.BlockSpec((tt, LANES), lambda b, i: (i, 0))
    outs = [(512, BF16), (128, F32), (128, F32), (128, BF16), None, (256, BF16), (128, F32), (128, BF16),
            (512, F32), (512, F32)]
    out_shape, out_specs = [], []
    for o in outs:
        if o is None:
            out_shape.append(jax.ShapeDtypeStruct((n, LANES, t), BF16))
            out_specs.append(pl.BlockSpec((None, LANES, tt), lambda b, i: (b, 0, i)))
        else:
            out_shape.append(jax.ShapeDtypeStruct((n, t, o[0]), o[1]))
            out_specs.append(row(o[0]))
    return pl.pallas_call(
        _in_proj_body,
        grid=(n, t // tt),
        in_specs=[row(d), mod_spec, mod_spec,
                  pl.BlockSpec((1, d), lambda b, i: (0, 0)),
                  pl.BlockSpec(w.shape, lambda b, i: (0, 0))] + [tab] * 6,
        out_specs=out_specs,
        out_shape=out_shape,
        compiler_params=_cparams("parallel", "arbitrary"),
        name="in_proj",
    )(x, sc, sh, g.reshape(1, d), w, *tables)


def _pattn_body(q_ref, iq_ref, iwt_ref, k_ref, vt_ref, ik_ref, o_ref,
                hi_ref, lo_ref, lb_ref, p_ref, m_ref, acc_ref, mm_ref, *, n_sel, idx_bits):
    qi = pl.program_id(1)
    tq = q_ref.shape[0]
    tk = tq
    n_tiles = qi + 1

    iq_t = iq_ref[...].astype(F32).T
    iq_st = jnp.concatenate([iq_t[IDX_HEAD_DIM * h:IDX_HEAD_DIM * (h + 1)] for h in range(N_IDX_HEADS)],
                            axis=1).astype(BF16)
    w_t = iwt_ref[...]
    q_t = q_ref[...].astype(F32).T
    zero_half = jnp.zeros((HEAD_DIM, tq), F32)
    q_pad = []
    for g in range(N_KV_HEADS):
        cols = []
        for hh in range(HEADS_PER_KV):
            h = g * HEADS_PER_KV + hh
            qh = q_t[HEAD_DIM * h:HEAD_DIM * (h + 1)]
            cols.append(jnp.concatenate([qh, zero_half] if g == 0 else [zero_half, qh], axis=0))
        q_pad.append(jnp.concatenate(cols, axis=1).astype(BF16))

    row_s = lax.broadcasted_iota(I32, (tk, tq), 0)
    col_t = lax.broadcasted_iota(I32, (tk, tq), 1)
    causal = row_s <= col_t

    def tile_off(j):
        return pl.multiple_of(j * tk, tk)

    def walk_tiles(step):
        def pair(p, c):
            step(2 * p, 0, False)
            step(2 * p + 1, 1, False)
            return c

        lax.fori_loop(0, qi // 2, pair, 0)

        @pl.when(qi % 2 == 0)
        def _():
            step(qi, 0, True)

        @pl.when(qi % 2 == 1)
        def _():
            step(qi - 1, 0, False)
            step(qi, 1, True)

    def index_dots(j, slot):
        ik = ik_ref[pl.ds(tile_off(j), tk), :][:, :IDX_HEAD_DIM]
        mm_ref[slot] = jnp.dot(ik, iq_st, preferred_element_type=F32)

    def p1(j, slot, last):
        if not last:
            index_dots(j + 1, 1 - slot)
        sc = w_t[0:1, :] * jnp.maximum(mm_ref[slot, :, 0:tq], 0.0)
        for h in range(1, N_IDX_HEADS):
            sc = sc + w_t[h:h + 1, :] * jnp.maximum(mm_ref[slot, :, h * tq:(h + 1) * tq], 0.0)
        key = _float_key(sc)
        if last:
            key = jnp.where(causal, key, INT_MIN)
        rows = pl.ds(tile_off(j), tk)
        hi_ref[rows, :] = (key >> 16).astype(I16)
        lo_ref[rows, :] = ((key & 0xFFFF) + I16_MIN).astype(I16)

    index_dots(0, 0)
    walk_tiles(p1)

    span = COUNT_TILES * tk
    n_spans = (n_tiles + COUNT_TILES - 1) // COUNT_TILES
    groups = span // BF16_SUBLANES
    pad_rows = pl.ds(tile_off(n_tiles), span - tk)
    hi_ref[pad_rows, :] = jnp.full((span - tk, tq), I16_MIN, I16)
    lo_ref[pad_rows, :] = jnp.full((span - tk, tq), I16_MIN, I16)

    def rows16(v):
        return jnp.broadcast_to(v.astype(I16), (BF16_SUBLANES, tq))[None]

    def span_rows(j):
        return pl.ds(pl.multiple_of(j * span, span), span)

    def tile16(ref, j):
        return ref[span_rows(j), :].reshape(groups, BF16_SUBLANES, tq)

    one16, zero16 = jnp.int16(1), jnp.int16(0)

    def count(pred):
        def body(j, acc):
            marks = pred(j)
            parts = [marks[g] for g in range(groups)]
            while len(parts) > 1:
                parts = [parts[i] + parts[i + 1] for i in range(0, len(parts), 2)]
            return acc + parts[0]

        acc = lax.fori_loop(0, n_spans, body, jnp.zeros((BF16_SUBLANES, tq), I16))
        return jnp.sum(acc.astype(I32).astype(F32), axis=0, keepdims=True)

    def bisect16(ref, need):
        def step(b, thr):
            cand = thr + jnp.left_shift(jnp.int32(1), 15 - b)
            cand_b = rows16(cand)
            c = count(lambda j: jnp.where(tile16(ref, j) >= cand_b, one16, zero16))
            return jnp.where(c >= need, cand, thr)
        return lax.fori_loop(0, 16, step, jnp.full((1, tq), I16_MIN, I32))

    thr_hi = bisect16(hi_ref, n_sel)
    hi_b = rows16(thr_hi)
    c_above = count(lambda j: jnp.where(tile16(hi_ref, j) > hi_b, one16, zero16))

    def p2(j, c):
        rows = span_rows(j)
        lb_ref[rows, :] = jnp.where(hi_ref[rows, :] == thr_hi.astype(I16), lo_ref[rows, :], jnp.int16(I16_MIN))
        return c

    lax.fori_loop(0, n_spans, p2, 0)
    thr_lo = bisect16(lb_ref, n_sel - c_above)
    lo_b = rows16(thr_lo)
    c_gt = c_above + count(lambda j: jnp.where(tile16(lb_ref, j) > lo_b, one16, zero16))
    ties_left = n_sel - c_gt

    def is_tie(j):
        return jnp.where(tile16(hi_ref, j) == hi_b, jnp.where(tile16(lo_ref, j) == lo_b, one16, zero16), zero16)

    c_tie = count(is_tie)
    p_ref[...] = jnp.full((1, tq), 2 ** idx_bits - 1, I32)
    has_thr = jnp.where(thr_hi > I16_MIN, 1, jnp.where(thr_lo > I16_MIN, 1, 0))
    excess = jnp.max(jnp.where(has_thr > 0, c_tie - ties_left, 0.0)) > 0.0

    idx16 = row_s.astype(I16)

    @pl.when(excess)
    def _():
        sub = lax.broadcasted_iota(I32, (span, tq), 0).astype(I16).reshape(groups, BF16_SUBLANES, tq)

        def tb(b, p):
            cand = p | jnp.left_shift(jnp.int32(1), idx_bits - 1 - b)
            c = count(lambda j: jnp.where(sub < rows16(cand - j * span), is_tie(j), zero16))
            return jnp.where(c < ties_left, cand, p)

        p_ref[...] = lax.fori_loop(0, idx_bits, tb, jnp.zeros((1, tq), I32))

    p_last = p_ref[...]
    thr_hi16, thr_lo16 = thr_hi.astype(I16), thr_lo.astype(I16)

    m_ref[...] = jnp.full(m_ref.shape, NEG_BIG, F32)
    acc_ref[...] = jnp.zeros(acc_ref.shape, F32)
    ones_rows = jnp.ones((V_AUG_ROWS - HEAD_DIM, tk), BF16)

    w_c = ATT_CHUNK_HEADS * tq
    chunks = [(g, c * w_c) for g in range(N_KV_HEADS) for c in range(HEADS_PER_KV // ATT_CHUNK_HEADS)]

    def score_chunk(kt, slot, ci):
        g, c0 = chunks[ci]
        mm_ref[slot, :, ci * w_c:(ci + 1) * w_c] = jnp.dot(kt, q_pad[g][:, c0:c0 + w_c],
                                                             preferred_element_type=F32)

    kt0 = k_ref[pl.ds(0, tk), :]
    for ci in range(len(chunks)):
        score_chunk(kt0, 0, ci)

    def attend(j, slot, diag):
        off = tile_off(j)
        hi, lo = hi_ref[pl.ds(off, tk), :], lo_ref[pl.ds(off, tk), :]
        keep_tie = jnp.where(idx16 <= (p_last - j * tk).astype(I16), one16, zero16)
        sel = jnp.where(hi > thr_hi16, one16,
                        jnp.where(hi == thr_hi16,
                                  jnp.where(lo > thr_lo16, one16, jnp.where(lo == thr_lo16, keep_tie, zero16)),
                                  zero16))
        bias = jnp.where(sel.astype(I32) > 0, 0.0, NEG_BIG)
        if diag:
            bias = jnp.where(causal, bias, NEG_BIG)
        vt = vt_ref[:, pl.ds(off, tk)]
        v_aug = [jnp.concatenate([vt[HEAD_DIM * g:HEAD_DIM * (g + 1)], ones_rows], axis=0)
                 for g in range(N_KV_HEADS)]
        bias_c = jnp.concatenate([bias] * ATT_CHUNK_HEADS, axis=1)
        if not diag:
            kt_next = k_ref[pl.ds(tile_off(j + 1), tk), :]
        for ci, (g, c0) in enumerate(chunks):
            if not diag:
                score_chunk(kt_next, 1 - slot, ci)
            cols = slice(c0, c0 + w_c)
            s = mm_ref[slot, :, ci * w_c:(ci + 1) * w_c] + bias_c
            m_old = m_ref[g:g + 1, cols]
            m_new = jnp.maximum(m_old, jnp.max(s, axis=0, keepdims=True))
            alpha = jnp.exp2(m_old - m_new)
            p = jnp.exp2(s - m_new).astype(BF16)
            acc_ref[g, :, cols] = alpha * acc_ref[g, :, cols] + jnp.dot(v_aug[g], p, preferred_element_type=F32)
            m_ref[g:g + 1, cols] = m_new

    walk_tiles(attend)

    rows = []
    for g in range(N_KV_HEADS):
        acc = acc_ref[g]
        o_g = acc[:HEAD_DIM] / acc[HEAD_DIM:HEAD_DIM + 1]
        rows += [o_g[:, hh * tq:(hh + 1) * tq] for hh in range(HEADS_PER_KV)]
    o_ref[...] = jnp.concatenate(rows, axis=0).T


def prompt_attention(q, iq, iw_t, kb, vt, ikb, tq):
    n, t, _ = q.shape
    n_sel = min(TOPK_MAX, t // 4)
    idx_bits = max(1, (t - 1).bit_length())
    body = functools.partial(_pattn_body, n_sel=n_sel, idx_bits=idx_bits)
    return pl.pallas_call(
        body,
        grid=(n, t // tq),
        in_specs=[pl.BlockSpec((None, tq, q.shape[2]), lambda b, i: (b, i, 0)),
                  pl.BlockSpec((None, tq, iq.shape[2]), lambda b, i: (b, i, 0)),
                  pl.BlockSpec((None, N_IDX_HEADS, tq), lambda b, i: (b, 0, i)),
                  pl.BlockSpec((None, t, LANES), lambda b, i: (b, 0, 0)),
                  pl.BlockSpec((None, LANES, t), lambda b, i: (b, 0, 0)),
                  pl.BlockSpec((None, t, LANES), lambda b, i: (b, 0, 0))],
        out_specs=pl.BlockSpec((None, tq, N_HEADS * HEAD_DIM), lambda b, i: (b, i, 0)),
        out_shape=jax.ShapeDtypeStruct((n, t, N_HEADS * HEAD_DIM), F32),
        scratch_shapes=[pltpu.VMEM((t + (COUNT_TILES - 1) * tq, tq), I16),
                        pltpu.VMEM((t + (COUNT_TILES - 1) * tq, tq), I16),
                        pltpu.VMEM((t + (COUNT_TILES - 1) * tq, tq), I16),
                        pltpu.VMEM((1, tq), I32),
                        pltpu.VMEM((N_KV_HEADS, HEADS_PER_KV * tq), F32),
                        pltpu.VMEM((N_KV_HEADS, V_AUG_ROWS, HEADS_PER_KV * tq), F32),
                        pltpu.VMEM((2, tq, N_HEADS * tq), F32)],
        compiler_params=_cparams("parallel", "arbitrary"),
        name="prompt_attention",
    )(q, iq, iw_t, kb, vt, ikb)


Q_PAD = SUBLANES


def _sattn_body(pt_ref, q_ref, iq_ref, iw_ref, kn_ref, vn_ref, ikn_ref, ck_hbm, cv_hbm, cik_hbm, o_ref,
                kbuf, vbuf, ikbuf, s_ref, p_ref, sem, *, n_pages, n_sel, idx_bits):
    b = pl.program_id(0)
    slot = b % 2
    past = n_pages * PAGE_SIZE
    lk = past + LANES

    streams = ((cik_hbm, ikbuf), (ck_hbm, kbuf), (cv_hbm, vbuf))

    def page_copy(kind, p, seq, slt):
        src, dst = streams[kind]
        return pltpu.make_async_copy(src.at[pt_ref[seq, p]], dst.at[slt, :, pl.ds(p * PAGE_SIZE, PAGE_SIZE)],
                                     sem.at[slt, kind])

    def start_pages(seq, slt):
        for kind in range(len(streams)):
            for p in range(n_pages):
                page_copy(kind, p, seq, slt).start()

    def wait_pages(kind):
        for p in range(n_pages):
            page_copy(kind, p, b, slot).wait()

    @pl.when(b == 0)
    def _():
        start_pages(0, 0)

    @pl.when(b + 1 < pl.num_programs(0))
    def _():
        start_pages(b + 1, 1 - slot)

    def new_cols(ref):
        rows = jnp.concatenate([ref[...], jnp.zeros((LANES - Q_PAD, LANES), F32)], axis=0)
        return rows.T

    new_tile = pl.ds(past, LANES)
    kbuf[slot, :, new_tile] = new_cols(kn_ref)
    vbuf[slot, :, new_tile] = new_cols(vn_ref)
    ikbuf[slot, :, new_tile] = new_cols(ikn_ref)[:IDX_HEAD_DIM]

    iq = iq_ref[...].astype(F32)
    iq_st = jnp.concatenate([iq[:, IDX_HEAD_DIM * h:IDX_HEAD_DIM * (h + 1)] for h in range(N_IDX_HEADS)],
                            axis=0).astype(BF16)
    qf = q_ref[...].astype(F32)
    lane128 = lax.broadcasted_iota(I32, (Q_PAD, LANES), 1)
    q_rows = []
    for h in range(N_HEADS):
        g = h // HEADS_PER_KV
        slab = qf[:, LANES * (h // 2):LANES * (h // 2 + 1)]
        if (h % 2) != g:
            slab = pltpu.roll(slab, HEAD_DIM, axis=1)
        keep = (lane128 < HEAD_DIM) if g == 0 else (lane128 >= HEAD_DIM)
        q_rows.append(jnp.where(keep, slab, 0.0))
    q_pad = jnp.concatenate(q_rows, axis=0).astype(BF16)
    iw = iw_ref[...]

    key_pos = lax.broadcasted_iota(I32, (Q_PAD, lk), 1)
    q_row = lax.broadcasted_iota(I32, (Q_PAD, lk), 0)
    admissible = key_pos <= past + q_row

    wait_pages(0)
    d = jnp.dot(iq_st, ikbuf[slot].astype(BF16), preferred_element_type=F32)
    sc = None
    for h in range(N_IDX_HEADS):
        w_h = iw[:, IDX_HEAD_DIM + h:IDX_HEAD_DIM + h + 1]
        term = w_h * jnp.maximum(d[Q_PAD * h:Q_PAD * (h + 1)], 0.0)
        sc = term if sc is None else sc + term
    s_ref[...] = jnp.where(admissible, _float_key(sc), INT_MIN)

    lane_pos = lax.broadcasted_iota(I32, (Q_PAD, LANES), 1)

    def count(pred):
        terms = [pred(s_ref[:, c * LANES:(c + 1) * LANES], c * LANES) for c in range(lk // LANES)]
        while len(terms) > 1:
            terms = [terms[i] + terms[i + 1] if i + 1 < len(terms) else terms[i] for i in range(0, len(terms), 2)]
        return jnp.sum(terms[0], axis=1, keepdims=True)

    def counts3(c1, c2, c3):
        terms = [[], [], []]
        for c in range(lk // LANES):
            t = s_ref[:, c * LANES:(c + 1) * LANES]
            for k, cand in enumerate((c1, c2, c3)):
                terms[k].append(jnp.where(t >= cand, 1.0, 0.0))
        sums = []
        for ts in terms:
            while len(ts) > 1:
                ts = [ts[i] + ts[i + 1] if i + 1 < len(ts) else ts[i] for i in range(0, len(ts), 2)]
            sums.append(jnp.sum(ts[0], axis=1, keepdims=True))
        return sums

    def quaternary(i, thr):
        q = jnp.left_shift(jnp.int32(1), 30 - 2 * i)
        cands = [jnp.broadcast_to(thr + k * q, (Q_PAD, LANES)) for k in (1, 2, 3)]
        n1, n2, n3 = counts3(*cands)
        passed = (jnp.where(n1 >= n_sel, 1, 0) + jnp.where(n2 >= n_sel, 1, 0) + jnp.where(n3 >= n_sel, 1, 0))
        return thr + passed * q

    thr = lax.fori_loop(0, 16, quaternary, jnp.full((Q_PAD, 1), INT_MIN, I32))
    thr_b = jnp.broadcast_to(thr, (Q_PAD, LANES))
    ties_left = n_sel - count(lambda t, c0: jnp.where(t > thr_b, 1.0, 0.0))
    c_tie = count(lambda t, c0: jnp.where(t == thr_b, 1.0, 0.0))
    p_ref[...] = jnp.full((Q_PAD, 1), 2 ** idx_bits - 1, I32)

    @pl.when(jnp.max(jnp.where(thr > INT_MIN, c_tie - ties_left, 0.0)) > 0.0)
    def _():
        def tb(i, pidx):
            cand = jnp.broadcast_to(pidx | jnp.left_shift(jnp.int32(1), idx_bits - 1 - i), (Q_PAD, LANES))
            c = count(lambda t, c0: jnp.where(t == thr_b, jnp.where(lane_pos + c0 < cand, 1.0, 0.0), 0.0))
            return jnp.where(c < ties_left, cand[:, :1], pidx)

        p_ref[...] = lax.fori_loop(0, idx_bits, tb, jnp.zeros((Q_PAD, 1), I32))

    p_last = p_ref[...]
    t = s_ref[...]
    keep_tie = jnp.where(key_pos <= p_last, 0.0, NEG_BIG)
    bias = jnp.where(t > thr, 0.0, jnp.where(t == thr, keep_tie, NEG_BIG))
    bias = jnp.where(admissible, bias, NEG_BIG)

    wait_pages(1)
    s = jnp.dot(q_pad, kbuf[slot].astype(BF16), preferred_element_type=F32)
    s = s + jnp.concatenate([bias] * N_HEADS, axis=0)
    m = jnp.max(s, axis=1, keepdims=True)
    pexp = jnp.exp2(s - m)
    l = jnp.sum(pexp, axis=1, keepdims=True)
    wait_pages(2)
    o = lax.dot_general(pexp.astype(BF16), vbuf[slot].astype(BF16), (((1,), (1,)), ((), ())),
                        preferred_element_type=F32)
    o_ref[...] = o / l


def sample_attention(page_table, q, iq, ikw, k_new, v_new, cache_k, cache_v, cache_ik):
    n, n_new, _ = q.shape
    assert n_new <= Q_PAD
    n_pages = page_table.shape[1]
    past = n_pages * PAGE_SIZE
    lk = past + LANES
    n_sel = min(TOPK_MAX, (past + n_new) // 4)
    idx_bits = max(1, (lk - 1).bit_length())
    body = functools.partial(_sattn_body, n_pages=n_pages, n_sel=n_sel, idx_bits=idx_bits)
    pad = lambda a: jnp.pad(a, ((0, 0), (0, Q_PAD - n_new), (0, 0)))
    q, iq, ikw, k_new, v_new = pad(q), pad(iq), pad(ikw), pad(k_new), pad(v_new)
    seq = lambda w_: pl.BlockSpec((None, Q_PAD, w_), lambda b, pt: (b, 0, 0))
    hbm = pl.BlockSpec(memory_space=pl.ANY)
    grid_spec = pltpu.PrefetchScalarGridSpec(
        num_scalar_prefetch=1,
        grid=(n,),
        in_specs=[seq(q.shape[2]), seq(iq.shape[2]), seq(LANES), seq(LANES), seq(LANES), seq(LANES), hbm, hbm, hbm],
        out_specs=pl.BlockSpec((None, N_HEADS * Q_PAD, LANES), lambda b, pt: (b, 0, 0)),
        scratch_shapes=[pltpu.VMEM((2, LANES, lk), F32),
                        pltpu.VMEM((2, LANES, lk), F32),
                        pltpu.VMEM((2, IDX_HEAD_DIM, lk), F32),
                        pltpu.VMEM((Q_PAD, lk), I32),
                        pltpu.VMEM((Q_PAD, 1), I32),
                        pltpu.SemaphoreType.DMA((2, 3))],
    )
    raw = pl.pallas_call(
        body,
        grid_spec=grid_spec,
        out_shape=jax.ShapeDtypeStruct((n, N_HEADS * Q_PAD, LANES), F32),
        compiler_params=_cparams("arbitrary"),
        name="sample_attention",
    )(page_table, q, iq, ikw, k_new, v_new, ikw, cache_k, cache_v, cache_ik)
    raw = raw.reshape(n, N_HEADS, Q_PAD, N_KV_HEADS, HEAD_DIM)[:, :, :n_new]
    per_head = jnp.stack([raw[:, h, :, h // HEADS_PER_KV] for h in range(N_HEADS)], axis=2)
    return per_head.reshape(n, n_new, N_HEADS * HEAD_DIM)


def _rglru_body(xr_ref, yg_ref, cprev_ref, hprev_ref, cw_ref, cb_ref, wa_ref, ba_ref, wx_ref, bx_ref, lam_ref,
                y_ref, hlast_ref, xbuf, a_buf, b_buf, h_buf, h_carry):
    i = pl.program_id(1)
    tt = xr_ref.shape[0]
    halo = SUBLANES

    @pl.when(i == 0)
    def _():
        xbuf[pl.ds(0, halo), :] = jnp.zeros((halo, xbuf.shape[1]), F32)
        xbuf[pl.ds(halo - (CONV_WIDTH - 1), CONV_WIDTH - 1), :] = cprev_ref[...]
        h_carry[...] = hprev_ref[...]

    x = xr_ref[...]
    xbuf[pl.ds(halo, tt), :] = x
    xc = cb_ref[...] + cw_ref[CONV_WIDTH - 1:CONV_WIDTH, :] * x
    for j in range(CONV_WIDTH - 1):
        xc = xc + cw_ref[j:j + 1, :] * xbuf[pl.ds(halo - (CONV_WIDTH - 1) + j, tt), :]
    xcb = xc.astype(BF16)
    r = _sigmoid(jnp.dot(xcb, wa_ref[...], preferred_element_type=F32) + ba_ref[...])
    ig = _sigmoid(jnp.dot(xcb, wx_ref[...], preferred_element_type=F32) + bx_ref[...])
    z = -lam_ref[...]
    softplus = jnp.maximum(z, 0.0) + jnp.log1p(jnp.exp(-jnp.abs(z)))
    log_a = -RG_C * r * softplus
    a = jnp.exp(log_a)
    a_buf[...] = a
    b_buf[...] = jnp.sqrt(jnp.tanh(-log_a) * (1.0 + a * a)) * (ig * xc)

    def step(t, h):
        h = a_buf[pl.ds(t, 1), :] * h + b_buf[pl.ds(t, 1), :]
        h_buf[pl.ds(t, 1), :] = h
        return h

    h_fin = lax.fori_loop(0, tt, step, h_carry[...], unroll=min(8, tt))
    h_carry[...] = h_fin
    hlast_ref[...] = h_fin
    y_ref[...] = h_buf[...] * jax.nn.gelu(yg_ref[...])
    xbuf[pl.ds(halo - (CONV_WIDTH - 1), CONV_WIDTH - 1), :] = xbuf[pl.ds(halo + tt - (CONV_WIDTH - 1), CONV_WIDTH - 1), :]


def _block_diag(w):
    k, c, d = w.shape
    eye = jnp.eye(k, dtype=w.dtype)
    return (eye[:, None, :, None] * w[:, :, None, :]).reshape(k * c, k * d)


def rglru(xr, yg, conv_prev, h_prev, conv_w, conv_b, rg_wa, rg_ba, rg_wx, rg_bx, rg_lambda, tt):
    n, t, c = xr.shape
    row = pl.BlockSpec((None, tt, c), lambda b, i: (b, i, 0))
    const = lambda shape: pl.BlockSpec(shape, lambda b, i: (0,) * len(shape))
    vec = lambda a: a.reshape(1, c)
    return pl.pallas_call(
        _rglru_body,
        grid=(n, t // tt),
        in_specs=[row, row,
                  pl.BlockSpec((None, CONV_WIDTH - 1, c), lambda b, i: (b, 0, 0)),
                  pl.BlockSpec((None, 1, c), lambda b, i: (b, 0, 0)),
                  const((CONV_WIDTH, c)), const((1, c)), const((c, c)), const((1, c)), const((c, c)),
                  const((1, c)), const((1, c))],
        out_specs=[row, pl.BlockSpec((None, 1, c), lambda b, i: (b, 0, 0))],
        out_shape=[jax.ShapeDtypeStruct((n, t, c), F32), jax.ShapeDtypeStruct((n, 1, c), F32)],
        scratch_shapes=[pltpu.VMEM((tt + SUBLANES, c), F32), pltpu.VMEM((tt, c), F32), pltpu.VMEM((tt, c), F32),
                        pltpu.VMEM((tt, c), F32), pltpu.VMEM((1, c), F32)],
        compiler_params=_cparams("parallel", "arbitrary"),
        name="rglru",
    )(xr, yg, conv_prev, h_prev.reshape(n, 1, c), conv_w, vec(conv_b), _block_diag(rg_wa).astype(BF16), vec(rg_ba),
      _block_diag(rg_wx).astype(BF16), vec(rg_bx), vec(rg_lambda))


def _pack_bf16_pairs(x):
    c = x.shape[1] // 2
    a = pltpu.bitcast(x[:, :c].astype(BF16).astype(F32), U32)
    b = pltpu.bitcast(x[:, c:].astype(BF16).astype(F32), U32)
    return (a & jnp.uint32(0xFFFF0000)) | (b >> 16)


def _unpack_bf16_pairs(u):
    a = pltpu.bitcast(u & jnp.uint32(0xFFFF0000), F32)
    b = pltpu.bitcast(u << 16, F32)
    return jnp.concatenate([a, b], axis=1).astype(BF16)


def _out_router_body(attn_ref, rnn_ref, x_ref, g1_ref, sc2_ref, sh2_ref, ag_ref, rg_ref, wo_ref, n2_ref,
                     rw_ref, rb_ref, tri_ref, x1_ref, h2_ref, gate_ref, code_ref, count_ref):
    mixed = jnp.concatenate([_rms(attn_ref[...], ag_ref[...]), _rms(rnn_ref[...], rg_ref[...])], axis=1)
    x1 = x_ref[...] + g1_ref[...] * jnp.dot(mixed.astype(BF16), wo_ref[...], preferred_element_type=F32)
    x1_ref[...] = x1
    h2 = _rms(x1, n2_ref[...]) * (1.0 + sc2_ref[...]) + sh2_ref[...]
    h2_hi = h2.astype(BF16)
    h2_ref[...] = _pack_bf16_pairs(h2)
    h2_lo = (h2 - h2_hi.astype(F32)).astype(BF16)
    z = (jnp.dot(h2_hi, rw_ref[0], preferred_element_type=F32) + jnp.dot(h2_hi, rw_ref[1], preferred_element_type=F32)
         + jnp.dot(h2_lo, rw_ref[0], preferred_element_type=F32)) + rb_ref[...]
    lane = lax.broadcasted_iota(I32, z.shape, 1)
    neg_inf = -jnp.inf
    gl = jnp.where(lane < N_GROUPS, z, neg_inf)
    gmax = jnp.max(gl, axis=1, keepdims=True)
    gidx = jnp.min(jnp.where(gl == gmax, lane, LANES), axis=1, keepdims=True)
    gprob = 1.0 / jnp.sum(jnp.exp(gl - gmax), axis=1, keepdims=True)
    lo = EXPERT_LANE0 + EXPERTS_PER_GROUP * gidx
    el = jnp.where(lane >= lo, jnp.where(lane < lo + EXPERTS_PER_GROUP, z, neg_inf), neg_inf)
    m1 = jnp.max(el, axis=1, keepdims=True)
    j1 = jnp.min(jnp.where(el == m1, lane, LANES), axis=1, keepdims=True)
    el2 = jnp.where(lane == j1, neg_inf, el)
    m2 = jnp.max(el2, axis=1, keepdims=True)
    j2 = jnp.min(jnp.where(el2 == m2, lane, LANES), axis=1, keepdims=True)
    e21 = jnp.exp(m2 - m1)
    w1 = 1.0 / (1.0 + e21)
    gate_ref[...] = jnp.where(lane == j1, w1 * gprob, jnp.where(lane == j2, e21 * w1 * gprob, 0.0))
    tri = tri_ref[...]
    onehot = jnp.where(lane == gidx, 1.0, 0.0)
    ranks = jnp.dot(tri, onehot.astype(BF16), preferred_element_type=F32)
    counts = jnp.sum(onehot, axis=0, keepdims=True)
    blocks = jnp.floor((counts + (MOE_ROW_BLOCK - 0.5)) * (1.0 / MOE_ROW_BLOCK))
    blocks8 = jnp.broadcast_to(blocks, (SUBLANES, LANES)).astype(BF16)
    below = jnp.where(lax.broadcasted_iota(I32, (LANES, LANES), 0) < lax.broadcasted_iota(I32, (LANES, LANES), 1),
                      1.0, 0.0).astype(BF16)
    starts = jnp.dot(blocks8, below, preferred_element_type=F32)[0:1, :] * MOE_ROW_BLOCK
    dst = jnp.sum((ranks + starts) * onehot, axis=1, keepdims=True)
    code_ref[...] = jnp.where(lane == 0, dst, 0.0).T[0:1, :].astype(I32)
    count_ref[...] = starts.astype(I32)


def out_proj_router(attn, rnn, x, g1, sc2, sh2, attn_g, rnn_g, w_out, norm2_g, router_w, router_b, tt):
    n, t, d = x.shape
    c = attn.shape[2]
    per_row = g1.shape[1] != 1
    mod_spec = (pl.BlockSpec((None, tt, d), lambda b, i: (b, i, 0)) if per_row
                else pl.BlockSpec((None, 1, d), lambda b, i: (b, 0, 0)))
    row = lambda w_: pl.BlockSpec((None, tt, w_), lambda b, i: (b, i, 0))
    const = lambda shape: pl.BlockSpec(shape, lambda b, i: (0,) * len(shape))
    nt = t // tt
    per_tile = lambda w_: pl.BlockSpec((None, 1, w_), lambda b, i: (b * nt + i, 0, 0))
    tri = jnp.tril(jnp.ones((tt, tt), BF16), -1)
    return pl.pallas_call(
        _out_router_body,
        grid=(n, nt),
        in_specs=[row(c), row(c), row(d), mod_spec, mod_spec, mod_spec, const((1, c)), const((1, c)),
                  const(w_out.shape), const((1, d)), const(router_w.shape), const((1, LANES)), const((tt, tt))],
        out_specs=[row(d), row(d // 2), row(LANES), per_tile(tt), per_tile(LANES)],
        out_shape=[jax.ShapeDtypeStruct((n, t, d), F32), jax.ShapeDtypeStruct((n, t, d // 2), U32),
                   jax.ShapeDtypeStruct((n, t, LANES), F32), jax.ShapeDtypeStruct((n * nt, 1, tt), I32),
                   jax.ShapeDtypeStruct((n * nt, 1, LANES), I32)],
        compiler_params=_cparams("parallel", "arbitrary"),
        name="out_proj_router",
    )(attn, rnn, x, g1, sc2, sh2, attn_g.reshape(1, c), rnn_g.reshape(1, c), w_out, norm2_g.reshape(1, d),
      router_w, router_b, tri)


def split_bf16(w):
    hi = w.astype(BF16)
    return jnp.stack([hi, (w - hi.astype(F32)).astype(BF16)])


def _moe_body(dst_ref, start_ref, h2_ref, gate_ref, x1_ref, g2_ref, wgu_ref, wd_ref, fg_ref, y_ref,
              hs_ref, hb_ref, gs_ref, acc_ref):
    step = pl.program_id(2)
    tt = h2_ref.shape[0]
    d_exp = wd_ref.shape[1]
    rb = MOE_ROW_BLOCK

    @pl.when(step == 0)
    def _():
        hs_ref[...] = jnp.zeros(hs_ref.shape, U32)
        gs_ref[...] = jnp.zeros(gs_ref.shape, F32)

        def move_in(t, c):
            dst = dst_ref[0, t]
            hs_ref[pl.ds(dst, 1), :] = h2_ref[pl.ds(t, 1), :]
            gs_ref[pl.ds(dst, 1), :] = gate_ref[pl.ds(t, 1), :]
            return c

        lax.fori_loop(0, tt, move_in, 0, unroll=8)
        hb_ref[...] = _unpack_bf16_pairs(hs_ref[...])
        acc_ref[...] = jnp.zeros(acc_ref.shape, F32)

    lane = lax.broadcasted_iota(I32, (rb, LANES), 1)
    for k in range(MOE_EXPERTS_PER_STEP):
        e = step * MOE_EXPERTS_PER_STEP + k
        grp = e // EXPERTS_PER_GROUP

        def block(b, c, k=k, e=e):
            rows = pl.ds(pl.multiple_of(b * rb, rb), rb)
            gcol = jnp.sum(jnp.where(lane == e + EXPERT_LANE0, gs_ref[rows, :], 0.0), axis=1, keepdims=True)
            gu = jnp.dot(hb_ref[rows, :], wgu_ref[k], preferred_element_type=F32)
            gpart, upart = gu[:, :d_exp], gu[:, d_exp:]
            act = (gpart * _sigmoid(gpart) * upart).astype(BF16)
            acc_ref[rows, :] += gcol * jnp.dot(act, wd_ref[k], preferred_element_type=F32)
            return c

        lax.fori_loop(start_ref[0, grp] // rb, start_ref[0, grp + 1] // rb, block, 0)

    @pl.when(step == pl.num_programs(2) - 1)
    def _():
        def move_out(t, c):
            y_ref[pl.ds(t, 1), :] = acc_ref[pl.ds(dst_ref[0, t], 1), :]
            return c

        lax.fori_loop(0, tt, move_out, 0, unroll=8)
        x2 = x1_ref[...] + g2_ref[...] * y_ref[...]
        y_ref[...] = _rms(x2, fg_ref[...])


def moe_final(codes, counts, h2, gates, x1, g2, w_gu, w_down, final_g, tt):
    n, t, d = x1.shape
    n_exp = w_gu.shape[0]
    nt = t // tt
    per_row = g2.shape[1] != 1
    mod_spec = (pl.BlockSpec((None, tt, d), lambda b, i, e: (b, i, 0)) if per_row
                else pl.BlockSpec((None, 1, d), lambda b, i, e: (b, 0, 0)))
    row = lambda w_: pl.BlockSpec((None, tt, w_), lambda b, i, e: (b, i, 0))
    smem_tile = lambda w_: pl.BlockSpec((None, 1, w_), lambda b, i, e: (b * nt + i, 0, 0), memory_space=pltpu.SMEM)
    sort_rows = tt + N_GROUPS * MOE_ROW_BLOCK
    eps = MOE_EXPERTS_PER_STEP
    return pl.pallas_call(
        _moe_body,
        grid=(n, nt, n_exp // eps),
        in_specs=[smem_tile(tt), smem_tile(LANES), row(d // 2), row(LANES), row(d), mod_spec,
                  pl.BlockSpec((eps,) + w_gu.shape[1:], lambda b, i, e: (e, 0, 0)),
                  pl.BlockSpec((eps,) + w_down.shape[1:], lambda b, i, e: (e, 0, 0)),
                  pl.BlockSpec((1, d), lambda b, i, e: (0, 0))],
        out_specs=row(d),
        out_shape=jax.ShapeDtypeStruct((n, t, d), F32),
        scratch_shapes=[pltpu.VMEM((sort_rows, d // 2), U32),
                        pltpu.VMEM((sort_rows, d), BF16),
                        pltpu.VMEM((sort_rows, LANES), F32),
                        pltpu.VMEM((sort_rows, d), F32)],
        compiler_params=_cparams("parallel", "arbitrary", "arbitrary"),
        name="moe_final",
    )(codes, counts, h2, gates, x1, g2, w_gu, w_down, final_g.reshape(1, d))


def _tile(t, pref):
    return pref if t % pref == 0 else t


def _layer(x, mods, pos, attend, conv_prev, h_prev, n_seq, lw, final_g):
    sh1, sc1, g1, sh2, sc2, g2 = mods
    n, t, d = x.shape
    t_seq = n * t // n_seq
    seqs = lambda a: a.reshape(n_seq, t_seq, a.shape[-1])
    q, k, v, kb, vt, iq, ikw, ikb, xr, yg = in_proj(x, sc1, sh1, lw["norm1_g"], lw["w_in"], rope_tables(pos),
                                                     _tile(t, 512))
    attn = attend(q, iq, ikw, k, v, kb, vt, ikb)
    rnn, h_new = rglru(seqs(xr), seqs(yg), conv_prev, h_prev, lw["conv_w"], lw["conv_b"], lw["rg_wa"], lw["rg_ba"],
                       lw["rg_wx"], lw["rg_bx"], lw["rg_lambda"], _tile(t_seq, 512))
    moe_tile = _tile(t, 1024)
    x1, h2, gates, codes, counts = out_proj_router(
        attn, rnn.reshape(xr.shape), x, g1, sc2, sh2, lw["attn_out_g"], lw["rnn_out_g"], lw["w_out"], lw["norm2_g"],
        lw["router_w"], lw["router_b"], moe_tile)
    y = moe_final(codes, counts, h2, gates, x1, g2, lw["w_gu"], lw["w_down"], final_g, moe_tile)
    conv_new = seqs(xr)[:, t_seq - (CONV_WIDTH - 1):, :]
    return (seqs(y), seqs(k).reshape(n_seq, t_seq, N_KV_HEADS, HEAD_DIM),
            seqs(v).reshape(n_seq, t_seq, N_KV_HEADS, HEAD_DIM), seqs(ikw)[:, :, :IDX_HEAD_DIM], conv_new,
            h_new[:, 0, :])


def kernel(x_prompt, x_sample, cache_k, cache_v, cache_idx_k, state_conv, state_rglru, page_table, c_prompt,
           c_sample, ada_w, ada_b, norm1_g, w_in, conv_w, conv_b, rg_wa, rg_ba, rg_wx, rg_bx, rg_lambda, attn_out_g,
           rnn_out_g, w_out, norm2_g, router_grp_w, router_grp_b, router_exp_w, router_exp_b, exp_w_gate, exp_w_up,
           exp_w_down, final_g):
    depth = ada_w.shape[0]
    assert depth == 1, "this implementation handles the single-layer configuration"
    n_p, t_p, d = x_prompt.shape
    n_s, t_s, _ = x_sample.shape
    n_pages = page_table.shape[1]
    past = n_pages * PAGE_SIZE
    l = 0
    router_w = jnp.concatenate([router_grp_w[l], router_exp_w[l],
                                jnp.zeros((d, LANES - N_GROUPS - N_EXPERTS), F32)], axis=1)
    router_b = jnp.concatenate([router_grp_b[l], router_exp_b[l],
                                jnp.zeros((LANES - N_GROUPS - N_EXPERTS,), F32)]).reshape(1, LANES)
    lw = dict(norm1_g=norm1_g[l], w_in=prep_w_in(w_in[l]), conv_w=conv_w[l], conv_b=conv_b[l], rg_wa=rg_wa[l],
              rg_ba=rg_ba[l], rg_wx=rg_wx[l], rg_bx=rg_bx[l], rg_lambda=rg_lambda[l], attn_out_g=attn_out_g[l],
              rnn_out_g=rnn_out_g[l], w_out=w_out[l].astype(BF16), norm2_g=norm2_g[l],
              router_w=split_bf16(router_w), router_b=router_b,
              w_gu=jnp.concatenate([exp_w_gate[l], exp_w_up[l]], axis=2).astype(BF16),
              w_down=exp_w_down[l].astype(BF16))

    mod = ada_mod(jnp.concatenate([c_prompt, c_sample], axis=0), ada_w[l], ada_b[l])
    mods_p = [m[:n_p, None, :] for m in jnp.split(mod, 6, axis=1)]
    mods_s = [jnp.repeat(m[n_p:], t_s, axis=0)[None] for m in jnp.split(mod, 6, axis=1)]

    def attend_prompt(q, iq, ikw, k, v, kb, vt, ikb):
        iw_t = jnp.swapaxes(ikw[:, :, IDX_HEAD_DIM:IDX_HEAD_DIM + N_IDX_HEADS], 1, 2)
        return prompt_attention(q, iq, iw_t, kb, vt, ikb, _tile(t_p, 256))

    def attend_sample(q, iq, ikw, k, v, kb, vt, ikb):
        seqs = lambda a: a.reshape(n_s, t_s, a.shape[-1])
        n_pool = cache_k.shape[1]
        feat_major = lambda c: jnp.swapaxes(c.reshape(n_pool, PAGE_SIZE, -1), 1, 2)
        out = sample_attention(page_table, seqs(q), seqs(iq), seqs(ikw), seqs(k), seqs(v),
                               feat_major(cache_k[l]), feat_major(cache_v[l]), feat_major(cache_idx_k[l]))
        return out.reshape(1, n_s * t_s, N_HEADS * HEAD_DIM)

    c_rnn = conv_w.shape[2]
    ys, ks, vs, iks, cvs, hs = _layer(
        x_sample.reshape(1, n_s * t_s, d), mods_s, jnp.tile(past + jnp.arange(t_s), n_s), attend_sample,
        state_conv[l], state_rglru[l], n_s, lw, final_g)
    yp, kp, vp, ikp, cvp, hp = _layer(
        x_prompt, mods_p, jnp.arange(t_p), attend_prompt,
        jnp.zeros((n_p, CONV_WIDTH - 1, c_rnn), F32), jnp.zeros((n_p, c_rnn), F32), n_p, lw, final_g)

    st = lambda a: a[None]
    return (yp, ys, st(kp), st(vp), st(ikp), st(cvp), st(hp), st(ks), st(vs), st(iks), st(cvs), st(hs))
```

```python
import functools

import jax
import jax.numpy as jnp
from jax import lax
from jax.experimental import pallas as pl
from jax.experimental.pallas import tpu as pltpu

F32, BF16, I32, I16, U32 = jnp.float32, jnp.bfloat16, jnp.int32, jnp.int16, jnp.uint32
HIGHEST = lax.Precision.HIGHEST

HEAD_DIM = 64
N_HEADS = 8
N_KV_HEADS = 2
HEADS_PER_KV = N_HEADS // N_KV_HEADS
N_IDX_HEADS = 8
IDX_HEAD_DIM = 32
TOPK_MAX = 256
ROPE_THETA = 10000.0
CONV_WIDTH = 4
RG_BLOCKS = 8
RG_C = 8.0
N_GROUPS = 4
EXPERTS_PER_GROUP = 8
N_EXPERTS = N_GROUPS * EXPERTS_PER_GROUP
PAGE_SIZE = 128
NORM_EPS = 1e-6

LANES = 128
SUBLANES = 8
VMEM_LIMIT = 48 * 1024 * 1024
VMEM_LIMIT_MOE = 54 * 1024 * 1024
INT_MIN = -(2 ** 31)
I16_MIN = -(2 ** 15)
NEG_BIG = -1e30
EXPERT_LANE0 = N_GROUPS
BF16_SUBLANES = 16
V_AUG_ROWS = HEAD_DIM + BF16_SUBLANES
ATT_CHUNK_HEADS = 4
COUNT_TILES = 2
MOE_ROW_BLOCK = 304
MOE_EXPERTS_PER_STEP = 4
LOG2_E = 1.4426950408889634
Q_SCALE = HEAD_DIM ** -0.5 * LOG2_E


def _cparams(*sem):
    return pltpu.CompilerParams(dimension_semantics=sem, vmem_limit_bytes=VMEM_LIMIT)


def _rms(x, g):
    return x * lax.rsqrt(jnp.mean(x * x, axis=-1, keepdims=True) + NORM_EPS) * g


def _sigmoid(x):
    return 1.0 / (1.0 + jnp.exp(-x))


def _float_key(s):
    b = pltpu.bitcast(s + 0.0, I32)
    return b ^ ((b >> 31) & 0x7FFFFFFF)


def _ada_body(c_ref, w_ref, b_ref, o_ref):
    c = c_ref[...]
    s = c * _sigmoid(c)
    o_ref[...] = jnp.dot(s, w_ref[...], preferred_element_type=F32, precision=HIGHEST) + b_ref[...]


def ada_mod(c, w, b):
    m, d = c.shape
    n_chunks = w.shape[1] // d
    return pl.pallas_call(
        _ada_body,
        grid=(n_chunks,),
        in_specs=[pl.BlockSpec((m, d), lambda j: (0, 0)),
                  pl.BlockSpec((d, d), lambda j: (0, j)),
                  pl.BlockSpec((1, d), lambda j: (0, j))],
        out_specs=pl.BlockSpec((m, d), lambda j: (0, j)),
        out_shape=jax.ShapeDtypeStruct((m, w.shape[1]), F32),
        compiler_params=_cparams("arbitrary"),
        name="ada_mod",
    )(c, w, b.reshape(1, -1))


_C_Q, _C_K, _C_V, _C_IQ, _C_IKW, _C_XR, _C_YG, _C_END = (0, 512, 640, 768, 1024, 1152, 1664, 2176)


def prep_w_in(w_in):
    d_in = w_in.shape[0]
    q, k, v, iq, ik, iw, xr, yg = jnp.split(w_in, [512, 640, 768, 1024, 1056, 1064, 1576], axis=1)
    ikw = jnp.concatenate([ik, iw, jnp.zeros((d_in, LANES - IDX_HEAD_DIM - N_IDX_HEADS), w_in.dtype)], axis=1)
    return jnp.concatenate([q, k, v, iq, ikw, xr, yg], axis=1).astype(BF16)


def rope_tables(pos):
    pos = pos.astype(F32)[:, None]
    lane = jnp.arange(LANES)[None, :]

    def cs(head_dim):
        half = head_dim // 2
        inv = ROPE_THETA ** (-jnp.arange(half, dtype=F32) / half)
        ang = pos * inv[None, :]
        reps = LANES // head_dim
        cos, sin = jnp.tile(jnp.cos(ang), (1, 2 * reps)), jnp.tile(jnp.sin(ang), (1, 2 * reps))
        return cos, jnp.where(lane % head_dim < half, -sin, sin)

    cq, sq = cs(HEAD_DIM)
    ci, si = cs(IDX_HEAD_DIM)
    w_scale = (IDX_HEAD_DIM ** -0.5) * (N_IDX_HEADS ** -0.5)
    cx = jnp.where(lane < IDX_HEAD_DIM, ci, jnp.where(lane < IDX_HEAD_DIM + N_IDX_HEADS, w_scale, 0.0))
    sx = jnp.where(lane < IDX_HEAD_DIM, si, 0.0)
    return cq, sq, ci, si, cx.astype(F32), sx.astype(F32)


def _rope(x, cos, sin_signed, head_dim):
    half = head_dim // 2
    out = []
    for c in range(x.shape[1] // LANES):
        xs = x[:, c * LANES:(c + 1) * LANES]
        lane = lax.broadcasted_iota(I32, xs.shape, 1)
        partner = jnp.where(lane % head_dim < half, pltpu.roll(xs, LANES - half, axis=1), pltpu.roll(xs, half, axis=1))
        out.append(xs * cos + partner * sin_signed)
    return out[0] if len(out) == 1 else jnp.concatenate(out, axis=1)


def _in_proj_body(x_ref, sc_ref, sh_ref, g_ref, w_ref, cq_ref, sq_ref, ci_ref, si_ref, cx_ref, sx_ref,
                  q_ref, k_ref, v_ref, kb_ref, vt_ref, iq_ref, ikw_ref, ikb_ref, xr_ref, yg_ref):
    x = x_ref[...]
    h = (_rms(x, g_ref[...]) * (1.0 + sc_ref[...]) + sh_ref[...]).astype(BF16)

    def seg(a, b):
        return jnp.dot(h, w_ref[:, a:b], preferred_element_type=F32)

    cq, sq = cq_ref[...], sq_ref[...]
    q_ref[...] = (_rope(seg(_C_Q, _C_K), cq, sq, HEAD_DIM) * Q_SCALE).astype(BF16)
    k = _rope(seg(_C_K, _C_V), cq, sq, HEAD_DIM)
    k_ref[...] = k
    kb_ref[...] = k.astype(BF16)
    v = seg(_C_V, _C_IQ)
    v_ref[...] = v
    vt_ref[...] = v.T.astype(BF16)
    iq_ref[...] = _rope(seg(_C_IQ, _C_IKW), ci_ref[...], si_ref[...], IDX_HEAD_DIM).astype(BF16)
    ikw = _rope(seg(_C_IKW, _C_XR), cx_ref[...], sx_ref[...], IDX_HEAD_DIM)
    ikw_ref[...] = ikw
    ikb_ref[...] = ikw.astype(BF16)
    xr_ref[...] = seg(_C_XR, _C_YG)
    yg_ref[...] = seg(_C_YG, _C_END)


def in_proj(x, sc, sh, g, w, tables, tt):
    n, t, d = x.shape
    per_row = sc.shape[1] != 1
    mod_spec = (pl.BlockSpec((None, tt, d), lambda b, i: (b, i, 0)) if per_row
                else pl.BlockSpec((None, 1, d), lambda b, i: (b, 0, 0)))
    row = lambda w_: pl.BlockSpec((None, tt, w_), lambda b, i: (b, i, 0))
    tab = pl.BlockSpec((tt, LANES), lambda b, i: (i, 0))
    outs = [(512, BF16), (128, F32), (128, F32), (128, BF16), None, (256, BF16), (128, F32), (128, BF16),
            (512, F32), (512, F32)]
    out_shape, out_specs = [], []
    for o in outs:
        if o is None:
            out_shape.append(jax.ShapeDtypeStruct((n, LANES, t), BF16))
            out_specs.append(pl.BlockSpec((None, LANES, tt), lambda b, i: (b, 0, i)))
        else:
            out_shape.append(jax.ShapeDtypeStruct((n, t, o[0]), o[1]))
            out_specs.append(row(o[0]))
    return pl.pallas_call(
        _in_proj_body,
        grid=(n, t // tt),
        in_specs=[row(d), mod_spec, mod_spec,
                  pl.BlockSpec((1, d), lambda b, i: (0, 0)),
                  pl.BlockSpec(w.shape, lambda b, i: (0, 0))] + [tab] * 6,
        out_specs=out_specs,
        out_shape=out_shape,
        compiler_params=_cparams("parallel", "arbitrary"),
        name="in_proj",
    )(x, sc, sh, g.reshape(1, d), w, *tables)


def _pattn_body(q_ref, iq_ref, iwt_ref, k_ref, vt_ref, ik_ref, o_ref,
                hi_ref, lo_ref, lb_ref, p_ref, m_ref, acc_ref, mm_ref, *, n_sel, idx_bits):
    qi = pl.program_id(1)
    tq = q_ref.shape[0]
    tk = tq
    n_tiles = qi + 1

    iq_t = iq_ref[...].astype(F32).T
    iq_st = jnp.concatenate([iq_t[IDX_HEAD_DIM * h:IDX_HEAD_DIM * (h + 1)] for h in range(N_IDX_HEADS)],
                            axis=1).astype(BF16)
    w_t = iwt_ref[...]
    q_t = q_ref[...].astype(F32).T
    zero_half = jnp.zeros((HEAD_DIM, tq), F32)
    q_pad = []
    for g in range(N_KV_HEADS):
        cols = []
        for hh in range(HEADS_PER_KV):
            h = g * HEADS_PER_KV + hh
            qh = q_t[HEAD_DIM * h:HEAD_DIM * (h + 1)]
            cols.append(jnp.concatenate([qh, zero_half] if g == 0 else [zero_half, qh], axis=0))
        q_pad.append(jnp.concatenate(cols, axis=1).astype(BF16))

    row_s = lax.broadcasted_iota(I32, (tk, tq), 0)
    col_t = lax.broadcasted_iota(I32, (tk, tq), 1)
    causal = row_s <= col_t

    def tile_off(j):
        return pl.multiple_of(j * tk, tk)

    def walk_tiles(step):
        def pair(p, c):
            step(2 * p, 0, False)
            step(2 * p + 1, 1, False)
            return c

        lax.fori_loop(0, qi // 2, pair, 0)

        @pl.when(qi % 2 == 0)
        def _():
            step(qi, 0, True)

        @pl.when(qi % 2 == 1)
        def _():
            step(qi - 1, 0, False)
            step(qi, 1, True)

    def index_dots(j, slot):
        ik = ik_ref[pl.ds(tile_off(j), tk), :][:, :IDX_HEAD_DIM]
        mm_ref[slot] = jnp.dot(ik, iq_st, preferred_element_type=F32)

    def p1(j, slot, last):
        if not last:
            index_dots(j + 1, 1 - slot)
        sc = w_t[0:1, :] * jnp.maximum(mm_ref[slot, :, 0:tq], 0.0)
        for h in range(1, N_IDX_HEADS):
            sc = sc + w_t[h:h + 1, :] * jnp.maximum(mm_ref[slot, :, h * tq:(h + 1) * tq], 0.0)
        key = _float_key(sc)
        if last:
            key = jnp.where(causal, key, INT_MIN)
        rows = pl.ds(tile_off(j), tk)
        hi_ref[rows, :] = (key >> 16).astype(I16)
        lo_ref[rows, :] = ((key & 0xFFFF) + I16_MIN).astype(I16)

    index_dots(0, 0)
    walk_tiles(p1)

    span = COUNT_TILES * tk
    n_spans = (n_tiles + COUNT_TILES - 1) // COUNT_TILES
    groups = span // BF16_SUBLANES
    pad_rows = pl.ds(tile_off(n_tiles), span - tk)
    hi_ref[pad_rows, :] = jnp.full((span - tk, tq), I16_MIN, I16)
    lo_ref[pad_rows, :] = jnp.full((span - tk, tq), I16_MIN, I16)

    def rows16(v):
        return jnp.broadcast_to(v.astype(I16), (BF16_SUBLANES, tq))[None]

    def span_rows(j):
        return pl.ds(pl.multiple_of(j * span, span), span)

    def tile16(ref, j):
        return ref[span_rows(j), :].reshape(groups, BF16_SUBLANES, tq)

    one16, zero16 = jnp.int16(1), jnp.int16(0)

    def count(pred):
        def body(j, acc):
            marks = pred(j)
            parts = [marks[g] for g in range(groups)]
            while len(parts) > 1:
                parts = [parts[i] + parts[i + 1] for i in range(0, len(parts), 2)]
            return acc + parts[0]

        acc = lax.fori_loop(0, n_spans, body, jnp.zeros((BF16_SUBLANES, tq), I16))
        return jnp.sum(acc.astype(I32).astype(F32), axis=0, keepdims=True)

    def bisect16(ref, need):
        def step(b, thr):
            cand = thr + jnp.left_shift(jnp.int32(1), 15 - b)
            cand_b = rows16(cand)
            c = count(lambda j: jnp.where(tile16(ref, j) >= cand_b, one16, zero16))
            return jnp.where(c >= need, cand, thr)
        return lax.fori_loop(0, 16, step, jnp.full((1, tq), I16_MIN, I32))

    thr_hi = bisect16(hi_ref, n_sel)
    hi_b = rows16(thr_hi)
    c_above = count(lambda j: jnp.where(tile16(hi_ref, j) > hi_b, one16, zero16))

    def p2(j, c):
        rows = span_rows(j)
        lb_ref[rows, :] = jnp.where(hi_ref[rows, :] == thr_hi.astype(I16), lo_ref[rows, :], jnp.int16(I16_MIN))
        return c

    lax.fori_loop(0, n_spans, p2, 0)
    thr_lo = bisect16(lb_ref, n_sel - c_above)
    lo_b = rows16(thr_lo)
    c_gt = c_above + count(lambda j: jnp.where(tile16(lb_ref, j) > lo_b, one16, zero16))
    ties_left = n_sel - c_gt

    def is_tie(j):
        return jnp.where(tile16(hi_ref, j) == hi_b, jnp.where(tile16(lo_ref, j) == lo_b, one16, zero16), zero16)

    c_tie = count(is_tie)
    p_ref[...] = jnp.full((1, tq), 2 ** idx_bits - 1, I32)
    has_thr = jnp.where(thr_hi > I16_MIN, 1, jnp.where(thr_lo > I16_MIN, 1, 0))
    excess = jnp.max(jnp.where(has_thr > 0, c_tie - ties_left, 0.0)) > 0.0

    idx16 = row_s.astype(I16)

    @pl.when(excess)
    def _():
        sub = lax.broadcasted_iota(I32, (span, tq), 0).astype(I16).reshape(groups, BF16_SUBLANES, tq)

        def tb(b, p):
            cand = p | jnp.left_shift(jnp.int32(1), idx_bits - 1 - b)
            c = count(lambda j: jnp.where(sub < rows16(cand - j * span), is_tie(j), zero16))
            return jnp.where(c < ties_left, cand, p)

        p_ref[...] = lax.fori_loop(0, idx_bits, tb, jnp.zeros((1, tq), I32))

    p_last = p_ref[...]
    thr_hi16, thr_lo16 = thr_hi.astype(I16), thr_lo.astype(I16)

    m_ref[...] = jnp.full(m_ref.shape, NEG_BIG, F32)
    acc_ref[...] = jnp.zeros(acc_ref.shape, F32)
    ones_rows = jnp.ones((V_AUG_ROWS - HEAD_DIM, tk), BF16)

    w_c = ATT_CHUNK_HEADS * tq
    chunks = [(g, c * w_c) for g in range(N_KV_HEADS) for c in range(HEADS_PER_KV // ATT_CHUNK_HEADS)]

    def score_chunk(kt, slot, ci):
        g, c0 = chunks[ci]
        mm_ref[slot, :, ci * w_c:(ci + 1) * w_c] = jnp.dot(kt, q_pad[g][:, c0:c0 + w_c],
                                                             preferred_element_type=F32)

    kt0 = k_ref[pl.ds(0, tk), :]
    for ci in range(len(chunks)):
        score_chunk(kt0, 0, ci)

    def attend(j, slot, diag):
        off = tile_off(j)
        hi, lo = hi_ref[pl.ds(off, tk), :], lo_ref[pl.ds(off, tk), :]
        keep_tie = jnp.where(idx16 <= (p_last - j * tk).astype(I16), one16, zero16)
        sel = jnp.where(hi > thr_hi16, one16,
                        jnp.where(hi == thr_hi16,
                                  jnp.where(lo > thr_lo16, one16, jnp.where(lo == thr_lo16, keep_tie, zero16)),
                                  zero16))
        bias = jnp.where(sel.astype(I32) > 0, 0.0, NEG_BIG)
        if diag:
            bias = jnp.where(causal, bias, NEG_BIG)
        vt = vt_ref[:, pl.ds(off, tk)]
        v_aug = [jnp.concatenate([vt[HEAD_DIM * g:HEAD_DIM * (g + 1)], ones_rows], axis=0)
                 for g in range(N_KV_HEADS)]
        bias_c = jnp.concatenate([bias] * ATT_CHUNK_HEADS, axis=1)
        if not diag:
            kt_next = k_ref[pl.ds(tile_off(j + 1), tk), :]
        for ci, (g, c0) in enumerate(chunks):
            if not diag:
                score_chunk(kt_next, 1 - slot, ci)
            cols = slice(c0, c0 + w_c)
            s = mm_ref[slot, :, ci * w_c:(ci + 1) * w_c] + bias_c
            m_old = m_ref[g:g + 1, cols]
            m_new = jnp.maximum(m_old, jnp.max(s, axis=0, keepdims=True))
            alpha = jnp.exp2(m_old - m_new)
            p = jnp.exp2(s - m_new).astype(BF16)
            acc_ref[g, :, cols] = alpha * acc_ref[g, :, cols] + jnp.dot(v_aug[g], p, preferred_element_type=F32)
            m_ref[g:g + 1, cols] = m_new

    walk_tiles(attend)

    rows = []
    for g in range(N_KV_HEADS):
        acc = acc_ref[g]
        o_g = acc[:HEAD_DIM] / acc[HEAD_DIM:HEAD_DIM + 1]
        rows += [o_g[:, hh * tq:(hh + 1) * tq] for hh in range(HEADS_PER_KV)]
    o_ref[...] = jnp.concatenate(rows, axis=0).T


def prompt_attention(q, iq, iw_t, kb, vt, ikb, tq):
    n, t, _ = q.shape
    n_sel = min(TOPK_MAX, t // 4)
    idx_bits = max(1, (t - 1).bit_length())
    body = functools.partial(_pattn_body, n_sel=n_sel, idx_bits=idx_bits)
    return pl.pallas_call(
        body,
        grid=(n, t // tq),
        in_specs=[pl.BlockSpec((None, tq, q.shape[2]), lambda b, i: (b, i, 0)),
                  pl.BlockSpec((None, tq, iq.shape[2]), lambda b, i: (b, i, 0)),
                  pl.BlockSpec((None, N_IDX_HEADS, tq), lambda b, i: (b, 0, i)),
                  pl.BlockSpec((None, t, LANES), lambda b, i: (b, 0, 0)),
                  pl.BlockSpec((None, LANES, t), lambda b, i: (b, 0, 0)),
                  pl.BlockSpec((None, t, LANES), lambda b, i: (b, 0, 0))],
        out_specs=pl.BlockSpec((None, tq, N_HEADS * HEAD_DIM), lambda b, i: (b, i, 0)),
        out_shape=jax.ShapeDtypeStruct((n, t, N_HEADS * HEAD_DIM), F32),
        scratch_shapes=[pltpu.VMEM((t + (COUNT_TILES - 1) * tq, tq), I16),
                        pltpu.VMEM((t + (COUNT_TILES - 1) * tq, tq), I16),
                        pltpu.VMEM((t + (COUNT_TILES - 1) * tq, tq), I16),
                        pltpu.VMEM((1, tq), I32),
                        pltpu.VMEM((N_KV_HEADS, HEADS_PER_KV * tq), F32),
                        pltpu.VMEM((N_KV_HEADS, V_AUG_ROWS, HEADS_PER_KV * tq), F32),
                        pltpu.VMEM((2, tq, N_HEADS * tq), F32)],
        compiler_params=_cparams("parallel", "arbitrary"),
        name="prompt_attention",
    )(q, iq, iw_t, kb, vt, ikb)


Q_PAD = SUBLANES


def _sattn_body(pt_ref, q_ref, iq_ref, iw_ref, kn_ref, vn_ref, ikn_ref, ck_hbm, cv_hbm, cik_hbm, o_ref,
                kbuf, vbuf, ikbuf, s_ref, p_ref, sem, *, n_pages, n_sel, idx_bits):
    b = pl.program_id(0)
    slot = b % 2
    past = n_pages * PAGE_SIZE
    lk = past + LANES

    streams = ((cik_hbm, ikbuf), (ck_hbm, kbuf), (cv_hbm, vbuf))

    def page_copy(kind, p, seq, slt):
        src, dst = streams[kind]
        return pltpu.make_async_copy(src.at[pt_ref[seq, p]], dst.at[slt, :, pl.ds(p * PAGE_SIZE, PAGE_SIZE)],
                                     sem.at[slt, kind])

    def start_pages(seq, slt):
        for kind in range(len(streams)):
            for p in range(n_pages):
                page_copy(kind, p, seq, slt).start()

    def wait_pages(kind):
        for p in range(n_pages):
            page_copy(kind, p, b, slot).wait()

    @pl.when(b == 0)
    def _():
        start_pages(0, 0)

    @pl.when(b + 1 < pl.num_programs(0))
    def _():
        start_pages(b + 1, 1 - slot)

    def new_cols(ref):
        rows = jnp.concatenate([ref[...], jnp.zeros((LANES - Q_PAD, LANES), F32)], axis=0)
        return rows.T

    new_tile = pl.ds(past, LANES)
    kbuf[slot, :, new_tile] = new_cols(kn_ref)
    vbuf[slot, :, new_tile] = new_cols(vn_ref)
    ikbuf[slot, :, new_tile] = new_cols(ikn_ref)[:IDX_HEAD_DIM]

    iq = iq_ref[...].astype(F32)
    iq_st = jnp.concatenate([iq[:, IDX_HEAD_DIM * h:IDX_HEAD_DIM * (h + 1)] for h in range(N_IDX_HEADS)],
                            axis=0).astype(BF16)
    qf = q_ref[...].astype(F32)
    lane128 = lax.broadcasted_iota(I32, (Q_PAD, LANES), 1)
    q_rows = []
    for h in range(N_HEADS):
        g = h // HEADS_PER_KV
        slab = qf[:, LANES * (h // 2):LANES * (h // 2 + 1)]
        if (h % 2) != g:
            slab = pltpu.roll(slab, HEAD_DIM, axis=1)
        keep = (lane128 < HEAD_DIM) if g == 0 else (lane128 >= HEAD_DIM)
        q_rows.append(jnp.where(keep, slab, 0.0))
    q_pad = jnp.concatenate(q_rows, axis=0).astype(BF16)
    iw = iw_ref[...]

    key_pos = lax.broadcasted_iota(I32, (Q_PAD, lk), 1)
    q_row = lax.broadcasted_iota(I32, (Q_PAD, lk), 0)
    admissible = key_pos <= past + q_row

    wait_pages(0)
    d = jnp.dot(iq_st, ikbuf[slot].astype(BF16), preferred_element_type=F32)
    sc = None
    for h in range(N_IDX_HEADS):
        w_h = iw[:, IDX_HEAD_DIM + h:IDX_HEAD_DIM + h + 1]
        term = w_h * jnp.maximum(d[Q_PAD * h:Q_PAD * (h + 1)], 0.0)
        sc = term if sc is None else sc + term
    s_ref[...] = jnp.where(admissible, _float_key(sc), INT_MIN)

    lane_pos = lax.broadcasted_iota(I32, (Q_PAD, LANES), 1)

    def count(pred):
        terms = [pred(s_ref[:, c * LANES:(c + 1) * LANES], c * LANES) for c in range(lk // LANES)]
        while len(terms) > 1:
            terms = [terms[i] + terms[i + 1] if i + 1 < len(terms) else terms[i] for i in range(0, len(terms), 2)]
        return jnp.sum(terms[0], axis=1, keepdims=True)

    def counts3(c1, c2, c3):
        terms = [[], [], []]
        for c in range(lk // LANES):
            t = s_ref[:, c * LANES:(c + 1) * LANES]
            for k, cand in enumerate((c1, c2, c3)):
                terms[k].append(jnp.where(t >= cand, 1.0, 0.0))
        sums = []
        for ts in terms:
            while len(ts) > 1:
                ts = [ts[i] + ts[i + 1] if i + 1 < len(ts) else ts[i] for i in range(0, len(ts), 2)]
            sums.append(jnp.sum(ts[0], axis=1, keepdims=True))
        return sums

    def quaternary(i, thr):
        q = jnp.left_shift(jnp.int32(1), 30 - 2 * i)
        cands = [jnp.broadcast_to(thr + k * q, (Q_PAD, LANES)) for k in (1, 2, 3)]
        n1, n2, n3 = counts3(*cands)
        passed = (jnp.where(n1 >= n_sel, 1, 0) + jnp.where(n2 >= n_sel, 1, 0) + jnp.where(n3 >= n_sel, 1, 0))
        return thr + passed * q

    thr = lax.fori_loop(0, 16, quaternary, jnp.full((Q_PAD, 1), INT_MIN, I32))
    thr_b = jnp.broadcast_to(thr, (Q_PAD, LANES))
    ties_left = n_sel - count(lambda t, c0: jnp.where(t > thr_b, 1.0, 0.0))
    c_tie = count(lambda t, c0: jnp.where(t == thr_b, 1.0, 0.0))
    p_ref[...] = jnp.full((Q_PAD, 1), 2 ** idx_bits - 1, I32)

    @pl.when(jnp.max(jnp.where(thr > INT_MIN, c_tie - ties_left, 0.0)) > 0.0)
    def _():
        def tb(i, pidx):
            cand = jnp.broadcast_to(pidx | jnp.left_shift(jnp.int32(1), idx_bits - 1 - i), (Q_PAD, LANES))
            c = count(lambda t, c0: jnp.where(t == thr_b, jnp.where(lane_pos + c0 < cand, 1.0, 0.0), 0.0))
            return jnp.where(c < ties_left, cand[:, :1], pidx)

        p_ref[...] = lax.fori_loop(0, idx_bits, tb, jnp.zeros((Q_PAD, 1), I32))

    p_last = p_ref[...]
    t = s_ref[...]
    keep_tie = jnp.where(key_pos <= p_last, 0.0, NEG_BIG)
    bias = jnp.where(t > thr, 0.0, jnp.where(t == thr, keep_tie, NEG_BIG))
    bias = jnp.where(admissible, bias, NEG_BIG)

    wait_pages(1)
    s = jnp.dot(q_pad, kbuf[slot].astype(BF16), preferred_element_type=F32)
    s = s + jnp.concatenate([bias] * N_HEADS, axis=0)
    m = jnp.max(s, axis=1, keepdims=True)
    pexp = jnp.exp2(s - m)
    l = jnp.sum(pexp, axis=1, keepdims=True)
    wait_pages(2)
    o = lax.dot_general(pexp.astype(BF16), vbuf[slot].astype(BF16), (((1,), (1,)), ((), ())),
                        preferred_element_type=F32)
    o_ref[...] = o / l


def sample_attention(page_table, q, iq, ikw, k_new, v_new, cache_k, cache_v, cache_ik):
    n, n_new, _ = q.shape
    assert n_new <= Q_PAD
    n_pages = page_table.shape[1]
    past = n_pages * PAGE_SIZE
    lk = past + LANES
    n_sel = min(TOPK_MAX, (past + n_new) // 4)
    idx_bits = max(1, (lk - 1).bit_length())
    body = functools.partial(_sattn_body, n_pages=n_pages, n_sel=n_sel, idx_bits=idx_bits)
    pad = lambda a: jnp.pad(a, ((0, 0), (0, Q_PAD - n_new), (0, 0)))
    q, iq, ikw, k_new, v_new = pad(q), pad(iq), pad(ikw), pad(k_new), pad(v_new)
    seq = lambda w_: pl.BlockSpec((None, Q_PAD, w_), lambda b, pt: (b, 0, 0))
    hbm = pl.BlockSpec(memory_space=pl.ANY)
    grid_spec = pltpu.PrefetchScalarGridSpec(
        num_scalar_prefetch=1,
        grid=(n,),
        in_specs=[seq(q.shape[2]), seq(iq.shape[2]), seq(LANES), seq(LANES), seq(LANES), seq(LANES), hbm, hbm, hbm],
        out_specs=pl.BlockSpec((None, N_HEADS * Q_PAD, LANES), lambda b, pt: (b, 0, 0)),
        scratch_shapes=[pltpu.VMEM((2, LANES, lk), F32),
                        pltpu.VMEM((2, LANES, lk), F32),
                        pltpu.VMEM((2, IDX_HEAD_DIM, lk), F32),
                        pltpu.VMEM((Q_PAD, lk), I32),
                        pltpu.VMEM((Q_PAD, 1), I32),
                        pltpu.SemaphoreType.DMA((2, 3))],
    )
    raw = pl.pallas_call(
        body,
        grid_spec=grid_spec,
        out_shape=jax.ShapeDtypeStruct((n, N_HEADS * Q_PAD, LANES), F32),
        compiler_params=_cparams("arbitrary"),
        name="sample_attention",
    )(page_table, q, iq, ikw, k_new, v_new, ikw, cache_k, cache_v, cache_ik)
    raw = raw.reshape(n, N_HEADS, Q_PAD, N_KV_HEADS, HEAD_DIM)[:, :, :n_new]
    per_head = jnp.stack([raw[:, h, :, h // HEADS_PER_KV] for h in range(N_HEADS)], axis=2)
    return per_head.reshape(n, n_new, N_HEADS * HEAD_DIM)


def _rglru_body(xr_ref, yg_ref, cprev_ref, hprev_ref, cw_ref, cb_ref, wa_ref, ba_ref, wx_ref, bx_ref, lam_ref,
                y_ref, hlast_ref, xbuf, a_buf, b_buf, h_buf, h_carry):
    i = pl.program_id(1)
    tt = xr_ref.shape[0]
    halo = SUBLANES

    @pl.when(i == 0)
    def _():
        xbuf[pl.ds(0, halo), :] = jnp.zeros((halo, xbuf.shape[1]), F32)
        xbuf[pl.ds(halo - (CONV_WIDTH - 1), CONV_WIDTH - 1), :] = cprev_ref[...]
        h_carry[...] = hprev_ref[...]

    x = xr_ref[...]
    xbuf[pl.ds(halo, tt), :] = x
    xc = cb_ref[...] + cw_ref[CONV_WIDTH - 1:CONV_WIDTH, :] * x
    for j in range(CONV_WIDTH - 1):
        xc = xc + cw_ref[j:j + 1, :] * xbuf[pl.ds(halo - (CONV_WIDTH - 1) + j, tt), :]
    xcb = xc.astype(BF16)
    r = _sigmoid(jnp.dot(xcb, wa_ref[...], preferred_element_type=F32) + ba_ref[...])
    ig = _sigmoid(jnp.dot(xcb, wx_ref[...], preferred_element_type=F32) + bx_ref[...])
    z = -lam_ref[...]
    softplus = jnp.maximum(z, 0.0) + jnp.log1p(jnp.exp(-jnp.abs(z)))
    log_a = -RG_C * r * softplus
    a = jnp.exp(log_a)
    a_buf[...] = a
    b_buf[...] = jnp.sqrt(jnp.tanh(-log_a) * (1.0 + a * a)) * (ig * xc)

    def step(t, h):
        h = a_buf[pl.ds(t, 1), :] * h + b_buf[pl.ds(t, 1), :]
        h_buf[pl.ds(t, 1), :] = h
        return h

    h_fin = lax.fori_loop(0, tt, step, h_carry[...], unroll=min(8, tt))
    h_carry[...] = h_fin
    hlast_ref[...] = h_fin
    y_ref[...] = h_buf[...] * jax.nn.gelu(yg_ref[...])
    xbuf[pl.ds(halo - (CONV_WIDTH - 1), CONV_WIDTH - 1), :] = xbuf[pl.ds(halo + tt - (CONV_WIDTH - 1), CONV_WIDTH - 1), :]


def _block_diag(w):
    k, c, d = w.shape
    eye = jnp.eye(k, dtype=w.dtype)
    return (eye[:, None, :, None] * w[:, :, None, :]).reshape(k * c, k * d)


def rglru(xr, yg, conv_prev, h_prev, conv_w, conv_b, rg_wa, rg_ba, rg_wx, rg_bx, rg_lambda, tt):
    n, t, c = xr.shape
    row = pl.BlockSpec((None, tt, c), lambda b, i: (b, i, 0))
    const = lambda shape: pl.BlockSpec(shape, lambda b, i: (0,) * len(shape))
    vec = lambda a: a.reshape(1, c)
    return pl.pallas_call(
        _rglru_body,
        grid=(n, t // tt),
        in_specs=[row, row,
                  pl.BlockSpec((None, CONV_WIDTH - 1, c), lambda b, i: (b, 0, 0)),
                  pl.BlockSpec((None, 1, c), lambda b, i: (b, 0, 0)),
                  const((CONV_WIDTH, c)), const((1, c)), const((c, c)), const((1, c)), const((c, c)),
                  const((1, c)), const((1, c))],
        out_specs=[row, pl.BlockSpec((None, 1, c), lambda b, i: (b, 0, 0))],
        out_shape=[jax.ShapeDtypeStruct((n, t, c), F32), jax.ShapeDtypeStruct((n, 1, c), F32)],
        scratch_shapes=[pltpu.VMEM((tt + SUBLANES, c), F32), pltpu.VMEM((tt, c), F32), pltpu.VMEM((tt, c), F32),
                        pltpu.VMEM((tt, c), F32), pltpu.VMEM((1, c), F32)],
        compiler_params=_cparams("parallel", "arbitrary"),
        name="rglru",
    )(xr, yg, conv_prev, h_prev.reshape(n, 1, c), conv_w, vec(conv_b), _block_diag(rg_wa).astype(BF16), vec(rg_ba),
      _block_diag(rg_wx).astype(BF16), vec(rg_bx), vec(rg_lambda))


def _pack_bf16_pairs(x):
    c = x.shape[1] // 2
    a = pltpu.bitcast(x[:, :c].astype(BF16).astype(F32), U32)
    b = pltpu.bitcast(x[:, c:].astype(BF16).astype(F32), U32)
    return (a & jnp.uint32(0xFFFF0000)) | (b >> 16)


def _unpack_bf16_pairs(u):
    a = pltpu.bitcast(u & jnp.uint32(0xFFFF0000), F32)
    b = pltpu.bitcast(u << 16, F32)
    return jnp.concatenate([a, b], axis=1).astype(BF16)


def _out_router_body(attn_ref, rnn_ref, x_ref, g1_ref, sc2_ref, sh2_ref, ag_ref, rg_ref, wo_ref, n2_ref,
                     rw_ref, rb_ref, tri_ref, x1_ref, h2_ref, gate_ref, code_ref, count_ref):
    mixed = jnp.concatenate([_rms(attn_ref[...], ag_ref[...]), _rms(rnn_ref[...], rg_ref[...])], axis=1)
    x1 = x_ref[...] + g1_ref[...] * jnp.dot(mixed.astype(BF16), wo_ref[...], preferred_element_type=F32)
    x1_ref[...] = x1
    h2 = _rms(x1, n2_ref[...]) * (1.0 + sc2_ref[...]) + sh2_ref[...]
    h2_hi = h2.astype(BF16)
    h2_ref[...] = _pack_bf16_pairs(h2)
    h2_lo = (h2 - h2_hi.astype(F32)).astype(BF16)
    z = (jnp.dot(h2_hi, rw_ref[0], preferred_element_type=F32) + jnp.dot(h2_hi, rw_ref[1], preferred_element_type=F32)
         + jnp.dot(h2_lo, rw_ref[0], preferred_element_type=F32)) + rb_ref[...]
    lane = lax.broadcasted_iota(I32, z.shape, 1)
    neg_inf = -jnp.inf
    gl = jnp.where(lane < N_GROUPS, z, neg_inf)
    gmax = jnp.max(gl, axis=1, keepdims=True)
    gidx = jnp.min(jnp.where(gl == gmax, lane, LANES), axis=1, keepdims=True)
    gprob = 1.0 / jnp.sum(jnp.exp(gl - gmax), axis=1, keepdims=True)
    lo = EXPERT_LANE0 + EXPERTS_PER_GROUP * gidx
    el = jnp.where(lane >= lo, jnp.where(lane < lo + EXPERTS_PER_GROUP, z, neg_inf), neg_inf)
    m1 = jnp.max(el, axis=1, keepdims=True)
    j1 = jnp.min(jnp.where(el == m1, lane, LANES), axis=1, keepdims=True)
    el2 = jnp.where(lane == j1, neg_inf, el)
    m2 = jnp.max(el2, axis=1, keepdims=True)
    j2 = jnp.min(jnp.where(el2 == m2, lane, LANES), axis=1, keepdims=True)
    e21 = jnp.exp(m2 - m1)
    w1 = 1.0 / (1.0 + e21)
    gate_ref[...] = jnp.where(lane == j1, w1 * gprob, jnp.where(lane == j2, e21 * w1 * gprob, 0.0))
    tri = tri_ref[...]
    onehot = jnp.where(lane == gidx, 1.0, 0.0)
    ranks = jnp.dot(tri, onehot.astype(BF16), preferred_element_type=F32)
    counts = jnp.sum(onehot, axis=0, keepdims=True)
    blocks = jnp.floor((counts + (MOE_ROW_BLOCK - 0.5)) * (1.0 / MOE_ROW_BLOCK))
    blocks8 = jnp.broadcast_to(blocks, (SUBLANES, LANES)).astype(BF16)
    below = jnp.where(lax.broadcasted_iota(I32, (LANES, LANES), 0) < lax.broadcasted_iota(I32, (LANES, LANES), 1),
                      1.0, 0.0).astype(BF16)
    starts = jnp.dot(blocks8, below, preferred_element_type=F32)[0:1, :] * MOE_ROW_BLOCK
    dst = jnp.sum((ranks + starts) * onehot, axis=1, keepdims=True)
    code_ref[...] = jnp.where(lane == 0, dst, 0.0).T[0:1, :].astype(I32)
    count_ref[...] = starts.astype(I32)


def out_proj_router(attn, rnn, x, g1, sc2, sh2, attn_g, rnn_g, w_out, norm2_g, router_w, router_b, tt):
    n, t, d = x.shape
    c = attn.shape[2]
    per_row = g1.shape[1] != 1
    mod_spec = (pl.BlockSpec((None, tt, d), lambda b, i: (b, i, 0)) if per_row
                else pl.BlockSpec((None, 1, d), lambda b, i: (b, 0, 0)))
    row = lambda w_: pl.BlockSpec((None, tt, w_), lambda b, i: (b, i, 0))
    const = lambda shape: pl.BlockSpec(shape, lambda b, i: (0,) * len(shape))
    nt = t // tt
    per_tile = lambda w_: pl.BlockSpec((None, 1, w_), lambda b, i: (b * nt + i, 0, 0))
    tri = jnp.tril(jnp.ones((tt, tt), BF16), -1)
    return pl.pallas_call(
        _out_router_body,
        grid=(n, nt),
        in_specs=[row(c), row(c), row(d), mod_spec, mod_spec, mod_spec, const((1, c)), const((1, c)),
                  const(w_out.shape), const((1, d)), const(router_w.shape), const((1, LANES)), const((tt, tt))],
        out_specs=[row(d), row(d // 2), row(LANES), per_tile(tt), per_tile(LANES)],
        out_shape=[jax.ShapeDtypeStruct((n, t, d), F32), jax.ShapeDtypeStruct((n, t, d // 2), U32),
                   jax.ShapeDtypeStruct((n, t, LANES), F32), jax.ShapeDtypeStruct((n * nt, 1, tt), I32),
                   jax.ShapeDtypeStruct((n * nt, 1, LANES), I32)],
        compiler_params=_cparams("parallel", "arbitrary"),
        name="out_proj_router",
    )(attn, rnn, x, g1, sc2, sh2, attn_g.reshape(1, c), rnn_g.reshape(1, c), w_out, norm2_g.reshape(1, d),
      router_w, router_b, tri)


def split_bf16(w):
    hi = w.astype(BF16)
    return jnp.stack([hi, (w - hi.astype(F32)).astype(BF16)])


def _moe_body(dst_ref, start_ref, h2_ref, gate_ref, x1_ref, g2_ref, wgu_ref, wd_ref, fg_ref, y_ref,
              hs_ref, gs_ref, acc_ref):
    step = pl.program_id(2)
    tt = h2_ref.shape[0]
    d_exp = wd_ref.shape[1]
    rb = MOE_ROW_BLOCK

    @pl.when(step == 0)
    def _():
        hs_ref[...] = jnp.zeros(hs_ref.shape, U32)
        gs_ref[...] = jnp.zeros(gs_ref.shape, F32)

        def move_in(t, c):
            dst = dst_ref[0, t]
            hs_ref[pl.ds(dst, 1), :] = h2_ref[pl.ds(t, 1), :]
            gs_ref[pl.ds(dst, 1), :] = gate_ref[pl.ds(t, 1), :]
            return c

        lax.fori_loop(0, tt, move_in, 0, unroll=8)
        acc_ref[...] = jnp.zeros(acc_ref.shape, F32)

    lane = lax.broadcasted_iota(I32, (rb, LANES), 1)
    first = step * MOE_EXPERTS_PER_STEP
    grp = first // EXPERTS_PER_GROUP

    def block(b, c):
        rows = pl.ds(pl.multiple_of(b * rb, rb), rb)
        h = _unpack_bf16_pairs(hs_ref[rows, :])
        gates = gs_ref[rows, :]
        acc = acc_ref[rows, :]
        for k in range(MOE_EXPERTS_PER_STEP):
            gcol = jnp.sum(jnp.where(lane == first + k + EXPERT_LANE0, gates, 0.0), axis=1, keepdims=True)
            gu = jnp.dot(h, wgu_ref[k], preferred_element_type=F32)
            gpart, upart = gu[:, :d_exp], gu[:, d_exp:]
            act = (gpart * _sigmoid(gpart) * upart).astype(BF16)
            acc = acc + gcol * jnp.dot(act, wd_ref[k], preferred_element_type=F32)
        acc_ref[rows, :] = acc
        return c

    lax.fori_loop(start_ref[0, grp] // rb, start_ref[0, grp + 1] // rb, block, 0)

    @pl.when(step == pl.num_programs(2) - 1)
    def _():
        def move_out(t, c):
            y_ref[pl.ds(t, 1), :] = acc_ref[pl.ds(dst_ref[0, t], 1), :]
            return c

        lax.fori_loop(0, tt, move_out, 0, unroll=8)
        x2 = x1_ref[...] + g2_ref[...] * y_ref[...]
        y_ref[...] = _rms(x2, fg_ref[...])


def moe_final(codes, counts, h2, gates, x1, g2, w_gu, w_down, final_g, tt):
    n, t, d = x1.shape
    n_exp = w_gu.shape[0]
    nt = t // tt
    per_row = g2.shape[1] != 1
    mod_spec = (pl.BlockSpec((None, tt, d), lambda b, i, e: (b, i, 0)) if per_row
                else pl.BlockSpec((None, 1, d), lambda b, i, e: (b, 0, 0)))
    row = lambda w_: pl.BlockSpec((None, tt, w_), lambda b, i, e: (b, i, 0))
    smem_tile = lambda w_: pl.BlockSpec((None, 1, w_), lambda b, i, e: (b * nt + i, 0, 0), memory_space=pltpu.SMEM)
    sort_rows = tt + N_GROUPS * MOE_ROW_BLOCK
    eps = MOE_EXPERTS_PER_STEP
    return pl.pallas_call(
        _moe_body,
        grid=(n, nt, n_exp // eps),
        in_specs=[smem_tile(tt), smem_tile(LANES), row(d // 2), row(LANES), row(d), mod_spec,
                  pl.BlockSpec((eps,) + w_gu.shape[1:], lambda b, i, e: (e, 0, 0)),
                  pl.BlockSpec((eps,) + w_down.shape[1:], lambda b, i, e: (e, 0, 0)),
                  pl.BlockSpec((1, d), lambda b, i, e: (0, 0))],
        out_specs=row(d),
        out_shape=jax.ShapeDtypeStruct((n, t, d), F32),
        scratch_shapes=[pltpu.VMEM((sort_rows, d // 2), U32),
                        pltpu.VMEM((sort_rows, LANES), F32),
                        pltpu.VMEM((sort_rows, d), F32)],
        compiler_params=pltpu.CompilerParams(dimension_semantics=("parallel", "arbitrary", "arbitrary"),
                                             vmem_limit_bytes=VMEM_LIMIT_MOE),
        name="moe_final",
    )(codes, counts, h2, gates, x1, g2, w_gu, w_down, final_g.reshape(1, d))


def _tile(t, pref):
    return pref if t % pref == 0 else t


def _layer(x, mods, pos, attend, conv_prev, h_prev, n_seq, lw, final_g):
    sh1, sc1, g1, sh2, sc2, g2 = mods
    n, t, d = x.shape
    t_seq = n * t // n_seq
    seqs = lambda a: a.reshape(n_seq, t_seq, a.shape[-1])
    q, k, v, kb, vt, iq, ikw, ikb, xr, yg = in_proj(x, sc1, sh1, lw["norm1_g"], lw["w_in"], rope_tables(pos),
                                                     _tile(t, 512))
    attn = attend(q, iq, ikw, k, v, kb, vt, ikb)
    rnn, h_new = rglru(seqs(xr), seqs(yg), conv_prev, h_prev, lw["conv_w"], lw["conv_b"], lw["rg_wa"], lw["rg_ba"],
                       lw["rg_wx"], lw["rg_bx"], lw["rg_lambda"], _tile(t_seq, 512))
    moe_tile = _tile(t, 1024)
    x1, h2, gates, codes, counts = out_proj_router(
        attn, rnn.reshape(xr.shape), x, g1, sc2, sh2, lw["attn_out_g"], lw["rnn_out_g"], lw["w_out"], lw["norm2_g"],
        lw["router_w"], lw["router_b"], moe_tile)
    y = moe_final(codes, counts, h2, gates, x1, g2, lw["w_gu"], lw["w_down"], final_g, moe_tile)
    conv_new = seqs(xr)[:, t_seq - (CONV_WIDTH - 1):, :]
    return (seqs(y), seqs(k).reshape(n_seq, t_seq, N_KV_HEADS, HEAD_DIM),
            seqs(v).reshape(n_seq, t_seq, N_KV_HEADS, HEAD_DIM), seqs(ikw)[:, :, :IDX_HEAD_DIM], conv_new,
            h_new[:, 0, :])


def kernel(x_prompt, x_sample, cache_k, cache_v, cache_idx_k, state_conv, state_rglru, page_table, c_prompt,
           c_sample, ada_w, ada_b, norm1_g, w_in, conv_w, conv_b, rg_wa, rg_ba, rg_wx, rg_bx, rg_lambda, attn_out_g,
           rnn_out_g, w_out, norm2_g, router_grp_w, router_grp_b, router_exp_w, router_exp_b, exp_w_gate, exp_w_up,
           exp_w_down, final_g):
    depth = ada_w.shape[0]
    assert depth == 1, "this implementation handles the single-layer configuration"
    n_p, t_p, d = x_prompt.shape
    n_s, t_s, _ = x_sample.shape
    n_pages = page_table.shape[1]
    past = n_pages * PAGE_SIZE
    l = 0
    router_w = jnp.concatenate([router_grp_w[l], router_exp_w[l],
                                jnp.zeros((d, LANES - N_GROUPS - N_EXPERTS), F32)], axis=1)
    router_b = jnp.concatenate([router_grp_b[l], router_exp_b[l],
                                jnp.zeros((LANES - N_GROUPS - N_EXPERTS,), F32)]).reshape(1, LANES)
    lw = dict(norm1_g=norm1_g[l], w_in=prep_w_in(w_in[l]), conv_w=conv_w[l], conv_b=conv_b[l], rg_wa=rg_wa[l],
              rg_ba=rg_ba[l], rg_wx=rg_wx[l], rg_bx=rg_bx[l], rg_lambda=rg_lambda[l], attn_out_g=attn_out_g[l],
              rnn_out_g=rnn_out_g[l], w_out=w_out[l].astype(BF16), norm2_g=norm2_g[l],
              router_w=split_bf16(router_w), router_b=router_b,
              w_gu=jnp.concatenate([exp_w_gate[l], exp_w_up[l]], axis=2).astype(BF16),
              w_down=exp_w_down[l].astype(BF16))

    mod = ada_mod(jnp.concatenate([c_prompt, c_sample], axis=0), ada_w[l], ada_b[l])
    mods_p = [m[:n_p, None, :] for m in jnp.split(mod, 6, axis=1)]
    mods_s = [jnp.repeat(m[n_p:], t_s, axis=0)[None] for m in jnp.split(mod, 6, axis=1)]

    def attend_prompt(q, iq, ikw, k, v, kb, vt, ikb):
        iw_t = jnp.swapaxes(ikw[:, :, IDX_HEAD_DIM:IDX_HEAD_DIM + N_IDX_HEADS], 1, 2)
        return prompt_attention(q, iq, iw_t, kb, vt, ikb, _tile(t_p, 256))

    def attend_sample(q, iq, ikw, k, v, kb, vt, ikb):
        seqs = lambda a: a.reshape(n_s, t_s, a.shape[-1])
        n_pool = cache_k.shape[1]
        feat_major = lambda c: jnp.swapaxes(c.reshape(n_pool, PAGE_SIZE, -1), 1, 2)
        out = sample_attention(page_table, seqs(q), seqs(iq), seqs(ikw), seqs(k), seqs(v),
                               feat_major(cache_k[l]), feat_major(cache_v[l]), feat_major(cache_idx_k[l]))
        return out.reshape(1, n_s * t_s, N_HEADS * HEAD_DIM)

    c_rnn = conv_w.shape[2]
    ys, ks, vs, iks, cvs, hs = _layer(
        x_sample.reshape(1, n_s * t_s, d), mods_s, jnp.tile(past + jnp.arange(t_s), n_s), attend_sample,
        state_conv[l], state_rglru[l], n_s, lw, final_g)
    yp, kp, vp, ikp, cvp, hp = _layer(
        x_prompt, mods_p, jnp.arange(t_p), attend_prompt,
        jnp.zeros((n_p, CONV_WIDTH - 1, c_rnn), F32), jnp.zeros((n_p, c_rnn), F32), n_p, lw, final_g)

    st = lambda a: a[None]
    return (yp, ys, st(kp), st(vp), st(ikp), st(cvp), st(hp), st(ks), st(vs), st(iks), st(cvs), st(hs))
```

```python
import functools

import jax
import jax.numpy as jnp
from jax import lax
from jax.experimental import pallas as pl
from jax.experimental.pallas import tpu as pltpu

F32, BF16, I32, I16, U32 = jnp.float32, jnp.bfloat16, jnp.int32, jnp.int16, jnp.uint32
HIGHEST = lax.Precision.HIGHEST

HEAD_DIM = 64
N_HEADS = 8
N_KV_HEADS = 2
HEADS_PER_KV = N_HEADS // N_KV_HEADS
N_IDX_HEADS = 8
IDX_HEAD_DIM = 32
TOPK_MAX = 256
ROPE_THETA = 10000.0
CONV_WIDTH = 4
RG_BLOCKS = 8
RG_C = 8.0
N_GROUPS = 4
EXPERTS_PER_GROUP = 8
N_EXPERTS = N_GROUPS * EXPERTS_PER_GROUP
PAGE_SIZE = 128
NORM_EPS = 1e-6

LANES = 128
SUBLANES = 8
VMEM_LIMIT = 48 * 1024 * 1024
VMEM_LIMIT_MOE = 54 * 1024 * 1024
INT_MIN = -(2 ** 31)
I16_MIN = -(2 ** 15)
NEG_BIG = -1e30
EXPERT_LANE0 = N_GROUPS
BF16_SUBLANES = 16
V_AUG_ROWS = HEAD_DIM + BF16_SUBLANES
ATT_CHUNK_HEADS = 4
COUNT_TILES = 2
MOE_ROW_BLOCK = 304
MOE_EXPERTS_PER_STEP = 4
LOG2_E = 1.4426950408889634
Q_SCALE = HEAD_DIM ** -0.5 * LOG2_E


def _cparams(*sem):
    return pltpu.CompilerParams(dimension_semantics=sem, vmem_limit_bytes=VMEM_LIMIT)


def _rms(x, g):
    return x * lax.rsqrt(jnp.mean(x * x, axis=-1, keepdims=True) + NORM_EPS) * g


def _sigmoid(x):
    return 1.0 / (1.0 + jnp.exp(-x))


def _float_key(s):
    b = pltpu.bitcast(s + 0.0, I32)
    return b ^ ((b >> 31) & 0x7FFFFFFF)


def _ada_body(c_ref, w_ref, b_ref, o_ref):
    c = c_ref[...]
    s = c * _sigmoid(c)
    o_ref[...] = jnp.dot(s, w_ref[...], preferred_element_type=F32, precision=HIGHEST) + b_ref[...]


def ada_mod(c, w, b):
    m, d = c.shape
    n_chunks = w.shape[1] // d
    return pl.pallas_call(
        _ada_body,
        grid=(n_chunks,),
        in_specs=[pl.BlockSpec((m, d), lambda j: (0, 0)),
                  pl.BlockSpec((d, d), lambda j: (0, j)),
                  pl.BlockSpec((1, d), lambda j: (0, j))],
        out_specs=pl.BlockSpec((m, d), lambda j: (0, j)),
        out_shape=jax.ShapeDtypeStruct((m, w.shape[1]), F32),
        compiler_params=_cparams("arbitrary"),
        name="ada_mod",
    )(c, w, b.reshape(1, -1))


_C_Q, _C_K, _C_V, _C_IQ, _C_IKW, _C_XR, _C_YG, _C_END = (0, 512, 640, 768, 1024, 1152, 1664, 2176)


def prep_w_in(w_in):
    d_in = w_in.shape[0]
    q, k, v, iq, ik, iw, xr, yg = jnp.split(w_in, [512, 640, 768, 1024, 1056, 1064, 1576], axis=1)
    ikw = jnp.concatenate([ik, iw, jnp.zeros((d_in, LANES - IDX_HEAD_DIM - N_IDX_HEADS), w_in.dtype)], axis=1)
    return jnp.concatenate([q, k, v, iq, ikw, xr, yg], axis=1).astype(BF16)


def rope_tables(pos):
    pos = pos.astype(F32)[:, None]
    lane = jnp.arange(LANES)[None, :]

    def cs(head_dim):
        half = head_dim // 2
        inv = ROPE_THETA ** (-jnp.arange(half, dtype=F32) / half)
        ang = pos * inv[None, :]
        reps = LANES // head_dim
        cos, sin = jnp.tile(jnp.cos(ang), (1, 2 * reps)), jnp.tile(jnp.sin(ang), (1, 2 * reps))
        return cos, jnp.where(lane % head_dim < half, -sin, sin)

    cq, sq = cs(HEAD_DIM)
    ci, si = cs(IDX_HEAD_DIM)
    w_scale = (IDX_HEAD_DIM ** -0.5) * (N_IDX_HEADS ** -0.5)
    cx = jnp.where(lane < IDX_HEAD_DIM, ci, jnp.where(lane < IDX_HEAD_DIM + N_IDX_HEADS, w_scale, 0.0))
    sx = jnp.where(lane < IDX_HEAD_DIM, si, 0.0)
    return cq, sq, ci, si, cx.astype(F32), sx.astype(F32)


def _rope(x, cos, sin_signed, head_dim):
    half = head_dim // 2
    out = []
    for c in range(x.shape[1] // LANES):
        xs = x[:, c * LANES:(c + 1) * LANES]
        lane = lax.broadcasted_iota(I32, xs.shape, 1)
        partner = jnp.where(lane % head_dim < half, pltpu.roll(xs, LANES - half, axis=1), pltpu.roll(xs, half, axis=1))
        out.append(xs * cos + partner * sin_signed)
    return out[0] if len(out) == 1 else jnp.concatenate(out, axis=1)


def _in_proj_body(x_ref, sc_ref, sh_ref, g_ref, w_ref, cq_ref, sq_ref, ci_ref, si_ref, cx_ref, sx_ref,
                  q_ref, k_ref, v_ref, kb_ref, vt_ref, iq_ref, ikw_ref, ikb_ref, xr_ref, yg_ref, ik_ref):
    x = x_ref[...]
    h = (_rms(x, g_ref[...]) * (1.0 + sc_ref[...]) + sh_ref[...]).astype(BF16)

    def seg(a, b):
        return jnp.dot(h, w_ref[:, a:b], preferred_element_type=F32)

    cq, sq = cq_ref[...], sq_ref[...]
    q_ref[...] = (_rope(seg(_C_Q, _C_K), cq, sq, HEAD_DIM) * Q_SCALE).astype(BF16)
    k = _rope(seg(_C_K, _C_V), cq, sq, HEAD_DIM)
    k_ref[...] = k
    kb_ref[...] = k.astype(BF16)
    v = seg(_C_V, _C_IQ)
    v_ref[...] = v
    vt_ref[...] = v.T.astype(BF16)
    iq_ref[...] = _rope(seg(_C_IQ, _C_IKW), ci_ref[...], si_ref[...], IDX_HEAD_DIM).astype(BF16)
    ikw = _rope(seg(_C_IKW, _C_XR), cx_ref[...], sx_ref[...], IDX_HEAD_DIM)
    ikw_ref[...] = ikw
    ikb_ref[...] = ikw.astype(BF16)
    ik_ref[...] = ikw[:, :IDX_HEAD_DIM]
    xr_ref[...] = seg(_C_XR, _C_YG)
    yg_ref[...] = seg(_C_YG, _C_END)


def in_proj(x, sc, sh, g, w, tables, tt):
    n, t, d = x.shape
    per_row = sc.shape[1] != 1
    mod_spec = (pl.BlockSpec((None, tt, d), lambda b, i: (b, i, 0)) if per_row
                else pl.BlockSpec((None, 1, d), lambda b, i: (b, 0, 0)))
    row = lambda w_: pl.BlockSpec((None, tt, w_), lambda b, i: (b, i, 0))
    tab = pl.BlockSpec((tt, LANES), lambda b, i: (i, 0))
    outs = [(512, BF16), (128, F32), (128, F32), (128, BF16), None, (256, BF16), (128, F32), (128, BF16),
            (512, F32), (512, F32), (IDX_HEAD_DIM, F32)]
    out_shape, out_specs = [], []
    for o in outs:
        if o is None:
            out_shape.append(jax.ShapeDtypeStruct((n, LANES, t), BF16))
            out_specs.append(pl.BlockSpec((None, LANES, tt), lambda b, i: (b, 0, i)))
        else:
            out_shape.append(jax.ShapeDtypeStruct((n, t, o[0]), o[1]))
            out_specs.append(row(o[0]))
    return pl.pallas_call(
        _in_proj_body,
        grid=(n, t // tt),
        in_specs=[row(d), mod_spec, mod_spec,
                  pl.BlockSpec((1, d), lambda b, i: (0, 0)),
                  pl.BlockSpec(w.shape, lambda b, i: (0, 0))] + [tab] * 6,
        out_specs=out_specs,
        out_shape=out_shape,
        compiler_params=_cparams("parallel", "arbitrary"),
        name="in_proj",
    )(x, sc, sh, g.reshape(1, d), w, *tables)


def _pattn_body(q_ref, iq_ref, iwt_ref, k_ref, vt_ref, ik_ref, o_ref,
                hi_ref, lo_ref, lb_ref, p_ref, m_ref, acc_ref, mm_ref, *, n_sel, idx_bits):
    qi = pl.program_id(1)
    tq = q_ref.shape[0]
    tk = tq
    n_tiles = qi + 1

    iq_t = iq_ref[...].astype(F32).T
    iq_st = jnp.concatenate([iq_t[IDX_HEAD_DIM * h:IDX_HEAD_DIM * (h + 1)] for h in range(N_IDX_HEADS)],
                            axis=1).astype(BF16)
    w_t = iwt_ref[...]
    q_t = q_ref[...].astype(F32).T
    zero_half = jnp.zeros((HEAD_DIM, tq), F32)
    q_pad = []
    for g in range(N_KV_HEADS):
        cols = []
        for hh in range(HEADS_PER_KV):
            h = g * HEADS_PER_KV + hh
            qh = q_t[HEAD_DIM * h:HEAD_DIM * (h + 1)]
            cols.append(jnp.concatenate([qh, zero_half] if g == 0 else [zero_half, qh], axis=0))
        q_pad.append(jnp.concatenate(cols, axis=1).astype(BF16))

    row_s = lax.broadcasted_iota(I32, (tk, tq), 0)
    col_t = lax.broadcasted_iota(I32, (tk, tq), 1)
    causal = row_s <= col_t

    def tile_off(j):
        return pl.multiple_of(j * tk, tk)

    def walk_tiles(step):
        def pair(p, c):
            step(2 * p, 0, False)
            step(2 * p + 1, 1, False)
            return c

        lax.fori_loop(0, qi // 2, pair, 0)

        @pl.when(qi % 2 == 0)
        def _():
            step(qi, 0, True)

        @pl.when(qi % 2 == 1)
        def _():
            step(qi - 1, 0, False)
            step(qi, 1, True)

    def index_dots(j, slot):
        ik = ik_ref[pl.ds(tile_off(j), tk), :][:, :IDX_HEAD_DIM]
        mm_ref[slot] = jnp.dot(ik, iq_st, preferred_element_type=F32)

    def p1(j, slot, last):
        if not last:
            index_dots(j + 1, 1 - slot)
        sc = w_t[0:1, :] * jnp.maximum(mm_ref[slot, :, 0:tq], 0.0)
        for h in range(1, N_IDX_HEADS):
            sc = sc + w_t[h:h + 1, :] * jnp.maximum(mm_ref[slot, :, h * tq:(h + 1) * tq], 0.0)
        key = _float_key(sc)
        if last:
            key = jnp.where(causal, key, INT_MIN)
        rows = pl.ds(tile_off(j), tk)
        hi_ref[rows, :] = (key >> 16).astype(I16)
        lo_ref[rows, :] = ((key & 0xFFFF) + I16_MIN).astype(I16)

    index_dots(0, 0)
    walk_tiles(p1)

    span = COUNT_TILES * tk
    n_spans = (n_tiles + COUNT_TILES - 1) // COUNT_TILES
    groups = span // BF16_SUBLANES
    pad_rows = pl.ds(tile_off(n_tiles), span - tk)
    hi_ref[pad_rows, :] = jnp.full((span - tk, tq), I16_MIN, I16)
    lo_ref[pad_rows, :] = jnp.full((span - tk, tq), I16_MIN, I16)

    def rows16(v):
        return jnp.broadcast_to(v.astype(I16), (BF16_SUBLANES, tq))[None]

    def span_rows(j):
        return pl.ds(pl.multiple_of(j * span, span), span)

    def tile16(ref, j):
        return ref[span_rows(j), :].reshape(groups, BF16_SUBLANES, tq)

    one16, zero16 = jnp.int16(1), jnp.int16(0)

    def count(pred):
        def body(j, acc):
            marks = pred(j)
            parts = [marks[g] for g in range(groups)]
            while len(parts) > 1:
                parts = [parts[i] + parts[i + 1] for i in range(0, len(parts), 2)]
            return acc + parts[0]

        acc = lax.fori_loop(0, n_spans, body, jnp.zeros((BF16_SUBLANES, tq), I16))
        return jnp.sum(acc.astype(I32).astype(F32), axis=0, keepdims=True)

    def bisect16(ref, need):
        def step(b, thr):
            cand = thr + jnp.left_shift(jnp.int32(1), 15 - b)
            cand_b = rows16(cand)
            c = count(lambda j: jnp.where(tile16(ref, j) >= cand_b, one16, zero16))
            return jnp.where(c >= need, cand, thr)
        return lax.fori_loop(0, 16, step, jnp.full((1, tq), I16_MIN, I32))

    thr_hi = bisect16(hi_ref, n_sel)
    hi_b = rows16(thr_hi)
    c_above = count(lambda j: jnp.where(tile16(hi_ref, j) > hi_b, one16, zero16))

    def p2(j, c):
        rows = span_rows(j)
        lb_ref[rows, :] = jnp.where(hi_ref[rows, :] == thr_hi.astype(I16), lo_ref[rows, :], jnp.int16(I16_MIN))
        return c

    lax.fori_loop(0, n_spans, p2, 0)
    thr_lo = bisect16(lb_ref, n_sel - c_above)
    lo_b = rows16(thr_lo)
    c_gt = c_above + count(lambda j: jnp.where(tile16(lb_ref, j) > lo_b, one16, zero16))
    ties_left = n_sel - c_gt

    def is_tie(j):
        return jnp.where(tile16(hi_ref, j) == hi_b, jnp.where(tile16(lo_ref, j) == lo_b, one16, zero16), zero16)

    c_tie = count(is_tie)
    p_ref[...] = jnp.full((1, tq), 2 ** idx_bits - 1, I32)
    has_thr = jnp.where(thr_hi > I16_MIN, 1, jnp.where(thr_lo > I16_MIN, 1, 0))
    excess = jnp.max(jnp.where(has_thr > 0, c_tie - ties_left, 0.0)) > 0.0

    idx16 = row_s.astype(I16)

    @pl.when(excess)
    def _():
        sub = lax.broadcasted_iota(I32, (span, tq), 0).astype(I16).reshape(groups, BF16_SUBLANES, tq)

        def tb(b, p):
            cand = p | jnp.left_shift(jnp.int32(1), idx_bits - 1 - b)
            c = count(lambda j: jnp.where(sub < rows16(cand - j * span), is_tie(j), zero16))
            return jnp.where(c < ties_left, cand, p)

        p_ref[...] = lax.fori_loop(0, idx_bits, tb, jnp.zeros((1, tq), I32))

    p_last = p_ref[...]
    thr_hi16, thr_lo16 = thr_hi.astype(I16), thr_lo.astype(I16)

    m_ref[...] = jnp.full(m_ref.shape, NEG_BIG, F32)
    acc_ref[...] = jnp.zeros(acc_ref.shape, F32)
    ones_rows = jnp.ones((V_AUG_ROWS - HEAD_DIM, tk), BF16)

    w_c = ATT_CHUNK_HEADS * tq
    chunks = [(g, c * w_c) for g in range(N_KV_HEADS) for c in range(HEADS_PER_KV // ATT_CHUNK_HEADS)]

    def score_chunk(kt, slot, ci):
        g, c0 = chunks[ci]
        mm_ref[slot, :, ci * w_c:(ci + 1) * w_c] = jnp.dot(kt, q_pad[g][:, c0:c0 + w_c],
                                                             preferred_element_type=F32)

    kt0 = k_ref[pl.ds(0, tk), :]
    for ci in range(len(chunks)):
        score_chunk(kt0, 0, ci)

    def attend(j, slot, diag):
        off = tile_off(j)
        hi, lo = hi_ref[pl.ds(off, tk), :], lo_ref[pl.ds(off, tk), :]
        keep_tie = jnp.where(idx16 <= (p_last - j * tk).astype(I16), one16, zero16)
        sel = jnp.where(hi > thr_hi16, one16,
                        jnp.where(hi == thr_hi16,
                                  jnp.where(lo > thr_lo16, one16, jnp.where(lo == thr_lo16, keep_tie, zero16)),
                                  zero16))
        bias = jnp.where(sel.astype(I32) > 0, 0.0, NEG_BIG)
        if diag:
            bias = jnp.where(causal, bias, NEG_BIG)
        vt = vt_ref[:, pl.ds(off, tk)]
        v_aug = [jnp.concatenate([vt[HEAD_DIM * g:HEAD_DIM * (g + 1)], ones_rows], axis=0)
                 for g in range(N_KV_HEADS)]
        bias_c = jnp.concatenate([bias] * ATT_CHUNK_HEADS, axis=1)
        if not diag:
            kt_next = k_ref[pl.ds(tile_off(j + 1), tk), :]
        for ci, (g, c0) in enumerate(chunks):
            if not diag:
                score_chunk(kt_next, 1 - slot, ci)
            cols = slice(c0, c0 + w_c)
            s = mm_ref[slot, :, ci * w_c:(ci + 1) * w_c] + bias_c
            m_old = m_ref[g:g + 1, cols]
            m_new = jnp.maximum(m_old, jnp.max(s, axis=0, keepdims=True))
            alpha = jnp.exp2(m_old - m_new)
            p = jnp.exp2(s - m_new).astype(BF16)
            acc_ref[g, :, cols] = alpha * acc_ref[g, :, cols] + jnp.dot(v_aug[g], p, preferred_element_type=F32)
            m_ref[g:g + 1, cols] = m_new

    walk_tiles(attend)

    rows = []
    for g in range(N_KV_HEADS):
        acc = acc_ref[g]
        o_g = acc[:HEAD_DIM] / acc[HEAD_DIM:HEAD_DIM + 1]
        rows += [o_g[:, hh * tq:(hh + 1) * tq] for hh in range(HEADS_PER_KV)]
    o_ref[...] = jnp.concatenate(rows, axis=0).T


def prompt_attention(q, iq, iw_t, kb, vt, ikb, tq):
    n, t, _ = q.shape
    n_sel = min(TOPK_MAX, t // 4)
    idx_bits = max(1, (t - 1).bit_length())
    body = functools.partial(_pattn_body, n_sel=n_sel, idx_bits=idx_bits)
    return pl.pallas_call(
        body,
        grid=(n, t // tq),
        in_specs=[pl.BlockSpec((None, tq, q.shape[2]), lambda b, i: (b, i, 0)),
                  pl.BlockSpec((None, tq, iq.shape[2]), lambda b, i: (b, i, 0)),
                  pl.BlockSpec((None, N_IDX_HEADS, tq), lambda b, i: (b, 0, i)),
                  pl.BlockSpec((None, t, LANES), lambda b, i: (b, 0, 0)),
                  pl.BlockSpec((None, LANES, t), lambda b, i: (b, 0, 0)),
                  pl.BlockSpec((None, t, LANES), lambda b, i: (b, 0, 0))],
        out_specs=pl.BlockSpec((None, tq, N_HEADS * HEAD_DIM), lambda b, i: (b, i, 0)),
        out_shape=jax.ShapeDtypeStruct((n, t, N_HEADS * HEAD_DIM), F32),
        scratch_shapes=[pltpu.VMEM((t + (COUNT_TILES - 1) * tq, tq), I16),
                        pltpu.VMEM((t + (COUNT_TILES - 1) * tq, tq), I16),
                        pltpu.VMEM((t + (COUNT_TILES - 1) * tq, tq), I16),
                        pltpu.VMEM((1, tq), I32),
                        pltpu.VMEM((N_KV_HEADS, HEADS_PER_KV * tq), F32),
                        pltpu.VMEM((N_KV_HEADS, V_AUG_ROWS, HEADS_PER_KV * tq), F32),
                        pltpu.VMEM((2, tq, N_HEADS * tq), F32)],
        compiler_params=_cparams("parallel", "arbitrary"),
        name="prompt_attention",
    )(q, iq, iw_t, kb, vt, ikb)


Q_PAD = SUBLANES


def _sattn_body(pt_ref, q_ref, iq_ref, iw_ref, kn_ref, vn_ref, ikn_ref, ck_hbm, cv_hbm, cik_hbm, o_ref,
                kbuf, vbuf, ikbuf, s_ref, p_ref, sem, *, n_pages, n_sel, idx_bits):
    b = pl.program_id(0)
    slot = b % 2
    past = n_pages * PAGE_SIZE
    lk = past + LANES

    streams = ((cik_hbm, ikbuf), (ck_hbm, kbuf), (cv_hbm, vbuf))

    def page_copy(kind, p, seq, slt):
        src, dst = streams[kind]
        return pltpu.make_async_copy(src.at[pt_ref[seq, p]], dst.at[slt, :, pl.ds(p * PAGE_SIZE, PAGE_SIZE)],
                                     sem.at[slt, kind])

    def start_pages(seq, slt):
        for kind in range(len(streams)):
            for p in range(n_pages):
                page_copy(kind, p, seq, slt).start()

    def wait_pages(kind):
        for p in range(n_pages):
            page_copy(kind, p, b, slot).wait()

    @pl.when(b == 0)
    def _():
        start_pages(0, 0)

    @pl.when(b + 1 < pl.num_programs(0))
    def _():
        start_pages(b + 1, 1 - slot)

    def new_cols(ref):
        rows = jnp.concatenate([ref[...], jnp.zeros((LANES - Q_PAD, LANES), F32)], axis=0)
        return rows.T

    new_tile = pl.ds(past, LANES)
    kbuf[slot, :, new_tile] = new_cols(kn_ref)
    vbuf[slot, :, new_tile] = new_cols(vn_ref)
    ikbuf[slot, :, new_tile] = new_cols(ikn_ref)[:IDX_HEAD_DIM]

    iq = iq_ref[...].astype(F32)
    iq_st = jnp.concatenate([iq[:, IDX_HEAD_DIM * h:IDX_HEAD_DIM * (h + 1)] for h in range(N_IDX_HEADS)],
                            axis=0).astype(BF16)
    qf = q_ref[...].astype(F32)
    lane128 = lax.broadcasted_iota(I32, (Q_PAD, LANES), 1)
    q_rows = []
    for h in range(N_HEADS):
        g = h // HEADS_PER_KV
        slab = qf[:, LANES * (h // 2):LANES * (h // 2 + 1)]
        if (h % 2) != g:
            slab = pltpu.roll(slab, HEAD_DIM, axis=1)
        keep = (lane128 < HEAD_DIM) if g == 0 else (lane128 >= HEAD_DIM)
        q_rows.append(jnp.where(keep, slab, 0.0))
    q_pad = jnp.concatenate(q_rows, axis=0).astype(BF16)
    iw = iw_ref[...]

    key_pos = lax.broadcasted_iota(I32, (Q_PAD, lk), 1)
    q_row = lax.broadcasted_iota(I32, (Q_PAD, lk), 0)
    admissible = key_pos <= past + q_row

    wait_pages(0)
    d = jnp.dot(iq_st, ikbuf[slot].astype(BF16), preferred_element_type=F32)
    sc = None
    for h in range(N_IDX_HEADS):
        w_h = iw[:, IDX_HEAD_DIM + h:IDX_HEAD_DIM + h + 1]
        term = w_h * jnp.maximum(d[Q_PAD * h:Q_PAD * (h + 1)], 0.0)
        sc = term if sc is None else sc + term
    s_ref[...] = jnp.where(admissible, _float_key(sc), INT_MIN)

    lane_pos = lax.broadcasted_iota(I32, (Q_PAD, LANES), 1)

    def count(pred):
        terms = [pred(s_ref[:, c * LANES:(c + 1) * LANES], c * LANES) for c in range(lk // LANES)]
        while len(terms) > 1:
            terms = [terms[i] + terms[i + 1] if i + 1 < len(terms) else terms[i] for i in range(0, len(terms), 2)]
        return jnp.sum(terms[0], axis=1, keepdims=True)

    def counts3(c1, c2, c3):
        terms = [[], [], []]
        for c in range(lk // LANES):
            t = s_ref[:, c * LANES:(c + 1) * LANES]
            for k, cand in enumerate((c1, c2, c3)):
                terms[k].append(jnp.where(t >= cand, 1.0, 0.0))
        sums = []
        for ts in terms:
            while len(ts) > 1:
                ts = [ts[i] + ts[i + 1] if i + 1 < len(ts) else ts[i] for i in range(0, len(ts), 2)]
            sums.append(jnp.sum(ts[0], axis=1, keepdims=True))
        return sums

    def quaternary(i, thr):
        q = jnp.left_shift(jnp.int32(1), 30 - 2 * i)
        cands = [jnp.broadcast_to(thr + k * q, (Q_PAD, LANES)) for k in (1, 2, 3)]
        n1, n2, n3 = counts3(*cands)
        passed = (jnp.where(n1 >= n_sel, 1, 0) + jnp.where(n2 >= n_sel, 1, 0) + jnp.where(n3 >= n_sel, 1, 0))
        return thr + passed * q

    thr = lax.fori_loop(0, 16, quaternary, jnp.full((Q_PAD, 1), INT_MIN, I32))
    thr_b = jnp.broadcast_to(thr, (Q_PAD, LANES))
    ties_left = n_sel - count(lambda t, c0: jnp.where(t > thr_b, 1.0, 0.0))
    c_tie = count(lambda t, c0: jnp.where(t == thr_b, 1.0, 0.0))
    p_ref[...] = jnp.full((Q_PAD, 1), 2 ** idx_bits - 1, I32)

    @pl.when(jnp.max(jnp.where(thr > INT_MIN, c_tie - ties_left, 0.0)) > 0.0)
    def _():
        def tb(i, pidx):
            cand = jnp.broadcast_to(pidx | jnp.left_shift(jnp.int32(1), idx_bits - 1 - i), (Q_PAD, LANES))
            c = count(lambda t, c0: jnp.where(t == thr_b, jnp.where(lane_pos + c0 < cand, 1.0, 0.0), 0.0))
            return jnp.where(c < ties_left, cand[:, :1], pidx)

        p_ref[...] = lax.fori_loop(0, idx_bits, tb, jnp.zeros((Q_PAD, 1), I32))

    p_last = p_ref[...]
    t = s_ref[...]
    keep_tie = jnp.where(key_pos <= p_last, 0.0, NEG_BIG)
    bias = jnp.where(t > thr, 0.0, jnp.where(t == thr, keep_tie, NEG_BIG))
    bias = jnp.where(admissible, bias, NEG_BIG)

    wait_pages(1)
    s = jnp.dot(q_pad, kbuf[slot].astype(BF16), preferred_element_type=F32)
    s = s + jnp.concatenate([bias] * N_HEADS, axis=0)
    m = jnp.max(s, axis=1, keepdims=True)
    pexp = jnp.exp2(s - m)
    l = jnp.sum(pexp, axis=1, keepdims=True)
    wait_pages(2)
    o = lax.dot_general(pexp.astype(BF16), vbuf[slot].astype(BF16), (((1,), (1,)), ((), ())),
                        preferred_element_type=F32)
    o_ref[...] = o / l


def sample_attention(page_table, q, iq, ikw, k_new, v_new, cache_k, cache_v, cache_ik):
    n, n_new, _ = q.shape
    assert n_new <= Q_PAD
    n_pages = page_table.shape[1]
    past = n_pages * PAGE_SIZE
    lk = past + LANES
    n_sel = min(TOPK_MAX, (past + n_new) // 4)
    idx_bits = max(1, (lk - 1).bit_length())
    body = functools.partial(_sattn_body, n_pages=n_pages, n_sel=n_sel, idx_bits=idx_bits)
    pad = lambda a: jnp.pad(a, ((0, 0), (0, Q_PAD - n_new), (0, 0)))
    q, iq, ikw, k_new, v_new = pad(q), pad(iq), pad(ikw), pad(k_new), pad(v_new)
    seq = lambda w_: pl.BlockSpec((None, Q_PAD, w_), lambda b, pt: (b, 0, 0))
    hbm = pl.BlockSpec(memory_space=pl.ANY)
    grid_spec = pltpu.PrefetchScalarGridSpec(
        num_scalar_prefetch=1,
        grid=(n,),
        in_specs=[seq(q.shape[2]), seq(iq.shape[2]), seq(LANES), seq(LANES), seq(LANES), seq(LANES), hbm, hbm, hbm],
        out_specs=pl.BlockSpec((None, N_HEADS * Q_PAD, LANES), lambda b, pt: (b, 0, 0)),
        scratch_shapes=[pltpu.VMEM((2, LANES, lk), F32),
                        pltpu.VMEM((2, LANES, lk), F32),
                        pltpu.VMEM((2, IDX_HEAD_DIM, lk), F32),
                        pltpu.VMEM((Q_PAD, lk), I32),
                        pltpu.VMEM((Q_PAD, 1), I32),
                        pltpu.SemaphoreType.DMA((2, 3))],
    )
    raw = pl.pallas_call(
        body,
        grid_spec=grid_spec,
        out_shape=jax.ShapeDtypeStruct((n, N_HEADS * Q_PAD, LANES), F32),
        compiler_params=_cparams("arbitrary"),
        name="sample_attention",
    )(page_table, q, iq, ikw, k_new, v_new, ikw, cache_k, cache_v, cache_ik)
    raw = raw.reshape(n, N_HEADS, Q_PAD, N_KV_HEADS, HEAD_DIM)[:, :, :n_new]
    per_head = jnp.stack([raw[:, h, :, h // HEADS_PER_KV] for h in range(N_HEADS)], axis=2)
    return per_head.reshape(n, n_new, N_HEADS * HEAD_DIM)


def _rglru_body(xr_ref, yg_ref, cprev_ref, hprev_ref, cw_ref, cb_ref, wa_ref, ba_ref, wx_ref, bx_ref, lam_ref,
                y_ref, hlast_ref, xbuf, a_buf, b_buf, h_buf, h_carry):
    i = pl.program_id(1)
    tt = xr_ref.shape[0]
    halo = SUBLANES

    @pl.when(i == 0)
    def _():
        xbuf[pl.ds(0, halo), :] = jnp.zeros((halo, xbuf.shape[1]), F32)
        xbuf[pl.ds(halo - (CONV_WIDTH - 1), CONV_WIDTH - 1), :] = cprev_ref[...]
        h_carry[...] = hprev_ref[...]

    x = xr_ref[...]
    xbuf[pl.ds(halo, tt), :] = x
    xc = cb_ref[...] + cw_ref[CONV_WIDTH - 1:CONV_WIDTH, :] * x
    for j in range(CONV_WIDTH - 1):
        xc = xc + cw_ref[j:j + 1, :] * xbuf[pl.ds(halo - (CONV_WIDTH - 1) + j, tt), :]
    xcb = xc.astype(BF16)
    r = _sigmoid(jnp.dot(xcb, wa_ref[...], preferred_element_type=F32) + ba_ref[...])
    ig = _sigmoid(jnp.dot(xcb, wx_ref[...], preferred_element_type=F32) + bx_ref[...])
    z = -lam_ref[...]
    softplus = jnp.maximum(z, 0.0) + jnp.log1p(jnp.exp(-jnp.abs(z)))
    log_a = -RG_C * r * softplus
    a = jnp.exp(log_a)
    a_buf[...] = a
    b_buf[...] = jnp.sqrt(jnp.tanh(-log_a) * (1.0 + a * a)) * (ig * xc)

    def step(t, h):
        h = a_buf[pl.ds(t, 1), :] * h + b_buf[pl.ds(t, 1), :]
        h_buf[pl.ds(t, 1), :] = h
        return h

    h_fin = lax.fori_loop(0, tt, step, h_carry[...], unroll=min(8, tt))
    h_carry[...] = h_fin
    hlast_ref[...] = h_fin
    y_ref[...] = h_buf[...] * jax.nn.gelu(yg_ref[...])
    xbuf[pl.ds(halo - (CONV_WIDTH - 1), CONV_WIDTH - 1), :] = xbuf[pl.ds(halo + tt - (CONV_WIDTH - 1), CONV_WIDTH - 1), :]


def _block_diag(w):
    k, c, d = w.shape
    eye = jnp.eye(k, dtype=w.dtype)
    return (eye[:, None, :, None] * w[:, :, None, :]).reshape(k * c, k * d)


def rglru(xr, yg, conv_prev, h_prev, conv_w, conv_b, rg_wa, rg_ba, rg_wx, rg_bx, rg_lambda, tt):
    n, t, c = xr.shape
    row = pl.BlockSpec((None, tt, c), lambda b, i: (b, i, 0))
    const = lambda shape: pl.BlockSpec(shape, lambda b, i: (0,) * len(shape))
    vec = lambda a: a.reshape(1, c)
    return pl.pallas_call(
        _rglru_body,
        grid=(n, t // tt),
        in_specs=[row, row,
                  pl.BlockSpec((None, CONV_WIDTH - 1, c), lambda b, i: (b, 0, 0)),
                  pl.BlockSpec((None, 1, c), lambda b, i: (b, 0, 0)),
                  const((CONV_WIDTH, c)), const((1, c)), const((c, c)), const((1, c)), const((c, c)),
                  const((1, c)), const((1, c))],
        out_specs=[row, pl.BlockSpec((None, 1, c), lambda b, i: (b, 0, 0))],
        out_shape=[jax.ShapeDtypeStruct((n, t, c), F32), jax.ShapeDtypeStruct((n, 1, c), F32)],
        scratch_shapes=[pltpu.VMEM((tt + SUBLANES, c), F32), pltpu.VMEM((tt, c), F32), pltpu.VMEM((tt, c), F32),
                        pltpu.VMEM((tt, c), F32), pltpu.VMEM((1, c), F32)],
        compiler_params=_cparams("parallel", "arbitrary"),
        name="rglru",
    )(xr, yg, conv_prev, h_prev.reshape(n, 1, c), conv_w, vec(conv_b), _block_diag(rg_wa).astype(BF16), vec(rg_ba),
      _block_diag(rg_wx).astype(BF16), vec(rg_bx), vec(rg_lambda))


def _pack_bf16_pairs(x):
    c = x.shape[1] // 2
    a = pltpu.bitcast(x[:, :c].astype(BF16).astype(F32), U32)
    b = pltpu.bitcast(x[:, c:].astype(BF16).astype(F32), U32)
    return (a & jnp.uint32(0xFFFF0000)) | (b >> 16)


def _unpack_bf16_pairs(u):
    a = pltpu.bitcast(u & jnp.uint32(0xFFFF0000), F32)
    b = pltpu.bitcast(u << 16, F32)
    return jnp.concatenate([a, b], axis=1).astype(BF16)


def _out_router_body(attn_ref, rnn_ref, x_ref, g1_ref, sc2_ref, sh2_ref, ag_ref, rg_ref, wo_ref, n2_ref,
                     rw_ref, rb_ref, tri_ref, x1_ref, h2_ref, gate_ref, code_ref, count_ref):
    mixed = jnp.concatenate([_rms(attn_ref[...], ag_ref[...]), _rms(rnn_ref[...], rg_ref[...])], axis=1)
    x1 = x_ref[...] + g1_ref[...] * jnp.dot(mixed.astype(BF16), wo_ref[...], preferred_element_type=F32)
    x1_ref[...] = x1
    h2 = _rms(x1, n2_ref[...]) * (1.0 + sc2_ref[...]) + sh2_ref[...]
    h2_hi = h2.astype(BF16)
    h2_ref[...] = _pack_bf16_pairs(h2)
    h2_lo = (h2 - h2_hi.astype(F32)).astype(BF16)
    z = (jnp.dot(h2_hi, rw_ref[0], preferred_element_type=F32) + jnp.dot(h2_hi, rw_ref[1], preferred_element_type=F32)
         + jnp.dot(h2_lo, rw_ref[0], preferred_element_type=F32)) + rb_ref[...]
    lane = lax.broadcasted_iota(I32, z.shape, 1)
    neg_inf = -jnp.inf
    gl = jnp.where(lane < N_GROUPS, z, neg_inf)
    gmax = jnp.max(gl, axis=1, keepdims=True)
    gidx = jnp.min(jnp.where(gl == gmax, lane, LANES), axis=1, keepdims=True)
    gprob = 1.0 / jnp.sum(jnp.exp(gl - gmax), axis=1, keepdims=True)
    lo = EXPERT_LANE0 + EXPERTS_PER_GROUP * gidx
    el = jnp.where(lane >= lo, jnp.where(lane < lo + EXPERTS_PER_GROUP, z, neg_inf), neg_inf)
    m1 = jnp.max(el, axis=1, keepdims=True)
    j1 = jnp.min(jnp.where(el == m1, lane, LANES), axis=1, keepdims=True)
    el2 = jnp.where(lane == j1, neg_inf, el)
    m2 = jnp.max(el2, axis=1, keepdims=True)
    j2 = jnp.min(jnp.where(el2 == m2, lane, LANES), axis=1, keepdims=True)
    e21 = jnp.exp(m2 - m1)
    w1 = 1.0 / (1.0 + e21)
    gate_ref[...] = jnp.where(lane == j1, w1 * gprob, jnp.where(lane == j2, e21 * w1 * gprob, 0.0))
    tri = tri_ref[...]
    onehot = jnp.where(lane == gidx, 1.0, 0.0)
    ranks = jnp.dot(tri, onehot.astype(BF16), preferred_element_type=F32)
    counts = jnp.sum(onehot, axis=0, keepdims=True)
    blocks = jnp.floor((counts + (MOE_ROW_BLOCK - 0.5)) * (1.0 / MOE_ROW_BLOCK))
    blocks8 = jnp.broadcast_to(blocks, (SUBLANES, LANES)).astype(BF16)
    below = jnp.where(lax.broadcasted_iota(I32, (LANES, LANES), 0) < lax.broadcasted_iota(I32, (LANES, LANES), 1),
                      1.0, 0.0).astype(BF16)
    starts = jnp.dot(blocks8, below, preferred_element_type=F32)[0:1, :] * MOE_ROW_BLOCK
    dst = jnp.sum((ranks + starts) * onehot, axis=1, keepdims=True)
    code_ref[...] = jnp.where(lane == 0, dst, 0.0).T[0:1, :].astype(I32)
    count_ref[...] = starts.astype(I32)


def out_proj_router(attn, rnn, x, g1, sc2, sh2, attn_g, rnn_g, w_out, norm2_g, router_w, router_b, tt):
    n, t, d = x.shape
    c = attn.shape[2]
    per_row = g1.shape[1] != 1
    mod_spec = (pl.BlockSpec((None, tt, d), lambda b, i: (b, i, 0)) if per_row
                else pl.BlockSpec((None, 1, d), lambda b, i: (b, 0, 0)))
    row = lambda w_: pl.BlockSpec((None, tt, w_), lambda b, i: (b, i, 0))
    const = lambda shape: pl.BlockSpec(shape, lambda b, i: (0,) * len(shape))
    nt = t // tt
    per_tile = lambda w_: pl.BlockSpec((None, 1, w_), lambda b, i: (b * nt + i, 0, 0))
    tri = jnp.tril(jnp.ones((tt, tt), BF16), -1)
    return pl.pallas_call(
        _out_router_body,
        grid=(n, nt),
        in_specs=[row(c), row(c), row(d), mod_spec, mod_spec, mod_spec, const((1, c)), const((1, c)),
                  const(w_out.shape), const((1, d)), const(router_w.shape), const((1, LANES)), const((tt, tt))],
        out_specs=[row(d), row(d // 2), row(LANES), per_tile(tt), per_tile(LANES)],
        out_shape=[jax.ShapeDtypeStruct((n, t, d), F32), jax.ShapeDtypeStruct((n, t, d // 2), U32),
                   jax.ShapeDtypeStruct((n, t, LANES), F32), jax.ShapeDtypeStruct((n * nt, 1, tt), I32),
                   jax.ShapeDtypeStruct((n * nt, 1, LANES), I32)],
        compiler_params=_cparams("parallel", "arbitrary"),
        name="out_proj_router",
    )(attn, rnn, x, g1, sc2, sh2, attn_g.reshape(1, c), rnn_g.reshape(1, c), w_out, norm2_g.reshape(1, d),
      router_w, router_b, tri)


def split_bf16(w):
    hi = w.astype(BF16)
    return jnp.stack([hi, (w - hi.astype(F32)).astype(BF16)])


def _moe_body(dst_ref, start_ref, h2_ref, gate_ref, x1_ref, g2_ref, wgu_ref, wd_ref, fg_ref, y_ref,
              hs_ref, gs_ref, acc_ref):
    step = pl.program_id(2)
    tt = h2_ref.shape[0]
    d_exp = wd_ref.shape[1]
    rb = MOE_ROW_BLOCK

    @pl.when(step == 0)
    def _():
        hs_ref[...] = jnp.zeros(hs_ref.shape, U32)
        gs_ref[...] = jnp.zeros(gs_ref.shape, F32)

        def move_in(t, c):
            dst = dst_ref[0, t]
            hs_ref[pl.ds(dst, 1), :] = h2_ref[pl.ds(t, 1), :]
            gs_ref[pl.ds(dst, 1), :] = gate_ref[pl.ds(t, 1), :]
            return c

        lax.fori_loop(0, tt, move_in, 0, unroll=8)
        acc_ref[...] = jnp.zeros(acc_ref.shape, F32)

    lane = lax.broadcasted_iota(I32, (rb, LANES), 1)
    first = step * MOE_EXPERTS_PER_STEP
    grp = first // EXPERTS_PER_GROUP

    def block(b, c):
        rows = pl.ds(pl.multiple_of(b * rb, rb), rb)
        h = _unpack_bf16_pairs(hs_ref[rows, :])
        gates = gs_ref[rows, :]
        acc = acc_ref[rows, :]
        for k in range(MOE_EXPERTS_PER_STEP):
            gcol = jnp.sum(jnp.where(lane == first + k + EXPERT_LANE0, gates, 0.0), axis=1, keepdims=True)
            gu = jnp.dot(h, wgu_ref[k], preferred_element_type=F32)
            gpart, upart = gu[:, :d_exp], gu[:, d_exp:]
            act = (gpart * _sigmoid(gpart) * upart).astype(BF16)
            acc = acc + gcol * jnp.dot(act, wd_ref[k], preferred_element_type=F32)
        acc_ref[rows, :] = acc
        return c

    lax.fori_loop(start_ref[0, grp] // rb, start_ref[0, grp + 1] // rb, block, 0)

    @pl.when(step == pl.num_programs(2) - 1)
    def _():
        def move_out(t, c):
            y_ref[pl.ds(t, 1), :] = acc_ref[pl.ds(dst_ref[0, t], 1), :]
            return c

        lax.fori_loop(0, tt, move_out, 0, unroll=8)
        x2 = x1_ref[...] + g2_ref[...] * y_ref[...]
        y_ref[...] = _rms(x2, fg_ref[...])


def moe_final(codes, counts, h2, gates, x1, g2, w_gu, w_down, final_g, tt):
    n, t, d = x1.shape
    n_exp = w_gu.shape[0]
    nt = t // tt
    per_row = g2.shape[1] != 1
    mod_spec = (pl.BlockSpec((None, tt, d), lambda b, i, e: (b, i, 0)) if per_row
                else pl.BlockSpec((None, 1, d), lambda b, i, e: (b, 0, 0)))
    row = lambda w_: pl.BlockSpec((None, tt, w_), lambda b, i, e: (b, i, 0))
    smem_tile = lambda w_: pl.BlockSpec((None, 1, w_), lambda b, i, e: (b * nt + i, 0, 0), memory_space=pltpu.SMEM)
    sort_rows = tt + N_GROUPS * MOE_ROW_BLOCK
    eps = MOE_EXPERTS_PER_STEP
    return pl.pallas_call(
        _moe_body,
        grid=(n, nt, n_exp // eps),
        in_specs=[smem_tile(tt), smem_tile(LANES), row(d // 2), row(LANES), row(d), mod_spec,
                  pl.BlockSpec((eps,) + w_gu.shape[1:], lambda b, i, e: (e, 0, 0)),
                  pl.BlockSpec((eps,) + w_down.shape[1:], lambda b, i, e: (e, 0, 0)),
                  pl.BlockSpec((1, d), lambda b, i, e: (0, 0))],
        out_specs=row(d),
        out_shape=jax.ShapeDtypeStruct((n, t, d), F32),
        scratch_shapes=[pltpu.VMEM((sort_rows, d // 2), U32),
                        pltpu.VMEM((sort_rows, LANES), F32),
                        pltpu.VMEM((sort_rows, d), F32)],
        compiler_params=pltpu.CompilerParams(dimension_semantics=("parallel", "arbitrary", "arbitrary"),
                                             vmem_limit_bytes=VMEM_LIMIT_MOE),
        name="moe_final",
    )(codes, counts, h2, gates, x1, g2, w_gu, w_down, final_g.reshape(1, d))


def _tile(t, pref):
    return pref if t % pref == 0 else t


def _layer(x, mods, pos, attend, conv_prev, h_prev, n_seq, lw, final_g):
    sh1, sc1, g1, sh2, sc2, g2 = mods
    n, t, d = x.shape
    t_seq = n * t // n_seq
    seqs = lambda a: a.reshape(n_seq, t_seq, a.shape[-1])
    q, k, v, kb, vt, iq, ikw, ikb, xr, yg, ik = in_proj(x, sc1, sh1, lw["norm1_g"], lw["w_in"], rope_tables(pos),
                                                         _tile(t, 512))
    attn = attend(q, iq, ikw, k, v, kb, vt, ikb)
    rnn, h_new = rglru(seqs(xr), seqs(yg), conv_prev, h_prev, lw["conv_w"], lw["conv_b"], lw["rg_wa"], lw["rg_ba"],
                       lw["rg_wx"], lw["rg_bx"], lw["rg_lambda"], _tile(t_seq, 512))
    moe_tile = _tile(t, 1024)
    x1, h2, gates, codes, counts = out_proj_router(
        attn, rnn.reshape(xr.shape), x, g1, sc2, sh2, lw["attn_out_g"], lw["rnn_out_g"], lw["w_out"], lw["norm2_g"],
        lw["router_w"], lw["router_b"], moe_tile)
    y = moe_final(codes, counts, h2, gates, x1, g2, lw["w_gu"], lw["w_down"], final_g, moe_tile)
    conv_new = seqs(xr)[:, t_seq - (CONV_WIDTH - 1):, :]
    return (seqs(y), seqs(k).reshape(n_seq, t_seq, N_KV_HEADS, HEAD_DIM),
            seqs(v).reshape(n_seq, t_seq, N_KV_HEADS, HEAD_DIM), seqs(ik), conv_new, h_new[:, 0, :])


def kernel(x_prompt, x_sample, cache_k, cache_v, cache_idx_k, state_conv, state_rglru, page_table, c_prompt,
           c_sample, ada_w, ada_b, norm1_g, w_in, conv_w, conv_b, rg_wa, rg_ba, rg_wx, rg_bx, rg_lambda, attn_out_g,
           rnn_out_g, w_out, norm2_g, router_grp_w, router_grp_b, router_exp_w, router_exp_b, exp_w_gate, exp_w_up,
           exp_w_down, final_g):
    depth = ada_w.shape[0]
    assert depth == 1, "this implementation handles the single-layer configuration"
    n_p, t_p, d = x_prompt.shape
    n_s, t_s, _ = x_sample.shape
    n_pages = page_table.shape[1]
    past = n_pages * PAGE_SIZE
    l = 0
    router_w = jnp.concatenate([router_grp_w[l], router_exp_w[l],
                                jnp.zeros((d, LANES - N_GROUPS - N_EXPERTS), F32)], axis=1)
    router_b = jnp.concatenate([router_grp_b[l], router_exp_b[l],
                                jnp.zeros((LANES - N_GROUPS - N_EXPERTS,), F32)]).reshape(1, LANES)
    lw = dict(norm1_g=norm1_g[l], w_in=prep_w_in(w_in[l]), conv_w=conv_w[l], conv_b=conv_b[l], rg_wa=rg_wa[l],
              rg_ba=rg_ba[l], rg_wx=rg_wx[l], rg_bx=rg_bx[l], rg_lambda=rg_lambda[l], attn_out_g=attn_out_g[l],
              rnn_out_g=rnn_out_g[l], w_out=w_out[l].astype(BF16), norm2_g=norm2_g[l],
              router_w=split_bf16(router_w), router_b=router_b,
              w_gu=jnp.concatenate([exp_w_gate[l], exp_w_up[l]], axis=2).astype(BF16),
              w_down=exp_w_down[l].astype(BF16))

    mod = ada_mod(jnp.concatenate([c_prompt, c_sample], axis=0), ada_w[l], ada_b[l])
    mods_p = [m[:n_p, None, :] for m in jnp.split(mod, 6, axis=1)]
    mods_s = [jnp.repeat(m[n_p:], t_s, axis=0)[None] for m in jnp.split(mod, 6, axis=1)]

    def attend_prompt(q, iq, ikw, k, v, kb, vt, ikb):
        iw_t = jnp.swapaxes(ikw[:, :, IDX_HEAD_DIM:IDX_HEAD_DIM + N_IDX_HEADS], 1, 2)
        return prompt_attention(q, iq, iw_t, kb, vt, ikb, _tile(t_p, 256))

    def attend_sample(q, iq, ikw, k, v, kb, vt, ikb):
        seqs = lambda a: a.reshape(n_s, t_s, a.shape[-1])
        n_pool = cache_k.shape[1]
        feat_major = lambda c: jnp.swapaxes(c.reshape(n_pool, PAGE_SIZE, -1), 1, 2)
        out = sample_attention(page_table, seqs(q), seqs(iq), seqs(ikw), seqs(k), seqs(v),
                               feat_major(cache_k[l]), feat_major(cache_v[l]), feat_major(cache_idx_k[l]))
        return out.reshape(1, n_s * t_s, N_HEADS * HEAD_DIM)

    c_rnn = conv_w.shape[2]
    ys, ks, vs, iks, cvs, hs = _layer(
        x_sample.reshape(1, n_s * t_s, d), mods_s, jnp.tile(past + jnp.arange(t_s), n_s), attend_sample,
        state_conv[l], state_rglru[l], n_s, lw, final_g)
    yp, kp, vp, ikp, cvp, hp = _layer(
        x_prompt, mods_p, jnp.arange(t_p), attend_prompt,
        jnp.zeros((n_p, CONV_WIDTH - 1, c_rnn), F32), jnp.zeros((n_p, c_rnn), F32), n_p, lw, final_g)

    st = lambda a: a[None]
    return (yp, ys, st(kp), st(vp), st(ikp), st(cvp), st(hp), st(ks), st(vs), st(iks), st(cvs), st(hs))
```
